```python
import math
import jax
import jax.numpy as jnp
from jax import lax
import numpy as np

D_MODEL = 1024
BATCH = 16
SEQ = 4096
DEPTH = 2

MEM_LEN = 256
HEAD_DIM = 64
ROPE_THETA = 500000.0
ROPE_FRACTION = 4
NORM_EPS = 1e-5

DIL_PATTERNS = ((128, 1), (512, 4), (2048, 16))
DIL_GROUPS = len(DIL_PATTERNS)
DIL_HEADS = 4
DIL_BLOCK = 128
DIL_WIDTH = DIL_HEADS * HEAD_DIM
DIFF_HEADS = 4
DIFF_WIDTH = DIFF_HEADS * 2 * HEAD_DIM
Q_BLOCK = 128
HGRN_HEADS = 4
HGRN_DK = 64
HGRN_DV = 64
HGRN_CHUNK = 64
HGRN_KW = HGRN_HEADS * HGRN_DK
HGRN_VW = HGRN_HEADS * HGRN_DV
RWKV_HEADS = 4
RWKV_HEAD_DIM = 64
RWKV_WIDTH = RWKV_HEADS * RWKV_HEAD_DIM
RWKV_DECAY_LORA = 64
RWKV_A_LORA = 64
RWKV_MV_LORA = 32
RWKV_GATE_LORA = 128
RWKV_GN_EPS = 1e-5 * RWKV_HEAD_DIM
N_BRANCH = 4
SEG_A = DIL_GROUPS * 3 * DIL_WIDTH
SEG_B = 3 * DIFF_WIDTH
SEG_C = 2 * HGRN_KW + 2 * HGRN_VW
SEG_D = 3 * RWKV_WIDTH + RWKV_DECAY_LORA + RWKV_A_LORA + RWKV_GATE_LORA
SEG_G = N_BRANCH * D_MODEL
SEGMENTS = (SEG_A, SEG_B, SEG_C, SEG_D, SEG_G)
N_IN = SEG_A + SEG_B + SEG_C + SEG_D + SEG_G
MEM_HEADS = 4
MEM_HEAD_DIM = D_MODEL // MEM_HEADS
D_FF = 2816
CONV_WIDTH = 3

kernel_name = 'hybrid_gated_parallel_mixer_trunk'
F32 = jnp.float32


def split_last(t, sizes):
    return jnp.split(t, np.cumsum(sizes)[:-1].tolist(), axis=-1)


def to_heads(t, n_heads):
    return t.reshape(t.shape[:-1] + (n_heads, t.shape[-1] // n_heads))


def rms_norm(x, g, eps=NORM_EPS):
    xf = x.astype(F32)
    y = xf * lax.rsqrt(jnp.mean(xf * xf, axis=-1, keepdims=True) + eps)
    return (y * g.astype(F32)).astype(x.dtype)


def partial_rotary(t, positions):
    rot = t.shape[-1] // ROPE_FRACTION
    half = rot // 2
    inv_freq = ROPE_THETA ** (-jnp.arange(half, dtype=F32) / half)
    ang = positions.astype(F32)[:, :, None, None] * inv_freq
    cos, sin = jnp.cos(ang), jnp.sin(ang)
    tf = t.astype(F32)
    t1, t2 = tf[..., :half], tf[..., half:rot]
    out = jnp.concatenate([t1 * cos - t2 * sin, t2 * cos + t1 * sin, tf[..., rot:]], axis=-1)
    return out.astype(t.dtype)


def dilated_window_attention(q, k, v, window, dilation):
    B, S, H, dh = q.shape
    span = window // dilation
    blk = DIL_BLOCK
    unit = dilation * blk
    s_pad = -(-S // unit) * unit
    sub_len = s_pad // dilation
    nb = sub_len // blk

    def to_blocks(t):
        t = jnp.pad(t, ((0, 0), (0, s_pad - S), (0, 0), (0, 0)))
        t = t.reshape(B, sub_len, dilation, H, dh).transpose(0, 2, 3, 1, 4)
        return t.reshape(B, dilation, H, nb, blk, dh)

    def with_prev(t):
        prev = jnp.pad(t[:, :, :, :-1], ((0, 0), (0, 0), (0, 0), (1, 0), (0, 0), (0, 0)))
        return jnp.concatenate([prev, t], axis=4)

    qb = to_blocks(q)
    kb = with_prev(to_blocks(k))
    vb = with_prev(to_blocks(v))
    s = jnp.einsum('brhnqd,brhnkd->brhnqk', qb, kb).astype(F32) * dh ** -0.5
    qi = jnp.arange(blk)[None, :, None]
    kj = jnp.arange(2 * blk)[None, None, :]
    bi = jnp.arange(nb)[:, None, None]
    dist = qi + blk - kj
    valid = (dist >= 0) & (dist <= span) & (bi * blk + kj >= blk)
    s = jnp.where(valid, s, -jnp.inf)
    m = jnp.max(s, axis=-1, keepdims=True)
    p = jnp.exp(s - m)
    den = jnp.sum(p, axis=-1, keepdims=True)
    o = jnp.einsum('brhnqk,brhnkd->brhnqd', (p / den).astype(v.dtype), vb)
    lse = (m + jnp.log(den))[..., 0]
    o = o.reshape(B, dilation, H, sub_len, dh).transpose(0, 3, 1, 2, 4).reshape(B, s_pad, H, dh)
    lse = lse.reshape(B, dilation, H, sub_len).transpose(0, 3, 1, 2).reshape(B, s_pad, H)
    return o[:, :S], lse[:, :S]


def dilated_branch(seg, positions):
    B, S, _ = seg.shape
    qkv = seg.reshape(B, S, DIL_GROUPS, 3, DIL_HEADS, HEAD_DIM)
    outs, lses = [], []
    for g, (window, dilation) in enumerate(DIL_PATTERNS):
        q = partial_rotary(qkv[:, :, g, 0], positions)
        k = partial_rotary(qkv[:, :, g, 1], positions)
        o, lse = dilated_window_attention(q, k, qkv[:, :, g, 2], window, dilation)
        outs.append(o)
        lses.append(lse)
    wts = jax.nn.softmax(jnp.stack(lses, 0), axis=0).astype(seg.dtype)
    o = jnp.einsum('gbsh,gbshd->bshd', wts, jnp.stack(outs, 0))
    return o.reshape(B, S, DIL_WIDTH)


def differential_attention(q1, q2, k1, k2, v, lam):
    B, S, H, d = q1.shape
    nb = S // Q_BLOCK
    scale = d ** -0.5
    kpos = jnp.arange(S)

    def blocks(t):
        return jnp.moveaxis(t.reshape(B, nb, Q_BLOCK, H, d), 1, 0)

    def probs(qx, kx, mask):
        s = jnp.einsum('bqhd,bkhd->bhqk', qx, kx).astype(F32) * scale
        return jax.nn.softmax(jnp.where(mask, s, -jnp.inf), axis=-1)

    def one_block(args):
        i, qa, qb = args
        mask = kpos[None, :] <= (i * Q_BLOCK + jnp.arange(Q_BLOCK))[:, None]
        p = probs(qa, k1, mask) - lam * probs(qb, k2, mask)
        return jnp.einsum('bhqk,bkhe->bqhe', p.astype(v.dtype), v)

    o = lax.map(one_block, (jnp.arange(nb), blocks(q1), blocks(q2)))
    return jnp.moveaxis(o, 0, 1).reshape(B, S, H, v.shape[-1])


def diff_branch(seg, positions, lam_vecs, norm_g, lam_init):
    B, S, _ = seg.shape
    q, k, v = split_last(seg, (DIFF_WIDTH, DIFF_WIDTH, DIFF_WIDTH))
    q = q.reshape(B, S, DIFF_HEADS, 2, HEAD_DIM)
    k = k.reshape(B, S, DIFF_HEADS, 2, HEAD_DIM)
    q1 = partial_rotary(q[:, :, :, 0], positions)
    q2 = partial_rotary(q[:, :, :, 1], positions)
    k1 = partial_rotary(k[:, :, :, 0], positions)
    k2 = partial_rotary(k[:, :, :, 1], positions)
    v = v.reshape(B, S, DIFF_HEADS, 2 * HEAD_DIM)
    lv = lam_vecs.astype(F32)
    lam = jnp.exp(jnp.sum(lv[0] * lv[1])) - jnp.exp(jnp.sum(lv[2] * lv[3])) + lam_init
    o = differential_attention(q1, q2, k1, k2, v, lam)
    o = rms_norm(o, norm_g) * (1.0 - lam_init)
    return o.reshape(B, S, DIFF_WIDTH)


def hgrn2_chunked(q, log_f, k, i):
    B, S, H, dk = q.shape
    dv = i.shape[-1]
    C = HGRN_CHUNK
    n = S // C

    def chunks(t):
        return t.astype(F32).reshape(B, n, C, H, t.shape[-1]).transpose(1, 0, 3, 2, 4)

    causal = jnp.tril(jnp.ones((C, C), dtype=bool))[:, :, None]

    def step(state, xs):
        qc, lfc, kc, ic = xs
        b = jnp.cumsum(lfc, axis=2)
        o_inter = jnp.einsum('bhck,bhkv->bhcv', qc * jnp.exp(b), state)
        diff = b[:, :, :, None, :] - b[:, :, None, :, :]
        decay = jnp.exp(jnp.where(causal, diff, -jnp.inf))
        att = jnp.einsum('bhtk,bhtsk,bhsk->bhts', qc, decay, kc)
        o = o_inter + jnp.einsum('bhts,bhsv->bhtv', att, ic)
        b_last = b[:, :, -1:, :]
        state = (jnp.exp(b_last[:, :, 0, :])[..., None] * state
                 + jnp.einsum('bhsk,bhsv->bhkv', kc * jnp.exp(b_last - b), ic))
        return state, o

    s0 = jnp.zeros((B, H, dk, dv), F32)
    _, o = lax.scan(step, s0, (chunks(q), chunks(log_f), chunks(k), chunks(i)))
    return o.transpose(1, 0, 3, 2, 4).reshape(B, S, H, dv)


def hgrn_branch(seg, lb, norm_g):
    B, S, _ = seg.shape
    q, f, i, g = split_last(seg, (HGRN_KW, HGRN_KW, HGRN_VW, HGRN_VW))
    lb = lb.astype(F32)
    f = f.astype(F32)
    log_f = jnp.logaddexp(jnp.log(lb), jnp.log1p(-lb) + jax.nn.log_sigmoid(f))
    k = (1.0 - lb) * jax.nn.sigmoid(-f)
    o = hgrn2_chunked(to_heads(jax.nn.silu(q), HGRN_HEADS), to_heads(log_f, HGRN_HEADS),
                      to_heads(k, HGRN_HEADS), to_heads(i, HGRN_HEADS))
    o = rms_norm(o, norm_g).reshape(B, S, HGRN_VW) * jax.nn.silu(g.astype(F32))
    return o.astype(seg.dtype)


def rwkv7_scan(r, w, k, v, kk, a):
    B, S, H, d = r.shape

    def step(state, xs):
        rt, wt, kt, vt, kkt, at = xs
        sa = jnp.einsum('bhvk,bhk->bhv', state, -kkt)
        state = (state * wt[:, :, None, :] + sa[..., None] * (kkt * at)[:, :, None, :]
                 + vt[..., None] * kt[:, :, None, :])
        return state, jnp.einsum('bhvk,bhk->bhv', state, rt)

    xs = tuple(jnp.moveaxis(t.astype(F32), 1, 0) for t in (r, w, k, v, kk, a))
    _, y = lax.scan(step, jnp.zeros((B, H, d, d), F32), xs)
    return jnp.moveaxis(y, 0, 1)


def rwkv_branch(seg, mu, w0, w2, a0, a2, g2, k_k, k_a, r_k, lnx_g, lnx_b, v_first, vmix):
    B, S, _ = seg.shape
    prev = jnp.pad(seg, ((0, 0), (1, 0), (0, 0)))[:, :-1]
    seg = seg + (prev - seg) * mu
    r, k, v, w_low, a_low, g_low = split_last(
        seg, (RWKV_WIDTH, RWKV_WIDTH, RWKV_WIDTH, RWKV_DECAY_LORA, RWKV_A_LORA, RWKV_GATE_LORA))
    w = -jax.nn.softplus(-(w0 + jnp.tanh(w_low) @ w2).astype(F32)) - 0.5
    decay = jnp.exp(-jnp.exp(w))
    a = jax.nn.sigmoid(a0 + a_low @ a2)
    g = jax.nn.sigmoid(g_low) @ g2
    kk = to_heads(k * k_k, RWKV_HEADS).astype(F32)
    kk = kk / jnp.maximum(jnp.sqrt(jnp.sum(kk * kk, axis=-1, keepdims=True)), 1e-12)
    k = k * (1.0 + (a - 1.0) * k_a)
    if vmix is None:
        v_first = v
    else:
        v0, v1, v2 = vmix
        v = v + (v_first - v) * jax.nn.sigmoid(v0 + (v @ v1) @ v2)
    rh, kh, vh, ah = (to_heads(t, RWKV_HEADS) for t in (r, k, v, a))
    y = rwkv7_scan(rh, to_heads(decay, RWKV_HEADS), kh, vh, kk, ah)
    mean = jnp.mean(y, axis=-1, keepdims=True)
    var = jnp.mean(jnp.square(y - mean), axis=-1, keepdims=True)
    y = ((y - mean) * lax.rsqrt(var + RWKV_GN_EPS)).reshape(B, S, RWKV_WIDTH) * lnx_g + lnx_b
    bonus = (jnp.sum(rh * kh * r_k, axis=-1, keepdims=True) * vh).reshape(B, S, RWKV_WIDTH)
    y = (y + bonus) * g
    return y.astype(seg.dtype), v_first


def setup_inputs(seed: int = 0) -> dict:
    key = jax.random.key(seed)
    keys = iter(jax.random.split(key, 64))
    L, D, W = DEPTH, D_MODEL, RWKV_WIDTH

    def normal(shape, scale):
        return scale * jax.random.normal(next(keys), shape, F32)

    def uniform(shape, lo, hi):
        return jax.random.uniform(next(keys), shape, F32, lo, hi)

    def gain(shape):
        return 1.0 + normal(shape, 0.02)

    def dense(shape):
        return normal(shape, shape[-2] ** -0.5)

    x = normal((BATCH, SEQ, D), 1.0)
    mem = normal((BATCH, MEM_LEN, D), 1.0)
    start = jax.random.randint(next(keys), (BATCH, 1), 0, 1024, jnp.int32)
    positions = start + jnp.arange(SEQ, dtype=jnp.int32)[None, :]
    conv_w = normal((L, CONV_WIDTH, 2 * D_FF), 0.2).at[:, CONV_WIDTH - 1].add(1.0)
    return {
        'x': x,
        'mem': mem,
        'positions': positions,
        'mix_norm_g': gain((L, D)),
        'w_in': dense((L, D, N_IN)),
        'diff_lam': normal((L, 4, HEAD_DIM), 0.1),
        'diff_norm_g': gain((L, 2 * HEAD_DIM)),
        'hgrn_lb_logits': normal((L, HGRN_KW), 1.0),
        'hgrn_norm_g': gain((L, HGRN_DV)),
        'rwkv_mu': uniform((L, SEG_D), 0.0, 1.0),
        'rwkv_w0': uniform((L, W), -5.0, 0.0),
        'rwkv_w2': normal((L, RWKV_DECAY_LORA, W), 0.1 * RWKV_DECAY_LORA ** -0.5),
        'rwkv_a0': normal((L, W), 0.1),
        'rwkv_a2': normal((L, RWKV_A_LORA, W), 0.1 * RWKV_A_LORA ** -0.5),
        'rwkv_g2': dense((L, RWKV_GATE_LORA, W)),
        'rwkv_k_k': 0.85 + normal((L, W), 0.05),
        'rwkv_k_a': 1.0 + normal((L, W), 0.05),
        'rwkv_r_k': normal((L, RWKV_HEADS, RWKV_HEAD_DIM), 0.1),
        'rwkv_lnx_g': gain((L, W)),
        'rwkv_lnx_b': normal((L, W), 0.02),
        'rwkv_v0': 1.0 + normal((L - 1, W), 0.1),
        'rwkv_v1': normal((L - 1, W, RWKV_MV_LORA), 0.1 * W ** -0.5),
        'rwkv_v2': normal((L - 1, RWKV_MV_LORA, W), 0.1 * RWKV_MV_LORA ** -0.5),
        'p_a': dense((L, DIL_WIDTH, D)),
        'p_b': dense((L, DIFF_WIDTH, D)),
        'p_c': dense((L, HGRN_VW, D)),
        'p_d': dense((L, RWKV_WIDTH, D)),
        'w_mix_out': dense((L, D, D)),
        'mem_q_norm_g': gain((L, D)),
        'mem_kv_norm_g': gain((L, D)),
        'w_mem_q': dense((L, D, D)),
        'w_mem_kv': dense((L, D, 2 * D)),
        'w_mem_o': dense((L, D, D)),
        'ffn_norm_g': gain((L, D)),
        'w_ffn_in': dense((L, D, 2 * D_FF)),
        'ffn_conv_w': conv_w,
        'ffn_conv_b': normal((L, 2 * D_FF), 0.02),
        'w_ffn_out': dense((L, D_FF, D)),
        'final_norm_g': gain((D,)),
    }


def reference(x, mem, positions, mix_norm_g, w_in, diff_lam, diff_norm_g, hgrn_lb_logits, hgrn_norm_g,
              rwkv_mu, rwkv_w0, rwkv_w2, rwkv_a0, rwkv_a2, rwkv_g2, rwkv_k_k, rwkv_k_a, rwkv_r_k,
              rwkv_lnx_g, rwkv_lnx_b, rwkv_v0, rwkv_v1, rwkv_v2, p_a, p_b, p_c, p_d, w_mix_out,
              mem_q_norm_g, mem_kv_norm_g, w_mem_q, w_mem_kv, w_mem_o,
              ffn_norm_g, w_ffn_in, ffn_conv_w, ffn_conv_b, w_ffn_out, final_norm_g):
    B, S, D = x.shape
    M = mem.shape[1]
    lb_all = jnp.cumsum(jax.nn.softmax(hgrn_lb_logits.astype(F32), axis=0), axis=0)
    lb_all = lb_all - lb_all[0:1]
    v_first = None
    for l in range(DEPTH):
        lam_init = 0.8 - 0.6 * math.exp(-0.3 * l)
        h = rms_norm(x, mix_norm_g[l])
        seg_a, seg_b, seg_c, seg_d, seg_g = split_last(h @ w_in[l], SEGMENTS)
        y_a = dilated_branch(seg_a, positions)
        y_b = diff_branch(seg_b, positions, diff_lam[l], diff_norm_g[l], lam_init)
        y_c = hgrn_branch(seg_c, lb_all[l], hgrn_norm_g[l])
        vmix = None if l == 0 else (rwkv_v0[l - 1], rwkv_v1[l - 1], rwkv_v2[l - 1])
        y_d, v_first = rwkv_branch(seg_d, rwkv_mu[l], rwkv_w0[l], rwkv_w2[l], rwkv_a0[l], rwkv_a2[l],
                                   rwkv_g2[l], rwkv_k_k[l], rwkv_k_a[l], rwkv_r_k[l], rwkv_lnx_g[l],
                                   rwkv_lnx_b[l], v_first, vmix)
        gates = jax.nn.sigmoid(seg_g.reshape(B, S, N_BRANCH, D))
        merged = (gates[:, :, 0] * (y_a @ p_a[l]) + gates[:, :, 1] * (y_b @ p_b[l])
                  + gates[:, :, 2] * (y_c @ p_c[l]) + gates[:, :, 3] * (y_d @ p_d[l]))
        x = x + merged @ w_mix_out[l]
        hq = to_heads(rms_norm(x, mem_q_norm_g[l]) @ w_mem_q[l], MEM_HEADS)
        kv = (rms_norm(mem, mem_kv_norm_g[l]) @ w_mem_kv[l]).reshape(B, M, 2, MEM_HEADS, MEM_HEAD_DIM)
        s = jnp.einsum('bshd,bmhd->bhsm', hq, kv[:, :, 0]).astype(F32) * MEM_HEAD_DIM ** -0.5
        p = jax.nn.softmax(s, axis=-1).astype(x.dtype)
        o = jnp.einsum('bhsm,bmhd->bshd', p, kv[:, :, 1]).reshape(B, S, D)
        x = x + o @ w_mem_o[l]
        u = rms_norm(x, ffn_norm_g[l]) @ w_ffn_in[l]
        u_pad = jnp.pad(u, ((0, 0), (CONV_WIDTH - 1, 0), (0, 0)))
        conv = ffn_conv_b[l]
        for j in range(CONV_WIDTH):
            conv = conv + u_pad[:, j:j + S] * ffn_conv_w[l, j]
        gate, val = jnp.split(conv, 2, axis=-1)
        x = x + (jax.nn.silu(gate) * val) @ w_ffn_out[l]
    return rms_norm(x, final_norm_g)
```

```python
import functools
import math

import jax
import jax.numpy as jnp
from jax import lax
from jax.experimental import pallas as pl
from jax.experimental.pallas import tpu as pltpu

F32 = jnp.float32
BF16 = jnp.bfloat16

NORM_EPS = 1e-5
HEAD_DIM = 64
ROPE_THETA = 500000.0
ROPE_DIMS = HEAD_DIM // 4
DIL_PATTERNS = ((128, 1), (512, 4), (2048, 16))
DIL_BLOCK = 128
N_HEADS = 4
MIX_W = N_HEADS * HEAD_DIM
DIFF_W = 2 * MIX_W
SEG_A = 3 * 3 * MIX_W
SEG_B = 3 * DIFF_W
SEG_C = 4 * MIX_W
SEG_D = 3 * MIX_W + 64 + 64 + 128
RWKV_GN_EPS = 1e-5 * HEAD_DIM
CHUNK = 64
LANES = 128
VMEM_LIMIT = 56 * 1024 * 1024


def _cparams(n_axes):
    return pltpu.CompilerParams(dimension_semantics=("arbitrary",) * n_axes,
                                vmem_limit_bytes=VMEM_LIMIT)


def _dot(a, b):
    return jnp.dot(a, b, preferred_element_type=F32)


def _dot_nt(a, b):
    return lax.dot_general(a, b, (((1,), (1,)), ((), ())), preferred_element_type=F32)


def _dot_tn(a, b):
    return lax.dot_general(a, b, (((0,), (0,)), ((), ())), preferred_element_type=F32)


def _dot_f32(a, b):
    return jnp.dot(a, b, precision=lax.Precision.HIGHEST, preferred_element_type=F32)


def _sigmoid(x):
    return 1.0 / (1.0 + jnp.exp(-x))


def _softplus(x):
    return jnp.maximum(x, 0.0) + jnp.log1p(jnp.exp(-jnp.abs(x)))


def _rms(x, g):
    ms = jnp.mean(x * x, axis=-1, keepdims=True)
    return x * lax.rsqrt(ms + NORM_EPS) * g


def _head_sum_matrix(width):
    r = lax.broadcasted_iota(jnp.int32, (width, width), 0) // HEAD_DIM
    c = lax.broadcasted_iota(jnp.int32, (width, width), 1) // HEAD_DIM
    return r == c


def _cumsum_rows(x):
    n = x.shape[0]
    row = lax.broadcasted_iota(jnp.int32, x.shape, 0)
    d = 1
    while d < n:
        x = x + jnp.where(row >= d, pltpu.roll(x, d, 0), 0.0)
        d *= 2
    return x


def _norm_matmul_body(*refs, rope):
    if rope:
        x_ref, g_ref, w_ref, pos_ref, invf_ref, o_ref, hn_ref, cos_ref, sa_ref, sb_ref = refs
    else:
        x_ref, g_ref, w_ref, o_ref, hn_ref = refs
    j = pl.program_id(1)

    @pl.when(j == 0)
    def _():
        hn_ref[...] = _rms(x_ref[...], g_ref[...]).astype(BF16)
        if rope:
            ang = pos_ref[...].astype(F32) * invf_ref[...]
            d = lax.broadcasted_iota(jnp.int32, (1, LANES), 1) % HEAD_DIM
            s = jnp.sin(ang)
            cos_ref[...] = jnp.cos(ang)
            sa_ref[...] = jnp.where(d < ROPE_DIMS // 2, -s, 0.0)
            sb_ref[...] = jnp.where((d >= ROPE_DIMS // 2) & (d < ROPE_DIMS), s, 0.0)

    acc = _dot(hn_ref[...], w_ref[...])
    if not rope:
        o_ref[...] = acc
        return

    @pl.when(j % 3 < 2)
    def _():
        half = ROPE_DIMS // 2
        for c in range(acc.shape[1] // LANES):
            t = acc[:, c * LANES:(c + 1) * LANES]
            o_ref[:, c * LANES:(c + 1) * LANES] = (
                t * cos_ref[...] + pltpu.roll(t, LANES - half, 1) * sa_ref[...]
                + pltpu.roll(t, half, 1) * sb_ref[...])

    @pl.when(j % 3 == 2)
    def _():
        o_ref[...] = acc


def _norm_matmul(x2, g, w_bf, tn, rope=None):
    T, D = x2.shape
    N = w_bf.shape[1]
    tm = min(T, 1024)
    in_specs = [pl.BlockSpec((tm, D), lambda i, j: (i, 0)),
                pl.BlockSpec((1, D), lambda i, j: (0, 0)),
                pl.BlockSpec((D, tn), lambda i, j: (0, j))]
    args = [x2, g.reshape(1, D), w_bf]
    scratch = [pltpu.VMEM((tm, D), BF16)]
    if rope is not None:
        in_specs += [pl.BlockSpec((tm, 1), lambda i, j: (i, 0)),
                     pl.BlockSpec((1, LANES), lambda i, j: (0, 0))]
        args += list(rope)
        scratch += [pltpu.VMEM((tm, LANES), F32)] * 3
    return pl.pallas_call(
        functools.partial(_norm_matmul_body, rope=rope is not None),
        grid=(T // tm, N // tn),
        in_specs=in_specs,
        out_specs=pl.BlockSpec((tm, tn), lambda i, j: (i, j)),
        out_shape=jax.ShapeDtypeStruct((T, N), F32),
        scratch_shapes=scratch,
        compiler_params=_cparams(2),
        name="norm_matmul_rope" if rope is not None else "norm_matmul",
    )(*args)


def _dilated_body(q_ref, kc_ref, kp_ref, vc_ref, vp_ref, o_ref, lse_ref):
    n = pl.program_id(2)
    blk = DIL_BLOCK
    q = q_ref[0] * (HEAD_DIM ** -0.5)
    kc = kc_ref[0].astype(BF16)
    kp = kp_ref[0].astype(BF16)
    vc = vc_ref[0].astype(BF16)
    vp = vp_ref[0].astype(BF16)
    head = lax.broadcasted_iota(jnp.int32, (1, MIX_W), 1) // HEAD_DIM
    qi = lax.broadcasted_iota(jnp.int32, (blk, blk), 0)
    kj = lax.broadcasted_iota(jnp.int32, (blk, blk), 1)
    cur_ok = kj <= qi
    prev_ok = (kj - qi) >= jnp.where(n > 0, 0, 2 * blk)
    o_acc = jnp.zeros((blk, MIX_W), F32)
    lse_acc = jnp.zeros((blk, MIX_W), F32)
    for h in range(N_HEADS):
        hm = head == h
        qh = jnp.where(hm, q, 0.0).astype(BF16)
        sc = jnp.where(cur_ok, _dot_nt(qh, kc), -jnp.inf)
        sp = jnp.where(prev_ok, _dot_nt(qh, kp), -jnp.inf)
        m = jnp.maximum(jnp.max(sc, axis=-1, keepdims=True), jnp.max(sp, axis=-1, keepdims=True))
        pc = jnp.exp(sc - m)
        pp = jnp.exp(sp - m)
        den = jnp.sum(pc, axis=-1, keepdims=True) + jnp.sum(pp, axis=-1, keepdims=True)
        inv = 1.0 / den
        oh = _dot((pc * inv).astype(BF16), vc) + _dot((pp * inv).astype(BF16), vp)
        o_acc = jnp.where(hm, oh, o_acc)
        lse_acc = jnp.where(hm, m + jnp.log(den), lse_acc)
    o_ref[0] = o_acc
    lse_ref[0] = lse_acc


def _dilated_group(seg_a, g, dilation):
    B, S, _ = seg_a.shape
    sub = S // dilation
    nb = sub // DIL_BLOCK
    a = seg_a.reshape(B, sub, dilation * SEG_A)
    ncol = SEG_A // MIX_W

    def cmap(which, prev):
        def f(b, r, n):
            row = jnp.maximum(n - 1, 0) if prev else n
            return (b, row, r * ncol + g * 3 + which)
        return f

    spec = lambda which, prev: pl.BlockSpec((1, DIL_BLOCK, MIX_W), cmap(which, prev))
    out_spec = pl.BlockSpec((1, DIL_BLOCK, MIX_W), lambda b, r, n: (b, n, r))
    o, lse = pl.pallas_call(
        _dilated_body,
        grid=(B, dilation, nb),
        in_specs=[spec(0, False), spec(1, False), spec(1, True), spec(2, False), spec(2, True)],
        out_specs=[out_spec, out_spec],
        out_shape=[jax.ShapeDtypeStruct((B, sub, dilation * MIX_W), F32)] * 2,
        compiler_params=_cparams(3),
        name="dilated_attention",
    )(a, a, a, a, a)
    return o.reshape(B, S, MIX_W), lse.reshape(B, S, MIX_W)


def _diff_body(q_ref, k_ref, v_ref, lam_ref, g_ref, o_ref, kbf_ref, vbf_ref, *, tq, lam_init):
    qi = pl.program_id(2)

    @pl.when(qi == 0)
    def _():
        kbf_ref[...] = k_ref[0].astype(BF16)
        vbf_ref[...] = v_ref[0].astype(BF16)

    q = q_ref[0] * (HEAD_DIM ** -0.5)
    lane = lax.broadcasted_iota(jnp.int32, (1, 2 * HEAD_DIM), 1)
    q1 = jnp.where(lane < HEAD_DIM, q, 0.0).astype(BF16)
    q2 = jnp.where(lane >= HEAD_DIM, q, 0.0).astype(BF16)
    row = lax.broadcasted_iota(jnp.int32, (tq, tq), 0)
    col = lax.broadcasted_iota(jnp.int32, (tq, tq), 1)

    def update(s, m, l, acc, v):
        m_new = jnp.maximum(m, jnp.max(s, axis=-1, keepdims=True))
        alpha = jnp.exp(m - m_new)
        p = jnp.exp(s - m_new)
        l = l * alpha + jnp.sum(p, axis=-1, keepdims=True)
        acc = acc * alpha + _dot(p.astype(BF16), v)
        return m_new, l, acc

    def step(kb, carry, diagonal):
        m1, l1, a1, m2, l2, a2 = carry
        rows = pl.ds(pl.multiple_of(kb * tq, tq), tq)
        k = kbf_ref[rows, :]
        v = vbf_ref[rows, :]
        s1 = _dot_nt(q1, k)
        s2 = _dot_nt(q2, k)
        if diagonal:
            s1 = jnp.where(col <= row, s1, -jnp.inf)
            s2 = jnp.where(col <= row, s2, -jnp.inf)
        m1, l1, a1 = update(s1, m1, l1, a1, v)
        m2, l2, a2 = update(s2, m2, l2, a2, v)
        return m1, l1, a1, m2, l2, a2

    neg = jnp.full((tq, 1), -jnp.inf, F32)
    zero = jnp.zeros((tq, 1), F32)
    acc0 = jnp.zeros((tq, 2 * HEAD_DIM), F32)
    carry = (neg, zero, acc0, neg, zero, acc0)
    carry = lax.fori_loop(0, qi, lambda kb, c: step(kb, c, False), carry)
    m1, l1, a1, m2, l2, a2 = step(qi, carry, True)

    lv = lam_ref[...]
    lam = (jnp.exp(jnp.sum(lv[0:1] * lv[1:2], axis=-1, keepdims=True))
           - jnp.exp(jnp.sum(lv[2:3] * lv[3:4], axis=-1, keepdims=True)) + lam_init)
    o = a1 / l1 - lam * (a2 / l2)
    o_ref[0] = _rms(o, g_ref[...]) * (1.0 - lam_init)


def _diff_attention(seg_b, lam_vecs, norm_g, lam_init):
    B, S, _ = seg_b.shape
    tq = min(S, 256)
    hw = 2 * HEAD_DIM
    return pl.pallas_call(
        functools.partial(_diff_body, tq=tq, lam_init=lam_init),
        grid=(B, N_HEADS, S // tq),
        in_specs=[pl.BlockSpec((1, tq, hw), lambda b, h, i: (b, i, h)),
                  pl.BlockSpec((1, S, hw), lambda b, h, i: (b, 0, N_HEADS + h)),
                  pl.BlockSpec((1, S, hw), lambda b, h, i: (b, 0, 2 * N_HEADS + h)),
                  pl.BlockSpec((4, HEAD_DIM), lambda b, h, i: (0, 0)),
                  pl.BlockSpec((1, hw), lambda b, h, i: (0, 0))],
        out_specs=pl.BlockSpec((1, tq, hw), lambda b, h, i: (b, i, h)),
        out_shape=jax.ShapeDtypeStruct((B, S, DIFF_W), F32),
        scratch_shapes=[pltpu.VMEM((S, hw), BF16), pltpu.VMEM((S, hw), BF16)],
        compiler_params=_cparams(3),
        name="diff_attention",
    )(seg_b, seg_b, seg_b, lam_vecs, norm_g.reshape(1, hw))


def _hgrn_body(q_ref, f_ref, i_ref, g_ref, la_ref, lc_ref, oml_ref, ng_ref, o_ref,
               st_ref, b_ref, k_ref, p_ref, a_ref, *, tr):
    t = pl.program_id(1)
    C = CHUNK
    W = MIX_W

    @pl.when(t == 0)
    def _():
        st_ref[...] = jnp.zeros_like(st_ref)

    same_head = _head_sum_matrix(W)
    e_bf = jnp.where(same_head, 1.0, 0.0).astype(BF16)
    e_f32 = jnp.where(same_head, 1.0, 0.0)
    row = lax.broadcasted_iota(jnp.int32, (C, W), 0)

    def chunk(c, _):
        rows = pl.ds(pl.multiple_of(c * C, C), C)
        qr = q_ref[0, rows, :]
        f = f_ref[0, rows, :]
        iv = i_ref[0, rows, :]
        q = qr * _sigmoid(qr)
        log_sig = jnp.minimum(f, 0.0) - jnp.log1p(jnp.exp(-jnp.abs(f)))
        x1 = la_ref[...]
        x2 = lc_ref[...] + log_sig
        log_f = jnp.maximum(x1, x2) + jnp.log1p(jnp.exp(-jnp.abs(x1 - x2)))
        k = oml_ref[...] * _sigmoid(-f)
        b = _cumsum_rows(log_f)
        b_ref[...] = b
        k_ref[...] = k
        st = st_ref[...]
        o = _dot_nt((q * jnp.exp(b)).astype(BF16), st.astype(BF16))

        def fill(s, _):
            e = jnp.exp(jnp.where(row >= s, b - b_ref[pl.ds(s, 1), :], -jnp.inf))
            p_ref[pl.ds(pl.multiple_of(s * C, C), C), :] = (q * e * k_ref[pl.ds(s, 1), :]).astype(BF16)
            return 0

        lax.fori_loop(0, C, fill, 0)
        a_ref[...] = _dot(p_ref[...], e_bf)

        def gather(s, acc):
            return acc + a_ref[pl.ds(pl.multiple_of(s * C, C), C), :] * i_ref[0, pl.ds(c * C + s, 1), :]

        o = lax.fori_loop(0, C, gather, o)
        b_last = b[C - 1:C, :]
        upd = _dot_tn(iv.astype(BF16), (k * jnp.exp(b_last - b)).astype(BF16))
        st_ref[...] = st * jnp.exp(b_last) + jnp.where(same_head, upd, 0.0)
        ms = _dot_f32(o * o, e_f32) * (1.0 / HEAD_DIM)
        gr = g_ref[0, rows, :]
        o_ref[0, rows, :] = o * lax.rsqrt(ms + NORM_EPS) * ng_ref[...] * (gr * _sigmoid(gr))
        return 0

    lax.fori_loop(0, tr // C, chunk, 0)


def _hgrn(seg_c, lb, norm_g):
    B, S, _ = seg_c.shape
    tr = min(S, 256)
    W = MIX_W
    col = lambda j: pl.BlockSpec((1, tr, W), lambda b, t, j=j: (b, t, j))
    vec = pl.BlockSpec((1, W), lambda b, t: (0, 0))
    lb = lb.reshape(1, W)
    return pl.pallas_call(
        functools.partial(_hgrn_body, tr=tr),
        grid=(B, S // tr),
        in_specs=[col(0), col(1), col(2), col(3), vec, vec, vec, vec],
        out_specs=pl.BlockSpec((1, tr, W), lambda b, t: (b, t, 0)),
        out_shape=jax.ShapeDtypeStruct((B, S, W), F32),
        scratch_shapes=[pltpu.VMEM((W, W), F32), pltpu.VMEM((CHUNK, W), F32), pltpu.VMEM((CHUNK, W), F32),
                        pltpu.VMEM((CHUNK * CHUNK, W), BF16), pltpu.VMEM((CHUNK * CHUNK, W), F32)],
        compiler_params=_cparams(2),
        name="hgrn2",
    )(seg_c, seg_c, seg_c, seg_c, jnp.log(lb), jnp.log1p(-lb), 1.0 - lb,
      jnp.tile(norm_g.reshape(1, HEAD_DIM), (1, N_HEADS)))


def _rwkv_body(*refs, tr, first_layer):
    if first_layer:
        (x_ref, mu_ref, w0_ref, w2_ref, a0_ref, a2_ref, g2_ref, kk_ref, ka_ref, rk_ref, lg_ref, lb_ref,
         y_ref, vf_out_ref,
         carry_ref, st_ref, r_s, k_s, v_s, n_s, b_s, lw_s, y_s) = refs
    else:
        (x_ref, vf_ref, mu_ref, w0_ref, w2_ref, a0_ref, a2_ref, g2_ref, kk_ref, ka_ref, rk_ref, lg_ref, lb_ref,
         v0_ref, v1_ref, v2_ref,
         y_ref,
         carry_ref, st_ref, r_s, k_s, v_s, n_s, b_s, lw_s, y_s) = refs
    t = pl.program_id(1)
    C = CHUNK
    W = MIX_W

    @pl.when(t == 0)
    def _():
        st_ref[...] = jnp.zeros_like(st_ref)
        carry_ref[...] = jnp.zeros_like(carry_ref)

    same_head = _head_sum_matrix(W)
    e_f32 = jnp.where(same_head, 1.0, 0.0)
    head = lax.broadcasted_iota(jnp.int32, (1, W), 1) // HEAD_DIM

    xs = x_ref[0]
    rowi = lax.broadcasted_iota(jnp.int32, xs.shape, 0)
    prev = jnp.where(rowi == 0, carry_ref[0:1, :], pltpu.roll(xs, 1, 0))
    carry_ref[0:1, :] = xs[tr - 1:tr, :]
    xm = xs + (prev - xs) * mu_ref[...]
    r = xm[:, 0:W]
    k = xm[:, W:2 * W]
    v = xm[:, 2 * W:3 * W]
    w_low = xm[:, 3 * W:3 * W + 64]
    a_low = xm[:, 3 * W + 64:3 * W + 128]
    g_low = xm[:, 3 * W + 128:3 * W + 256]
    wlog = -_softplus(-(w0_ref[...] + _dot(jnp.tanh(w_low).astype(BF16), w2_ref[...]))) - 0.5
    a = _sigmoid(a0_ref[...] + _dot(a_low.astype(BF16), a2_ref[...]))
    g = _dot(_sigmoid(g_low).astype(BF16), g2_ref[...])
    kkr = k * kk_ref[...]
    kn = kkr / jnp.maximum(jnp.sqrt(_dot_f32(kkr * kkr, e_f32)), 1e-12)
    k = k * (1.0 + (a - 1.0) * ka_ref[...])
    if first_layer:
        vf_out_ref[0] = v
    else:
        mix = _dot(_dot(v.astype(BF16), v1_ref[...]).astype(BF16), v2_ref[...])
        v = v + (vf_ref[0] - v) * _sigmoid(v0_ref[...] + mix)
    r_s[...] = r
    k_s[...] = k
    v_s[...] = v
    n_s[...] = kn
    b_s[...] = kn * a
    lw_s[...] = -jnp.exp(wlog)

    tt = lax.broadcasted_iota(jnp.int32, (N_HEADS * C, 2 * C), 0) % C
    ss = lax.broadcasted_iota(jnp.int32, (N_HEADS * C, 2 * C), 1) % C
    eye = jnp.where(lax.broadcasted_iota(jnp.int32, (C, C), 0) == lax.broadcasted_iota(jnp.int32, (C, C), 1),
                    1.0, 0.0)

    def chunk(c, _):
        rows = pl.ds(pl.multiple_of(c * C, C), C)
        rc, kc, vc, nc, bc, lw = r_s[rows, :], k_s[rows, :], v_s[rows, :], n_s[rows, :], b_s[rows, :], lw_s[rows, :]
        cs = _cumsum_rows(lw)
        c_last = cs[C - 1:C, :]
        r_t = rc * jnp.exp(cs)
        n_t = nc * jnp.exp(cs - lw)
        inv = jnp.exp(-cs)
        kb = jnp.concatenate([kc * inv, bc * inv], axis=0).astype(BF16)
        lhs = jnp.concatenate([jnp.where(head == h, n_t, 0.0) for h in range(N_HEADS)]
                              + [jnp.where(head == h, r_t, 0.0) for h in range(N_HEADS)], axis=0)
        gram = _dot_nt(lhs.astype(BF16), kb)
        g_low_tri = jnp.where(ss < tt, gram[0:N_HEADS * C], 0.0)
        g_inc_tri = jnp.where(ss <= tt, gram[N_HEADS * C:], 0.0)
        st = st_ref[...]
        st_bf = st.astype(BF16)
        rhs0 = _dot_nt(n_t.astype(BF16), st_bf)
        y = _dot_nt(r_t.astype(BF16), st_bf)
        vc_bf = vc.astype(BF16)
        u = jnp.zeros((C, W), F32)
        for h in range(N_HEADS):
            blk = g_low_tri[h * C:(h + 1) * C]
            l_k = blk[:, 0:C]
            l_b = blk[:, C:2 * C]
            inv_t = eye - l_b
            pw = _dot_f32(l_b, l_b)
            n = 2
            while n < C:
                inv_t = inv_t + _dot_f32(inv_t, pw)
                n *= 2
                if n < C:
                    pw = _dot_f32(pw, pw)
            rhs = rhs0 + _dot(l_k.astype(BF16), vc_bf)
            u = jnp.where(head == h, _dot_f32(inv_t, rhs), u)
        vu = jnp.concatenate([vc, -u], axis=0).astype(BF16)
        for h in range(N_HEADS):
            m_kb = g_inc_tri[h * C:(h + 1) * C].astype(BF16)
            y = jnp.where(head == h, y + _dot(m_kb, vu), y)
        y_s[rows, :] = y
        dec = jnp.exp(c_last - cs)
        kb_end = jnp.concatenate([kc * dec, bc * dec], axis=0).astype(BF16)
        st_ref[...] = st * jnp.exp(c_last) + jnp.where(same_head, _dot_tn(vu, kb_end), 0.0)
        return 0

    lax.fori_loop(0, tr // C, chunk, 0)

    y = y_s[...]
    r = r_s[...]
    k = k_s[...]
    v = v_s[...]
    mean = _dot_f32(y, e_f32) * (1.0 / HEAD_DIM)
    yc = y - mean
    var = _dot_f32(yc * yc, e_f32) * (1.0 / HEAD_DIM)
    yn = yc * lax.rsqrt(var + RWKV_GN_EPS) * lg_ref[...] + lb_ref[...]
    bonus = _dot_f32(r * k * rk_ref[...], e_f32) * v
    y_ref[0] = (yn + bonus) * g


def _rwkv(seg_d, p, v_first):
    B, S, _ = seg_d.shape
    tr = min(S, 256)
    W = MIX_W
    first = v_first is None
    row = lambda a: a.reshape(1, -1)
    full = lambda a: pl.BlockSpec(a.shape, lambda b, t: (0,) * a.ndim)
    tile = lambda w: pl.BlockSpec((1, tr, w), lambda b, t: (b, t, 0))
    params = [row(p['mu']), row(p['w0']), p['w2'].astype(BF16), row(p['a0']), p['a2'].astype(BF16),
              p['g2'].astype(BF16), row(p['k_k']), row(p['k_a']), row(p['r_k']), row(p['lnx_g']), row(p['lnx_b'])]
    args = [seg_d]
    in_specs = [tile(SEG_D)]
    if not first:
        args.append(v_first)
        in_specs.append(tile(W))
        params += [row(p['v0']), p['v1'].astype(BF16), p['v2'].astype(BF16)]
    args += params
    in_specs += [full(a) for a in params]
    n_out = 2 if first else 1
    outs = pl.pallas_call(
        functools.partial(_rwkv_body, tr=tr, first_layer=first),
        grid=(B, S // tr),
        in_specs=in_specs,
        out_specs=[tile(W)] * n_out,
        out_shape=[jax.ShapeDtypeStruct((B, S, W), F32)] * n_out,
        scratch_shapes=[pltpu.VMEM((8, SEG_D), F32), pltpu.VMEM((W, W), F32)] + [pltpu.VMEM((tr, W), F32)] * 7,
        compiler_params=_cparams(2),
        name="rwkv7",
    )(*args)
    return (outs[0], outs[1]) if first else (outs[0], v_first)


def _merge_body(o0, l0, o1, l1, o2, l2, yb, yc, yd, g0, g1, g2, g3, x_ref, pa, pb, pc, pd, wo, out_ref):
    la, lb, lc = l0[...], l1[...], l2[...]
    m = jnp.maximum(jnp.maximum(la, lb), lc)
    e0, e1, e2 = jnp.exp(la - m), jnp.exp(lb - m), jnp.exp(lc - m)
    inv = 1.0 / (e0 + e1 + e2)
    y_a = (e0 * inv) * o0[...] + (e1 * inv) * o1[...] + (e2 * inv) * o2[...]
    merged = (_sigmoid(g0[...]) * _dot(y_a.astype(BF16), pa[...])
              + _sigmoid(g1[...]) * _dot(yb[...].astype(BF16), pb[...])
              + _sigmoid(g2[...]) * _dot(yc[...].astype(BF16), pc[...])
              + _sigmoid(g3[...]) * _dot(yd[...].astype(BF16), pd[...]))
    out_ref[...] = x_ref[...] + _dot(merged.astype(BF16), wo[...])


def _merge(dil, y_b, y_c, y_d, gates, x2, p_a, p_b, p_c, p_d, w_out):
    T, D = x2.shape
    tm = min(T, 512)
    rows = lambda w, j=0: pl.BlockSpec((tm, w), lambda i, j=j: (i, j))
    full = lambda a: pl.BlockSpec(a.shape, lambda i: (0, 0))
    flat = lambda a: a.reshape(T, a.shape[-1])
    acts = []
    for o, lse in dil:
        acts += [flat(o), flat(lse)]
    acts += [flat(y_b), flat(y_c), flat(y_d)]
    weights = [w.astype(BF16) for w in (p_a, p_b, p_c, p_d, w_out)]
    in_specs = ([rows(MIX_W)] * 6 + [rows(DIFF_W), rows(MIX_W), rows(MIX_W)]
                + [rows(D, j) for j in range(4)] + [rows(D)] + [full(w) for w in weights])
    return pl.pallas_call(
        _merge_body,
        grid=(T // tm,),
        in_specs=in_specs,
        out_specs=rows(D),
        out_shape=jax.ShapeDtypeStruct((T, D), F32),
        compiler_params=_cparams(1),
        name="gated_merge",
    )(*acts, gates, gates, gates, gates, x2, *weights)


def _mem_body(x_ref, g_ref, wq_ref, kv_ref, wo_ref, out_ref, *, n_heads):
    x = x_ref[0]
    D = x.shape[-1]
    dm = D // n_heads
    q = _dot(_rms(x, g_ref[...]).astype(BF16), wq_ref[...]) * (dm ** -0.5)
    kv = kv_ref[0].astype(BF16)
    outs = []
    for h in range(n_heads):
        s = _dot_nt(q[:, h * dm:(h + 1) * dm].astype(BF16), kv[:, h * dm:(h + 1) * dm])
        p = jnp.exp(s - jnp.max(s, axis=-1, keepdims=True))
        p = p / jnp.sum(p, axis=-1, keepdims=True)
        outs.append(_dot(p.astype(BF16), kv[:, D + h * dm:D + (h + 1) * dm]))
    o = jnp.concatenate(outs, axis=-1)
    out_ref[0] = x + _dot(o.astype(BF16), wo_ref[...])


def _mem_attention(x, g, w_q, kv, w_o, n_heads=4):
    B, S, D = x.shape
    M = kv.shape[1]
    tm = min(S, 512)
    full = lambda a: pl.BlockSpec(a.shape, lambda b, i: (0, 0))
    wq, wo = w_q.astype(BF16), w_o.astype(BF16)
    g = g.reshape(1, D)
    return pl.pallas_call(
        functools.partial(_mem_body, n_heads=n_heads),
        grid=(B, S // tm),
        in_specs=[pl.BlockSpec((1, tm, D), lambda b, i: (b, i, 0)), full(g), full(wq),
                  pl.BlockSpec((1, M, 2 * D), lambda b, i: (b, 0, 0)), full(wo)],
        out_specs=pl.BlockSpec((1, tm, D), lambda b, i: (b, i, 0)),
        out_shape=jax.ShapeDtypeStruct((B, S, D), F32),
        compiler_params=_cparams(2),
        name="mem_attention",
    )(x, g, wq, kv, wo)


def _ffn_body(x_ref, halo_ref, g_ref, wg_ref, wv_ref, cwg_ref, cwv_ref, cbg_ref, cbv_ref, wo_ref, fg_ref,
              out_ref, hn_ref, hh_ref, acc_ref, *, final_norm):
    i = pl.program_id(1)
    c = pl.program_id(2)
    tm = x_ref.shape[1]

    @pl.when(c == 0)
    def _():
        hn_ref[...] = _rms(x_ref[0], g_ref[...]).astype(BF16)
        hh_ref[...] = _rms(halo_ref[0], g_ref[...]).astype(BF16)
        acc_ref[...] = jnp.zeros_like(acc_ref)

    live = jnp.where(i > 0, 1.0, 0.0)
    row = lax.broadcasted_iota(jnp.int32, (tm, 1), 0)

    def conv(w_ref, cw_ref, cb_ref):
        u = _dot(hn_ref[...], w_ref[...])
        uh = _dot(hh_ref[...], w_ref[...]) * live
        u1 = jnp.where(row == 0, uh[7:8], pltpu.roll(u, 1, 0))
        u2 = jnp.where(row == 0, uh[6:7], jnp.where(row == 1, uh[7:8], pltpu.roll(u, 2, 0)))
        cw = cw_ref[...]
        return cb_ref[...] + u2 * cw[0:1] + u1 * cw[1:2] + u * cw[2:3]

    gate = conv(wg_ref, cwg_ref, cbg_ref)
    val = conv(wv_ref, cwv_ref, cbv_ref)
    act = (gate * _sigmoid(gate) * val).astype(BF16)
    acc_ref[...] += _dot(act, wo_ref[...])

    @pl.when(c == pl.num_programs(2) - 1)
    def _():
        y = x_ref[0] + acc_ref[...]
        out_ref[0] = _rms(y, fg_ref[...]) if final_norm else y


def _ffn(x, g, w_in, conv_w, conv_b, w_out, final_g):
    B, S, D = x.shape
    d_ff = w_out.shape[0]
    tm = min(S, 512)
    fc = d_ff // 2 if (d_ff // 2) % LANES == 0 else d_ff
    nf = d_ff // fc
    w_in, w_out = w_in.astype(BF16), w_out.astype(BF16)
    conv_b = conv_b.reshape(1, 2 * d_ff)
    vec = pl.BlockSpec((1, D), lambda b, i, c: (0, 0))
    fg = (final_g if final_g is not None else g).reshape(1, D)
    return pl.pallas_call(
        functools.partial(_ffn_body, final_norm=final_g is not None),
        grid=(B, S // tm, nf),
        in_specs=[pl.BlockSpec((1, tm, D), lambda b, i, c: (b, i, 0)),
                  pl.BlockSpec((1, 8, D), lambda b, i, c: (b, jnp.maximum(i * (tm // 8) - 1, 0), 0)),
                  vec,
                  pl.BlockSpec((D, fc), lambda b, i, c: (0, c)),
                  pl.BlockSpec((D, fc), lambda b, i, c: (0, nf + c)),
                  pl.BlockSpec((3, fc), lambda b, i, c: (0, c)),
                  pl.BlockSpec((3, fc), lambda b, i, c: (0, nf + c)),
                  pl.BlockSpec((1, fc), lambda b, i, c: (0, c)),
                  pl.BlockSpec((1, fc), lambda b, i, c: (0, nf + c)),
                  pl.BlockSpec((fc, D), lambda b, i, c: (c, 0)),
                  vec],
        out_specs=pl.BlockSpec((1, tm, D), lambda b, i, c: (b, i, 0)),
        out_shape=jax.ShapeDtypeStruct((B, S, D), F32),
        scratch_shapes=[pltpu.VMEM((tm, D), BF16), pltpu.VMEM((8, D), BF16), pltpu.VMEM((tm, D), F32)],
        compiler_params=_cparams(3),
        name="conv_ffn",
    )(x, x, g.reshape(1, D), w_in, w_in, conv_w, conv_w, conv_b, conv_b, w_out, fg)


def kernel(x, mem, positions, mix_norm_g, w_in, diff_lam, diff_norm_g, hgrn_lb_logits, hgrn_norm_g, rwkv_mu, rwkv_w0, rwkv_w2, rwkv_a0, rwkv_a2, rwkv_g2, rwkv_k_k, rwkv_k_a, rwkv_r_k, rwkv_lnx_g, rwkv_lnx_b, rwkv_v0, rwkv_v1, rwkv_v2, p_a, p_b, p_c, p_d, w_mix_out, mem_q_norm_g, mem_kv_norm_g, w_mem_q, w_mem_kv, w_mem_o, ffn_norm_g, w_ffn_in, ffn_conv_w, ffn_conv_b, w_ffn_out, final_norm_g):
    B, S, D = x.shape
    M = mem.shape[1]
    T = B * S
    depth = w_in.shape[0]
    assert S % (DIL_PATTERNS[-1][1] * DIL_BLOCK) == 0 and S % CHUNK == 0

    half = ROPE_DIMS // 2
    inv_freq = ROPE_THETA ** (-jnp.arange(half, dtype=F32) / half)
    d = jnp.arange(LANES) % HEAD_DIM
    invf_lanes = jnp.where(d < ROPE_DIMS, inv_freq[d % half], 0.0).reshape(1, LANES)
    rope = (positions.reshape(T, 1), invf_lanes)
    lb_all = jnp.cumsum(jax.nn.softmax(hgrn_lb_logits.astype(F32), axis=0), axis=0)
    lb_all = lb_all - lb_all[0:1]
    offs = (0, SEG_A, SEG_A + SEG_B, SEG_A + SEG_B + SEG_C, SEG_A + SEG_B + SEG_C + SEG_D, w_in.shape[2])
    mem2 = mem.reshape(B * M, D)

    v_first = None
    for l in range(depth):
        lam_init = 0.8 - 0.6 * math.exp(-0.3 * l)
        w_l = w_in[l].astype(BF16)
        x2 = x.reshape(T, D)
        seg = lambda s: w_l[:, offs[s]:offs[s + 1]]
        seg_a = _norm_matmul(x2, mix_norm_g[l], seg(0), MIX_W, rope).reshape(B, S, SEG_A)
        seg_b = _norm_matmul(x2, mix_norm_g[l], seg(1), DIFF_W, rope).reshape(B, S, SEG_B)
        seg_c = _norm_matmul(x2, mix_norm_g[l], seg(2), SEG_C).reshape(B, S, SEG_C)
        seg_d = _norm_matmul(x2, mix_norm_g[l], seg(3), SEG_D).reshape(B, S, SEG_D)
        gates = _norm_matmul(x2, mix_norm_g[l], seg(4), D)
        dil = [_dilated_group(seg_a, g, dilation) for g, (_, dilation) in enumerate(DIL_PATTERNS)]
        y_b = _diff_attention(seg_b, diff_lam[l], diff_norm_g[l], lam_init)
        y_c = _hgrn(seg_c, lb_all[l], hgrn_norm_g[l])
        rp = dict(mu=rwkv_mu[l], w0=rwkv_w0[l], w2=rwkv_w2[l], a0=rwkv_a0[l], a2=rwkv_a2[l], g2=rwkv_g2[l],
                  k_k=rwkv_k_k[l], k_a=rwkv_k_a[l], r_k=rwkv_r_k[l], lnx_g=rwkv_lnx_g[l], lnx_b=rwkv_lnx_b[l])
        if l > 0:
            rp.update(v0=rwkv_v0[l - 1], v1=rwkv_v1[l - 1], v2=rwkv_v2[l - 1])
        y_d, v_first = _rwkv(seg_d, rp, v_first)
        x2 = _merge(dil, y_b, y_c, y_d, gates, x2, p_a[l], p_b[l], p_c[l], p_d[l], w_mix_out[l])
        kv = _norm_matmul(mem2, mem_kv_norm_g[l], w_mem_kv[l].astype(BF16), D).reshape(B, M, 2 * D)
        x = _mem_attention(x2.reshape(B, S, D), mem_q_norm_g[l], w_mem_q[l], kv, w_mem_o[l])
        x = _ffn(x, ffn_norm_g[l], w_ffn_in[l], ffn_conv_w[l], ffn_conv_b[l], w_ffn_out[l],
                 final_norm_g if l == depth - 1 else None)
    return x
```

```python
import functools
import math

import jax
import jax.numpy as jnp
from jax import lax
from jax.experimental import pallas as pl
from jax.experimental.pallas import tpu as pltpu

F32 = jnp.float32
BF16 = jnp.bfloat16

NORM_EPS = 1e-5
HEAD_DIM = 64
ROPE_THETA = 500000.0
ROPE_DIMS = HEAD_DIM // 4
DIL_PATTERNS = ((128, 1), (512, 4), (2048, 16))
DIL_BLOCK = 128
N_HEADS = 4
MIX_W = N_HEADS * HEAD_DIM
DIFF_W = 2 * MIX_W
SEG_A = 3 * 3 * MIX_W
SEG_B = 3 * DIFF_W
SEG_C = 4 * MIX_W
SEG_D = 3 * MIX_W + 64 + 64 + 128
RWKV_GN_EPS = 1e-5 * HEAD_DIM
CHUNK = 64
SUB = 16
LANES = 128
VMEM_LIMIT = 56 * 1024 * 1024
LOG2E = math.log2(math.e)


def _cparams(n_axes):
    return pltpu.CompilerParams(dimension_semantics=("arbitrary",) * n_axes,
                                vmem_limit_bytes=VMEM_LIMIT)


def _dot(a, b):
    return jnp.dot(a, b, preferred_element_type=F32)


def _dot_nt(a, b):
    return lax.dot_general(a, b, (((1,), (1,)), ((), ())), preferred_element_type=F32)


def _dot_tn(a, b):
    return lax.dot_general(a, b, (((0,), (0,)), ((), ())), preferred_element_type=F32)


def _sigmoid(x):
    return 1.0 / (1.0 + jnp.exp(-x))


def _softplus(x):
    return jnp.maximum(x, 0.0) + jnp.log1p(jnp.exp(-jnp.abs(x)))


def _rms(x, g):
    ms = jnp.mean(x * x, axis=-1, keepdims=True)
    return x * lax.rsqrt(ms + NORM_EPS) * g


def _head_sum_matrix(width):
    r = lax.broadcasted_iota(jnp.int32, (width, width), 0) // HEAD_DIM
    c = lax.broadcasted_iota(jnp.int32, (width, width), 1) // HEAD_DIM
    return r == c


def _cumsum_rows(x):
    n = x.shape[0]
    row = lax.broadcasted_iota(jnp.int32, x.shape, 0)
    d = 1
    while d < n:
        x = x + jnp.where(row >= d, pltpu.roll(x, d, 0), 0.0)
        d *= 2
    return x


def _sum_heads(x, e_bf):
    hi = x.astype(BF16)
    lo = (x - hi.astype(F32)).astype(BF16)
    return _dot(hi, e_bf) + _dot(lo, e_bf)


def _norm_matmul_body(*refs, rope_slabs):
    if rope_slabs:
        x_ref, g_ref, w_ref, pos_ref, invf_ref, o_ref, hn_ref, cos_ref, sa_ref, sb_ref = refs
    else:
        x_ref, g_ref, w_ref, o_ref, hn_ref = refs
    j = pl.program_id(1)

    @pl.when(j == 0)
    def _():
        hn_ref[...] = _rms(x_ref[...], g_ref[...]).astype(BF16)
        if rope_slabs:
            ang = pos_ref[...].astype(F32) * invf_ref[...]
            d = lax.broadcasted_iota(jnp.int32, (1, LANES), 1) % HEAD_DIM
            s = jnp.sin(ang)
            cos_ref[...] = jnp.cos(ang)
            sa_ref[...] = jnp.where(d < ROPE_DIMS // 2, -s, 0.0)
            sb_ref[...] = jnp.where((d >= ROPE_DIMS // 2) & (d < ROPE_DIMS), s, 0.0)

    acc = _dot(hn_ref[...], w_ref[...])
    if not rope_slabs:
        o_ref[...] = acc.astype(o_ref.dtype)
        return
    half = ROPE_DIMS // 2
    for c, roped in enumerate(rope_slabs):
        t = acc[:, c * LANES:(c + 1) * LANES]
        if roped:
            t = (t * cos_ref[...] + pltpu.roll(t, LANES - half, 1) * sa_ref[...]
                 + pltpu.roll(t, half, 1) * sb_ref[...])
        o_ref[:, c * LANES:(c + 1) * LANES] = t.astype(o_ref.dtype)


def _norm_matmul(x2, g, w_bf, tn, rope=None, rope_slabs=(), out_dtype=F32):
    T, D = x2.shape
    N = w_bf.shape[1]
    tm = min(T, 1024)
    assert len(rope_slabs) in (0, tn // LANES)
    in_specs = [pl.BlockSpec((tm, D), lambda i, j: (i, 0)),
                pl.BlockSpec((1, D), lambda i, j: (0, 0)),
                pl.BlockSpec((D, tn), lambda i, j: (0, j))]
    args = [x2, g.reshape(1, D), w_bf]
    scratch = [pltpu.VMEM((tm, D), BF16)]
    if rope_slabs:
        in_specs += [pl.BlockSpec((tm, 1), lambda i, j: (i, 0)),
                     pl.BlockSpec((1, LANES), lambda i, j: (0, 0))]
        args += list(rope)
        scratch += [pltpu.VMEM((tm, LANES), F32)] * 3
    return pl.pallas_call(
        functools.partial(_norm_matmul_body, rope_slabs=tuple(rope_slabs)),
        grid=(T // tm, N // tn),
        in_specs=in_specs,
        out_specs=pl.BlockSpec((tm, tn), lambda i, j: (i, j)),
        out_shape=jax.ShapeDtypeStruct((T, N), out_dtype),
        scratch_shapes=scratch,
        compiler_params=_cparams(2),
        name="norm_matmul_rope" if rope_slabs else "norm_matmul",
    )(*args)


def _dilated_body(q_ref, kc_ref, kp_ref, vc_ref, vp_ref, o_ref, lse_ref, *, dilation, nbs, width):
    n = pl.program_id(2)
    blk = DIL_BLOCK
    unit = blk * dilation
    n_lane_heads = LANES // HEAD_DIM
    lane_head = lax.broadcasted_iota(jnp.int32, (1, LANES), 1) // HEAD_DIM
    qi = lax.broadcasted_iota(jnp.int32, (blk, blk), 0)
    kj = lax.broadcasted_iota(jnp.int32, (blk, blk), 1)
    cur_ok = kj <= qi
    first_prev_ok = (kj - qi) >= jnp.where(n > 0, 0, 2 * blk)
    later_prev_ok = kj >= qi

    def rows_of(j, r):
        start = j * unit + r
        return pl.ds(start, blk) if dilation == 1 else pl.ds(start, blk, stride=dilation)

    def process(units):
        loaded = []
        for j, r in units:
            rows = rows_of(j, r)
            prows = rows_of(max(j - 1, 0), r)
            kpr, vpr = (kp_ref, vp_ref) if j == 0 else (kc_ref, vc_ref)
            loaded.append((rows, first_prev_ok if j == 0 else later_prev_ok,
                           q_ref[0, rows, :] * (HEAD_DIM ** -0.5),
                           kc_ref[0, rows, :].astype(BF16), kpr[0, prows, :].astype(BF16),
                           vc_ref[0, rows, :].astype(BF16), vpr[0, prows, :].astype(BF16)))
        heads = [(u, h) for u in range(len(units)) for h in range(n_lane_heads)]
        scores = []
        for u, h in heads:
            _, prev_ok, q, kc, kp, _, _ = loaded[u]
            qh = jnp.where(lane_head == h, q, 0.0).astype(BF16)
            scores.append((jnp.where(cur_ok, _dot_nt(qh, kc), -jnp.inf), jnp.where(prev_ok, _dot_nt(qh, kp), -jnp.inf)))
        stats = []
        for sc, sp in scores:
            m = jnp.maximum(jnp.max(sc, axis=-1, keepdims=True), jnp.max(sp, axis=-1, keepdims=True))
            pc = jnp.exp(sc - m)
            pp = jnp.exp(sp - m)
            den = jnp.sum(pc, axis=-1, keepdims=True) + jnp.sum(pp, axis=-1, keepdims=True)
            stats.append((m, pc, pp, den))
        outs = []
        for (u, h), (m, pc, pp, den) in zip(heads, stats):
            _, _, _, _, _, vc, vp = loaded[u]
            inv = 1.0 / den
            outs.append((_dot((pc * inv).astype(BF16), vc) + _dot((pp * inv).astype(BF16), vp), m + jnp.log(den)))
        for u in range(len(units)):
            o_acc = jnp.zeros((blk, LANES), F32)
            lse_acc = jnp.zeros((blk, LANES), F32)
            for h in range(n_lane_heads):
                oh, lse = outs[u * n_lane_heads + h]
                o_acc = jnp.where(lane_head == h, oh, o_acc)
                lse_acc = jnp.where(lane_head == h, lse, lse_acc)
            o_ref[0, loaded[u][0], :] = o_acc
            lse_ref[0, loaded[u][0], :] = lse_acc

    if dilation == 1:
        for j in range(0, nbs, width):
            process([(j + w, 0) for w in range(width)])
    else:
        step = dilation // width
        for j in range(nbs):
            def body(r, _, j=j):
                process([(j, r + w * step) for w in range(width)])
                return 0
            lax.fori_loop(0, step, body, 0)


def _dilated_group(seg_a, g, dilation, width=4):
    B, S, _ = seg_a.shape
    unit = DIL_BLOCK * dilation
    nbs = max(1, 512 // unit)
    tr = unit * nbs
    ncol = 2 * (g * 3)

    def cur(which):
        return pl.BlockSpec((1, tr, LANES), lambda b, p, n: (b, n, ncol + 2 * which + p))

    def prev(which):
        return pl.BlockSpec((1, unit, LANES), lambda b, p, n: (b, jnp.maximum(n * nbs - 1, 0), ncol + 2 * which + p))

    out_spec = pl.BlockSpec((1, tr, LANES), lambda b, p, n: (b, n, p))
    return pl.pallas_call(
        functools.partial(_dilated_body, dilation=dilation, nbs=nbs, width=width),
        grid=(B, MIX_W // LANES, S // tr),
        in_specs=[cur(0), cur(1), prev(1), cur(2), prev(2)],
        out_specs=[out_spec, out_spec],
        out_shape=[jax.ShapeDtypeStruct((B, S, MIX_W), F32)] * 2,
        compiler_params=_cparams(3),
        name="dilated_attention",
    )(seg_a, seg_a, seg_a, seg_a, seg_a)


def _diff_body(q_ref, k_ref, v_ref, lam_ref, g_ref, o_ref, vt_ref, a1_ref, a2_ref, *, tq, kc, lam_init):
    qi = pl.program_id(2)
    n_kv = vt_ref.shape[0]

    @pl.when(qi == 0)
    def _():
        for c in range(n_kv):
            vt_ref[c] = v_ref[0, c * kc:(c + 1) * kc, :].astype(F32).T.astype(BF16)

    q = q_ref[0].astype(F32) * (HEAD_DIM ** -0.5 * LOG2E)
    lane = lax.broadcasted_iota(jnp.int32, (1, 2 * HEAD_DIM), 1)
    q1 = jnp.where(lane < HEAD_DIM, q, 0.0).astype(BF16)
    q2 = jnp.where(lane >= HEAD_DIM, q, 0.0).astype(BF16)
    key = lax.broadcasted_iota(jnp.int32, (kc, tq), 0)
    qry = lax.broadcasted_iota(jnp.int32, (kc, tq), 1) + qi * tq
    a1_ref[...] = jnp.zeros_like(a1_ref)
    a2_ref[...] = jnp.zeros_like(a2_ref)

    def block(kb, carry, masked):
        m1, l1, m2, l2 = carry
        k = k_ref[0, pl.ds(pl.multiple_of(kb * kc, kc), kc), :]
        vt = vt_ref[kb]
        s1 = _dot_nt(k, q1)
        s2 = _dot_nt(k, q2)
        if masked:
            ok = key + kb * kc <= qry
            s1 = jnp.where(ok, s1, -jnp.inf)
            s2 = jnp.where(ok, s2, -jnp.inf)
        n1 = jnp.maximum(m1, jnp.max(s1, axis=0, keepdims=True))
        n2 = jnp.maximum(m2, jnp.max(s2, axis=0, keepdims=True))
        p1 = jnp.exp2(s1 - n1)
        p2 = jnp.exp2(s2 - n2)
        al1 = jnp.exp2(m1 - n1)
        al2 = jnp.exp2(m2 - n2)
        pv1 = _dot(vt, p1.astype(BF16))
        pv2 = _dot(vt, p2.astype(BF16))
        l1 = l1 * al1 + jnp.sum(p1.reshape(kc // 8, 8, tq), axis=0)
        l2 = l2 * al2 + jnp.sum(p2.reshape(kc // 8, 8, tq), axis=0)
        a1_ref[...] = a1_ref[...] * al1 + pv1
        a2_ref[...] = a2_ref[...] * al2 + pv2
        return n1, l1, n2, l2

    neg = jnp.full((1, tq), -jnp.inf, F32)
    zero = jnp.zeros((8, tq), F32)
    n_full = (qi * tq) // kc
    carry = lax.fori_loop(0, n_full, lambda kb, c: block(kb, c, False), (neg, zero, neg, zero))
    for u in range(tq // kc if tq > kc else 1):
        carry = block(n_full + u, carry, True)
    _, l1, _, l2 = carry

    lv = lam_ref[...]
    lam = (jnp.exp(jnp.sum(lv[0:1] * lv[1:2], axis=-1, keepdims=True))
           - jnp.exp(jnp.sum(lv[2:3] * lv[3:4], axis=-1, keepdims=True)) + lam_init)
    o_t = (a1_ref[...] / jnp.sum(l1, axis=0, keepdims=True)
           - lam * (a2_ref[...] / jnp.sum(l2, axis=0, keepdims=True)))
    o_ref[0] = _rms(o_t.T, g_ref[...]) * (1.0 - lam_init)


def _diff_attention(seg_b, lam_vecs, norm_g, lam_init, tq=512, kc=512):
    B, S, _ = seg_b.shape
    tq = min(S, tq)
    kc = min(kc, tq)
    hw = 2 * HEAD_DIM
    return pl.pallas_call(
        functools.partial(_diff_body, tq=tq, kc=kc, lam_init=lam_init),
        grid=(B, N_HEADS, S // tq),
        in_specs=[pl.BlockSpec((1, tq, hw), lambda b, h, i: (b, i, h)),
                  pl.BlockSpec((1, S, hw), lambda b, h, i: (b, 0, N_HEADS + h)),
                  pl.BlockSpec((1, S, hw), lambda b, h, i: (b, 0, 2 * N_HEADS + h)),
                  pl.BlockSpec((4, HEAD_DIM), lambda b, h, i: (0, 0)),
                  pl.BlockSpec((1, hw), lambda b, h, i: (0, 0))],
        out_specs=pl.BlockSpec((1, tq, hw), lambda b, h, i: (b, i, h)),
        out_shape=jax.ShapeDtypeStruct((B, S, DIFF_W), F32),
        scratch_shapes=[pltpu.VMEM((S // kc, hw, kc), BF16), pltpu.VMEM((hw, tq), F32), pltpu.VMEM((hw, tq), F32)],
        compiler_params=_cparams(3),
        name="diff_attention",
    )(seg_b, seg_b, seg_b, lam_vecs, norm_g.reshape(1, hw))


def _hgrn_body(q_ref, f_ref, i_ref, g_ref, la_ref, lc_ref, oml_ref, ng_ref, o_ref, st_ref, *, tr, unroll):
    t = pl.program_id(1)
    c = SUB
    W = MIX_W

    @pl.when(t == 0)
    def _():
        st_ref[...] = jnp.zeros_like(st_ref)

    same_head = _head_sum_matrix(W)
    e_bf = jnp.where(same_head, 1.0, 0.0).astype(BF16)
    row = lax.broadcasted_iota(jnp.int32, (c, W), 0)

    def sub(j, _):
        rows = pl.ds(pl.multiple_of(j * c, c), c)
        qr = q_ref[0, rows, :]
        f = f_ref[0, rows, :]
        iv = i_ref[0, rows, :]
        gr = g_ref[0, rows, :]
        q = qr * _sigmoid(qr)
        log_sig = jnp.minimum(f, 0.0) - jnp.log1p(jnp.exp(-jnp.abs(f)))
        x1 = la_ref[...]
        x2 = lc_ref[...] + log_sig
        log_f = jnp.maximum(x1, x2) + jnp.log1p(jnp.exp(-jnp.abs(x1 - x2)))
        k = oml_ref[...] * _sigmoid(-f)
        b = _cumsum_rows(log_f)
        st = st_ref[...]
        o = _dot_nt((q * jnp.exp(b)).astype(BF16), st.astype(BF16))
        ps = []
        for s in range(c):
            e = jnp.exp(jnp.where(row >= s, b - b[s:s + 1], -jnp.inf))
            ps.append((q * e * k[s:s + 1]).astype(BF16))
        a = _dot(jnp.concatenate(ps, axis=0), e_bf)
        for s in range(c):
            o = o + a[s * c:(s + 1) * c] * iv[s:s + 1]
        b_last = b[c - 1:c]
        upd = _dot_tn(iv.astype(BF16), (k * jnp.exp(b_last - b)).astype(BF16))
        st_ref[...] = st * jnp.exp(b_last) + jnp.where(same_head, upd, 0.0)
        ms = _sum_heads(o * o, e_bf) * (1.0 / HEAD_DIM)
        o_ref[0, rows, :] = o * lax.rsqrt(ms + NORM_EPS) * ng_ref[...] * (gr * _sigmoid(gr))
        return 0

    lax.fori_loop(0, tr // c, sub, 0, unroll=unroll)


def _hgrn(seg_c, lb, norm_g, tr=256, unroll=4):
    B, S, _ = seg_c.shape
    tr = min(S, tr)
    W = MIX_W
    col = lambda j: pl.BlockSpec((1, tr, W), lambda b, t, j=j: (b, t, j))
    vec = pl.BlockSpec((1, W), lambda b, t: (0, 0))
    lb = lb.reshape(1, W)
    return pl.pallas_call(
        functools.partial(_hgrn_body, tr=tr, unroll=unroll),
        grid=(B, S // tr),
        in_specs=[col(0), col(1), col(2), col(3), vec, vec, vec, vec],
        out_specs=pl.BlockSpec((1, tr, W), lambda b, t: (b, t, 0)),
        out_shape=jax.ShapeDtypeStruct((B, S, W), F32),
        scratch_shapes=[pltpu.VMEM((W, W), F32)],
        compiler_params=_cparams(2),
        name="hgrn2",
    )(seg_c, seg_c, seg_c, seg_c, jnp.log(lb), jnp.log1p(-lb), 1.0 - lb,
      jnp.tile(norm_g.reshape(1, HEAD_DIM), (1, N_HEADS)))


def _rwkv_body(*refs, tr, first_layer, unroll):
    if first_layer:
        (x_ref, mu_ref, w0_ref, w2_ref, a0_ref, a2_ref, g2_ref, kk_ref, ka_ref, rk_ref, lg_ref, lb_ref,
         y_ref, vf_out_ref,
         carry_ref, st_ref, r_s, k_s, v_s, n_s, b_s, lw_s, y_s) = refs
    else:
        (x_ref, vf_ref, mu_ref, w0_ref, w2_ref, a0_ref, a2_ref, g2_ref, kk_ref, ka_ref, rk_ref, lg_ref, lb_ref,
         v0_ref, v1_ref, v2_ref,
         y_ref,
         carry_ref, st_ref, r_s, k_s, v_s, n_s, b_s, lw_s, y_s) = refs
    t = pl.program_id(1)
    C = CHUNK
    W = MIX_W

    @pl.when(t == 0)
    def _():
        st_ref[...] = jnp.zeros_like(st_ref)
        carry_ref[...] = jnp.zeros_like(carry_ref)

    same_head = _head_sum_matrix(W)
    e_bf = jnp.where(same_head, 1.0, 0.0).astype(BF16)
    head = lax.broadcasted_iota(jnp.int32, (1, W), 1) // HEAD_DIM

    xs = x_ref[0]
    rowi = lax.broadcasted_iota(jnp.int32, xs.shape, 0)
    prev = jnp.where(rowi == 0, carry_ref[0:1, :], pltpu.roll(xs, 1, 0))
    carry_ref[0:1, :] = xs[tr - 1:tr, :]
    xm = xs + (prev - xs) * mu_ref[...]
    r = xm[:, 0:W]
    k = xm[:, W:2 * W]
    v = xm[:, 2 * W:3 * W]
    w_low = xm[:, 3 * W:3 * W + 64]
    a_low = xm[:, 3 * W + 64:3 * W + 128]
    g_low = xm[:, 3 * W + 128:3 * W + 256]
    wlog = -_softplus(-(w0_ref[...] + _dot(jnp.tanh(w_low).astype(BF16), w2_ref[...]))) - 0.5
    a = _sigmoid(a0_ref[...] + _dot(a_low.astype(BF16), a2_ref[...]))
    g = _dot(_sigmoid(g_low).astype(BF16), g2_ref[...])
    kkr = k * kk_ref[...]
    kn = kkr / jnp.maximum(jnp.sqrt(_sum_heads(kkr * kkr, e_bf)), 1e-12)
    k = k * (1.0 + (a - 1.0) * ka_ref[...])
    if first_layer:
        vf_out_ref[0] = v
    else:
        mix = _dot(_dot(v.astype(BF16), v1_ref[...]).astype(BF16), v2_ref[...])
        v = v + (vf_ref[0] - v) * _sigmoid(v0_ref[...] + mix)
    r_s[...] = r
    k_s[...] = k
    v_s[...] = v
    n_s[...] = kn
    b_s[...] = kn * a
    lw_s[...] = -jnp.exp(wlog)

    hc = N_HEADS * C
    tt = lax.broadcasted_iota(jnp.int32, (hc, hc), 0) % C
    ss = lax.broadcasted_iota(jnp.int32, (hc, hc), 1) % C
    strict = ss < tt
    incl = ss <= tt
    eye = jnp.where(lax.broadcasted_iota(jnp.int32, (hc, hc), 0) == lax.broadcasted_iota(jnp.int32, (hc, hc), 1),
                    1.0, 0.0)

    def stack(x):
        return jnp.concatenate([jnp.where(head == h, x, 0.0) for h in range(N_HEADS)], axis=0).astype(BF16)

    def fold(x):
        return x[0:C] + x[C:2 * C] + x[2 * C:3 * C] + x[3 * C:4 * C]

    def chunk(c, _):
        rows = pl.ds(pl.multiple_of(c * C, C), C)
        rc, kc, vc, nc, bc, lw = r_s[rows, :], k_s[rows, :], v_s[rows, :], n_s[rows, :], b_s[rows, :], lw_s[rows, :]
        cs = _cumsum_rows(lw)
        c_last = cs[C - 1:C, :]
        inv = jnp.exp(-cs)
        r_st = stack(rc * jnp.exp(cs))
        n_st = stack(nc * jnp.exp(cs - lw))
        k_st = stack(kc * inv)
        b_st = stack(bc * inv)
        v_st = stack(vc)
        l_k = jnp.where(strict, _dot_nt(n_st, k_st), 0.0).astype(BF16)
        l_b = jnp.where(strict, _dot_nt(n_st, b_st), 0.0)
        m_k = jnp.where(incl, _dot_nt(r_st, k_st), 0.0).astype(BF16)
        m_b = jnp.where(incl, _dot_nt(r_st, b_st), 0.0).astype(BF16)
        l_bf = l_b.astype(BF16)
        inv_t = eye - l_b
        pw = _dot(l_bf, l_bf)
        n = 2
        while n < C:
            inv_t = inv_t + _dot(inv_t.astype(BF16), pw.astype(BF16))
            n *= 2
            if n < C:
                pw = _dot(pw.astype(BF16), pw.astype(BF16))
        st = st_ref[...]
        st_bf = st.astype(BF16)
        rhs = _dot_nt(n_st, st_bf) + _dot(l_k, v_st)
        u_st = _dot(inv_t.astype(BF16), rhs.astype(BF16))
        y_st = _dot_nt(r_st, st_bf) + _dot(m_k, v_st) - _dot(m_b, u_st.astype(BF16))
        u = fold(u_st)
        y_s[rows, :] = fold(y_st)
        dec = jnp.exp(c_last - cs)
        vu = jnp.concatenate([vc, -u], axis=0).astype(BF16)
        kb_end = jnp.concatenate([kc * dec, bc * dec], axis=0).astype(BF16)
        st_ref[...] = st * jnp.exp(c_last) + jnp.where(same_head, _dot_tn(vu, kb_end), 0.0)
        return 0

    lax.fori_loop(0, tr // C, chunk, 0, unroll=unroll)

    y = y_s[...]
    r = r_s[...]
    k = k_s[...]
    v = v_s[...]
    mean = _sum_heads(y, e_bf) * (1.0 / HEAD_DIM)
    yc = y - mean
    var = _sum_heads(yc * yc, e_bf) * (1.0 / HEAD_DIM)
    yn = yc * lax.rsqrt(var + RWKV_GN_EPS) * lg_ref[...] + lb_ref[...]
    bonus = _sum_heads(r * k * rk_ref[...], e_bf) * v
    y_ref[0] = (yn + bonus) * g


def _rwkv(seg_d, p, v_first, tr=256, unroll=1):
    B, S, _ = seg_d.shape
    tr = min(S, tr)
    W = MIX_W
    first = v_first is None
    row = lambda a: a.reshape(1, -1)
    full = lambda a: pl.BlockSpec(a.shape, lambda b, t: (0,) * a.ndim)
    tile = lambda w: pl.BlockSpec((1, tr, w), lambda b, t: (b, t, 0))
    params = [row(p['mu']), row(p['w0']), p['w2'].astype(BF16), row(p['a0']), p['a2'].astype(BF16),
              p['g2'].astype(BF16), row(p['k_k']), row(p['k_a']), row(p['r_k']), row(p['lnx_g']), row(p['lnx_b'])]
    args = [seg_d]
    in_specs = [tile(SEG_D)]
    if not first:
        args.append(v_first)
        in_specs.append(tile(W))
        params += [row(p['v0']), p['v1'].astype(BF16), p['v2'].astype(BF16)]
    args += params
    in_specs += [full(a) for a in params]
    n_out = 2 if first else 1
    outs = pl.pallas_call(
        functools.partial(_rwkv_body, tr=tr, first_layer=first, unroll=unroll),
        grid=(B, S // tr),
        in_specs=in_specs,
        out_specs=[tile(W)] * n_out,
        out_shape=[jax.ShapeDtypeStruct((B, S, W), F32)] * n_out,
        scratch_shapes=[pltpu.VMEM((8, SEG_D), F32), pltpu.VMEM((W, W), F32)] + [pltpu.VMEM((tr, W), F32)] * 7,
        compiler_params=_cparams(2),
        name="rwkv7",
    )(*args)
    return (outs[0], outs[1]) if first else (outs[0], v_first)


def _merge_body(o0, l0, o1, l1, o2, l2, yb, yc, yd, g0, g1, g2, g3, x_ref, pa, pb, pc, pd, wo, out_ref):
    la, lb, lc = l0[...], l1[...], l2[...]
    m = jnp.maximum(jnp.maximum(la, lb), lc)
    e0, e1, e2 = jnp.exp(la - m), jnp.exp(lb - m), jnp.exp(lc - m)
    inv = 1.0 / (e0 + e1 + e2)
    y_a = (e0 * inv) * o0[...] + (e1 * inv) * o1[...] + (e2 * inv) * o2[...]
    merged = (_sigmoid(g0[...]) * _dot(y_a.astype(BF16), pa[...])
              + _sigmoid(g1[...]) * _dot(yb[...].astype(BF16), pb[...])
              + _sigmoid(g2[...]) * _dot(yc[...].astype(BF16), pc[...])
              + _sigmoid(g3[...]) * _dot(yd[...].astype(BF16), pd[...]))
    out_ref[...] = x_ref[...] + _dot(merged.astype(BF16), wo[...])


def _merge(dil, y_b, y_c, y_d, gates, x2, p_a, p_b, p_c, p_d, w_out):
    T, D = x2.shape
    tm = min(T, 512)
    rows = lambda w, j=0: pl.BlockSpec((tm, w), lambda i, j=j: (i, j))
    full = lambda a: pl.BlockSpec(a.shape, lambda i: (0, 0))
    flat = lambda a: a.reshape(T, a.shape[-1])
    acts = []
    for o, lse in dil:
        acts += [flat(o), flat(lse)]
    acts += [flat(y_b), flat(y_c), flat(y_d)]
    weights = [w.astype(BF16) for w in (p_a, p_b, p_c, p_d, w_out)]
    in_specs = ([rows(MIX_W)] * 6 + [rows(DIFF_W), rows(MIX_W), rows(MIX_W)]
                + [rows(D, j) for j in range(4)] + [rows(D)] + [full(w) for w in weights])
    return pl.pallas_call(
        _merge_body,
        grid=(T // tm,),
        in_specs=in_specs,
        out_specs=rows(D),
        out_shape=jax.ShapeDtypeStruct((T, D), F32),
        compiler_params=_cparams(1),
        name="gated_merge",
    )(*acts, gates, gates, gates, gates, x2, *weights)


def _mem_body(x_ref, g_ref, wq_ref, kv_ref, wo_ref, out_ref, *, n_heads):
    x = x_ref[0]
    D = x.shape[-1]
    dm = D // n_heads
    q = _dot(_rms(x, g_ref[...]).astype(BF16), wq_ref[...]) * (dm ** -0.5)
    kv = kv_ref[0].astype(BF16)
    outs = []
    for h in range(n_heads):
        s = _dot_nt(q[:, h * dm:(h + 1) * dm].astype(BF16), kv[:, h * dm:(h + 1) * dm])
        p = jnp.exp(s - jnp.max(s, axis=-1, keepdims=True))
        p = p / jnp.sum(p, axis=-1, keepdims=True)
        outs.append(_dot(p.astype(BF16), kv[:, D + h * dm:D + (h + 1) * dm]))
    o = jnp.concatenate(outs, axis=-1)
    out_ref[0] = x + _dot(o.astype(BF16), wo_ref[...])


def _mem_attention(x, g, w_q, kv, w_o, n_heads=4):
    B, S, D = x.shape
    M = kv.shape[1]
    tm = min(S, 512)
    full = lambda a: pl.BlockSpec(a.shape, lambda b, i: (0, 0))
    wq, wo = w_q.astype(BF16), w_o.astype(BF16)
    g = g.reshape(1, D)
    return pl.pallas_call(
        functools.partial(_mem_body, n_heads=n_heads),
        grid=(B, S // tm),
        in_specs=[pl.BlockSpec((1, tm, D), lambda b, i: (b, i, 0)), full(g), full(wq),
                  pl.BlockSpec((1, M, 2 * D), lambda b, i: (b, 0, 0)), full(wo)],
        out_specs=pl.BlockSpec((1, tm, D), lambda b, i: (b, i, 0)),
        out_shape=jax.ShapeDtypeStruct((B, S, D), F32),
        compiler_params=_cparams(2),
        name="mem_attention",
    )(x, g, wq, kv, wo)


def _ffn_body(x_ref, halo_ref, g_ref, wg_ref, wv_ref, cwg_ref, cwv_ref, cbg_ref, cbv_ref, wo_ref, fg_ref,
              out_ref, hn_ref, hh_ref, acc_ref, *, final_norm):
    i = pl.program_id(1)
    c = pl.program_id(2)
    tm = x_ref.shape[1]

    @pl.when(c == 0)
    def _():
        hn_ref[...] = _rms(x_ref[0], g_ref[...]).astype(BF16)
        hh_ref[...] = _rms(halo_ref[0], g_ref[...]).astype(BF16)
        acc_ref[...] = jnp.zeros_like(acc_ref)

    live = jnp.where(i > 0, 1.0, 0.0)
    row = lax.broadcasted_iota(jnp.int32, (tm, 1), 0)

    def conv(w_ref, cw_ref, cb_ref):
        u = _dot(hn_ref[...], w_ref[...])
        uh = _dot(hh_ref[...], w_ref[...]) * live
        u1 = jnp.where(row == 0, uh[7:8], pltpu.roll(u, 1, 0))
        u2 = jnp.where(row == 0, uh[6:7], jnp.where(row == 1, uh[7:8], pltpu.roll(u, 2, 0)))
        cw = cw_ref[...]
        return cb_ref[...] + u2 * cw[0:1] + u1 * cw[1:2] + u * cw[2:3]

    gate = conv(wg_ref, cwg_ref, cbg_ref)
    val = conv(wv_ref, cwv_ref, cbv_ref)
    act = (gate * _sigmoid(gate) * val).astype(BF16)
    acc_ref[...] += _dot(act, wo_ref[...])

    @pl.when(c == pl.num_programs(2) - 1)
    def _():
        y = x_ref[0] + acc_ref[...]
        out_ref[0] = _rms(y, fg_ref[...]) if final_norm else y


def _ffn(x, g, w_in, conv_w, conv_b, w_out, final_g):
    B, S, D = x.shape
    d_ff = w_out.shape[0]
    tm = min(S, 512)
    fc = d_ff // 2 if (d_ff // 2) % LANES == 0 else d_ff
    nf = d_ff // fc
    w_in, w_out = w_in.astype(BF16), w_out.astype(BF16)
    conv_b = conv_b.reshape(1, 2 * d_ff)
    vec = pl.BlockSpec((1, D), lambda b, i, c: (0, 0))
    fg = (final_g if final_g is not None else g).reshape(1, D)
    return pl.pallas_call(
        functools.partial(_ffn_body, final_norm=final_g is not None),
        grid=(B, S // tm, nf),
        in_specs=[pl.BlockSpec((1, tm, D), lambda b, i, c: (b, i, 0)),
                  pl.BlockSpec((1, 8, D), lambda b, i, c: (b, jnp.maximum(i * (tm // 8) - 1, 0), 0)),
                  vec,
                  pl.BlockSpec((D, fc), lambda b, i, c: (0, c)),
                  pl.BlockSpec((D, fc), lambda b, i, c: (0, nf + c)),
                  pl.BlockSpec((3, fc), lambda b, i, c: (0, c)),
                  pl.BlockSpec((3, fc), lambda b, i, c: (0, nf + c)),
                  pl.BlockSpec((1, fc), lambda b, i, c: (0, c)),
                  pl.BlockSpec((1, fc), lambda b, i, c: (0, nf + c)),
                  pl.BlockSpec((fc, D), lambda b, i, c: (c, 0)),
                  vec],
        out_specs=pl.BlockSpec((1, tm, D), lambda b, i, c: (b, i, 0)),
        out_shape=jax.ShapeDtypeStruct((B, S, D), F32),
        scratch_shapes=[pltpu.VMEM((tm, D), BF16), pltpu.VMEM((8, D), BF16), pltpu.VMEM((tm, D), F32)],
        compiler_params=_cparams(3),
        name="conv_ffn",
    )(x, x, g.reshape(1, D), w_in, w_in, conv_w, conv_w, conv_b, conv_b, w_out, fg)


def kernel(x, mem, positions, mix_norm_g, w_in, diff_lam, diff_norm_g, hgrn_lb_logits, hgrn_norm_g, rwkv_mu, rwkv_w0, rwkv_w2, rwkv_a0, rwkv_a2, rwkv_g2, rwkv_k_k, rwkv_k_a, rwkv_r_k, rwkv_lnx_g, rwkv_lnx_b, rwkv_v0, rwkv_v1, rwkv_v2, p_a, p_b, p_c, p_d, w_mix_out, mem_q_norm_g, mem_kv_norm_g, w_mem_q, w_mem_kv, w_mem_o, ffn_norm_g, w_ffn_in, ffn_conv_w, ffn_conv_b, w_ffn_out, final_norm_g):
    B, S, D = x.shape
    M = mem.shape[1]
    T = B * S
    depth = w_in.shape[0]
    assert S % (DIL_PATTERNS[-1][1] * DIL_BLOCK) == 0 and S % CHUNK == 0

    half = ROPE_DIMS // 2
    inv_freq = ROPE_THETA ** (-jnp.arange(half, dtype=F32) / half)
    d = jnp.arange(LANES) % HEAD_DIM
    invf_lanes = jnp.where(d < ROPE_DIMS, inv_freq[d % half], 0.0).reshape(1, LANES)
    rope = (positions.reshape(T, 1), invf_lanes)
    lb_all = jnp.cumsum(jax.nn.softmax(hgrn_lb_logits.astype(F32), axis=0), axis=0)
    lb_all = lb_all - lb_all[0:1]
    offs = (0, SEG_A, SEG_A + SEG_B, SEG_A + SEG_B + SEG_C, SEG_A + SEG_B + SEG_C + SEG_D, w_in.shape[2])
    mem2 = mem.reshape(B * M, D)
    qkv_slabs = lambda w: (True,) * (2 * w // LANES) + (False,) * (w // LANES)

    v_first = None
    for l in range(depth):
        lam_init = 0.8 - 0.6 * math.exp(-0.3 * l)
        w_l = w_in[l].astype(BF16)
        x2 = x.reshape(T, D)
        seg = lambda s: w_l[:, offs[s]:offs[s + 1]]
        seg_a = _norm_matmul(x2, mix_norm_g[l], seg(0), 3 * MIX_W, rope, qkv_slabs(MIX_W)).reshape(B, S, SEG_A)
        seg_b = _norm_matmul(x2, mix_norm_g[l], seg(1), SEG_B, rope, qkv_slabs(DIFF_W), BF16).reshape(B, S, SEG_B)
        seg_c = _norm_matmul(x2, mix_norm_g[l], seg(2), SEG_C).reshape(B, S, SEG_C)
        seg_d = _norm_matmul(x2, mix_norm_g[l], seg(3), SEG_D).reshape(B, S, SEG_D)
        gates = _norm_matmul(x2, mix_norm_g[l], seg(4), D)
        dil = [_dilated_group(seg_a, g, dilation) for g, (_, dilation) in enumerate(DIL_PATTERNS)]
        y_b = _diff_attention(seg_b, diff_lam[l], diff_norm_g[l], lam_init)
        y_c = _hgrn(seg_c, lb_all[l], hgrn_norm_g[l])
        rp = dict(mu=rwkv_mu[l], w0=rwkv_w0[l], w2=rwkv_w2[l], a0=rwkv_a0[l], a2=rwkv_a2[l], g2=rwkv_g2[l],
                  k_k=rwkv_k_k[l], k_a=rwkv_k_a[l], r_k=rwkv_r_k[l], lnx_g=rwkv_lnx_g[l], lnx_b=rwkv_lnx_b[l])
        if l > 0:
            rp.update(v0=rwkv_v0[l - 1], v1=rwkv_v1[l - 1], v2=rwkv_v2[l - 1])
        y_d, v_first = _rwkv(seg_d, rp, v_first)
        x2 = _merge(dil, y_b, y_c, y_d, gates, x2, p_a[l], p_b[l], p_c[l], p_d[l], w_mix_out[l])
        kv = _norm_matmul(mem2, mem_kv_norm_g[l], w_mem_kv[l].astype(BF16), D).reshape(B, M, 2 * D)
        x = _mem_attention(x2.reshape(B, S, D), mem_q_norm_g[l], w_mem_q[l], kv, w_mem_o[l])
        x = _ffn(x, ffn_norm_g[l], w_ffn_in[l], ffn_conv_w[l], ffn_conv_b[l], w_ffn_out[l],
                 final_norm_g if l == depth - 1 else None)
    return x
```

```python
import functools
import math

import jax
import jax.numpy as jnp
from jax import lax
from jax.experimental import pallas as pl
from jax.experimental.pallas import tpu as pltpu

F32 = jnp.float32
BF16 = jnp.bfloat16

NORM_EPS = 1e-5
HEAD_DIM = 64
ROPE_THETA = 500000.0
ROPE_DIMS = HEAD_DIM // 4
DIL_PATTERNS = ((128, 1), (512, 4), (2048, 16))
DIL_BLOCK = 128
N_HEADS = 4
MIX_W = N_HEADS * HEAD_DIM
DIFF_W = 2 * MIX_W
SEG_A = 3 * 3 * MIX_W
SEG_B = 3 * DIFF_W
SEG_C = 4 * MIX_W
SEG_D = 3 * MIX_W + 64 + 64 + 128
RWKV_GN_EPS = 1e-5 * HEAD_DIM
CHUNK = 64
SUB = 16
LANES = 128
VMEM_LIMIT = 56 * 1024 * 1024
LOG2E = math.log2(math.e)


def _cparams(n_axes):
    return pltpu.CompilerParams(dimension_semantics=("arbitrary",) * n_axes,
                                vmem_limit_bytes=VMEM_LIMIT)


def _dot(a, b):
    return jnp.dot(a, b, preferred_element_type=F32)


def _dot_nt(a, b):
    return lax.dot_general(a, b, (((1,), (1,)), ((), ())), preferred_element_type=F32)


def _dot_tn(a, b):
    return lax.dot_general(a, b, (((0,), (0,)), ((), ())), preferred_element_type=F32)


def _sigmoid(x):
    return 1.0 / (1.0 + jnp.exp(-x))


def _softplus(x):
    return jnp.maximum(x, 0.0) + jnp.log1p(jnp.exp(-jnp.abs(x)))


def _rms(x, g):
    ms = jnp.mean(x * x, axis=-1, keepdims=True)
    return x * lax.rsqrt(ms + NORM_EPS) * g


def _head_sum_matrix(width):
    r = lax.broadcasted_iota(jnp.int32, (width, width), 0) // HEAD_DIM
    c = lax.broadcasted_iota(jnp.int32, (width, width), 1) // HEAD_DIM
    return r == c


def _cumsum_rows(x):
    n = x.shape[0]
    row = lax.broadcasted_iota(jnp.int32, x.shape, 0)
    d = 1
    while d < n:
        x = x + jnp.where(row >= d, pltpu.roll(x, d, 0), 0.0)
        d *= 2
    return x


def _sum_heads(x, e_bf):
    hi = x.astype(BF16)
    lo = (x - hi.astype(F32)).astype(BF16)
    return _dot(hi, e_bf) + _dot(lo, e_bf)


def _norm_matmul_body(*refs, rope_slabs):
    if rope_slabs:
        x_ref, g_ref, w_ref, pos_ref, invf_ref, o_ref, hn_ref, cos_ref, sa_ref, sb_ref = refs
    else:
        x_ref, g_ref, w_ref, o_ref, hn_ref = refs
    j = pl.program_id(1)

    @pl.when(j == 0)
    def _():
        hn_ref[...] = _rms(x_ref[...], g_ref[...]).astype(BF16)
        if rope_slabs:
            ang = pos_ref[...].astype(F32) * invf_ref[...]
            d = lax.broadcasted_iota(jnp.int32, (1, LANES), 1) % HEAD_DIM
            s = jnp.sin(ang)
            cos_ref[...] = jnp.cos(ang)
            sa_ref[...] = jnp.where(d < ROPE_DIMS // 2, -s, 0.0)
            sb_ref[...] = jnp.where((d >= ROPE_DIMS // 2) & (d < ROPE_DIMS), s, 0.0)

    acc = _dot(hn_ref[...], w_ref[...])
    if not rope_slabs:
        o_ref[...] = acc.astype(o_ref.dtype)
        return
    half = ROPE_DIMS // 2
    for c, roped in enumerate(rope_slabs):
        t = acc[:, c * LANES:(c + 1) * LANES]
        if roped:
            t = (t * cos_ref[...] + pltpu.roll(t, LANES - half, 1) * sa_ref[...]
                 + pltpu.roll(t, half, 1) * sb_ref[...])
        o_ref[:, c * LANES:(c + 1) * LANES] = t.astype(o_ref.dtype)


def _norm_matmul(x2, g, w_bf, tn, rope=None, rope_slabs=(), out_dtype=F32):
    T, D = x2.shape
    N = w_bf.shape[1]
    tm = min(T, 1024)
    assert len(rope_slabs) in (0, tn // LANES)
    in_specs = [pl.BlockSpec((tm, D), lambda i, j: (i, 0)),
                pl.BlockSpec((1, D), lambda i, j: (0, 0)),
                pl.BlockSpec((D, tn), lambda i, j: (0, j))]
    args = [x2, g.reshape(1, D), w_bf]
    scratch = [pltpu.VMEM((tm, D), BF16)]
    if rope_slabs:
        in_specs += [pl.BlockSpec((tm, 1), lambda i, j: (i, 0)),
                     pl.BlockSpec((1, LANES), lambda i, j: (0, 0))]
        args += list(rope)
        scratch += [pltpu.VMEM((tm, LANES), F32)] * 3
    return pl.pallas_call(
        functools.partial(_norm_matmul_body, rope_slabs=tuple(rope_slabs)),
        grid=(T // tm, N // tn),
        in_specs=in_specs,
        out_specs=pl.BlockSpec((tm, tn), lambda i, j: (i, j)),
        out_shape=jax.ShapeDtypeStruct((T, N), out_dtype),
        scratch_shapes=scratch,
        compiler_params=_cparams(2),
        name="norm_matmul_rope" if rope_slabs else "norm_matmul",
    )(*args)


def _dilated_body(q_ref, kc_ref, kp_ref, vc_ref, vp_ref, o_ref, lse_ref, *, dilation, nbs, width):
    n = pl.program_id(2)
    blk = DIL_BLOCK
    unit = blk * dilation
    n_lane_heads = LANES // HEAD_DIM
    lane_head = lax.broadcasted_iota(jnp.int32, (1, LANES), 1) // HEAD_DIM
    qi = lax.broadcasted_iota(jnp.int32, (blk, blk), 0)
    kj = lax.broadcasted_iota(jnp.int32, (blk, blk), 1)
    cur_ok = kj <= qi
    first_prev_ok = (kj - qi) >= jnp.where(n > 0, 0, 2 * blk)
    later_prev_ok = kj >= qi

    def rows_of(j, r):
        start = j * unit + r
        return pl.ds(start, blk) if dilation == 1 else pl.ds(start, blk, stride=dilation)

    def process(units):
        loaded = []
        for j, r in units:
            rows = rows_of(j, r)
            prows = rows_of(max(j - 1, 0), r)
            kpr, vpr = (kp_ref, vp_ref) if j == 0 else (kc_ref, vc_ref)
            loaded.append((rows, first_prev_ok if j == 0 else later_prev_ok,
                           q_ref[0, rows, :] * (HEAD_DIM ** -0.5),
                           kc_ref[0, rows, :].astype(BF16), kpr[0, prows, :].astype(BF16),
                           vc_ref[0, rows, :].astype(BF16), vpr[0, prows, :].astype(BF16)))
        heads = [(u, h) for u in range(len(units)) for h in range(n_lane_heads)]
        scores = []
        for u, h in heads:
            _, prev_ok, q, kc, kp, _, _ = loaded[u]
            qh = jnp.where(lane_head == h, q, 0.0).astype(BF16)
            scores.append((jnp.where(cur_ok, _dot_nt(qh, kc), -jnp.inf), jnp.where(prev_ok, _dot_nt(qh, kp), -jnp.inf)))
        stats = []
        for sc, sp in scores:
            m = jnp.maximum(jnp.max(sc, axis=-1, keepdims=True), jnp.max(sp, axis=-1, keepdims=True))
            pc = jnp.exp(sc - m)
            pp = jnp.exp(sp - m)
            den = jnp.sum(pc, axis=-1, keepdims=True) + jnp.sum(pp, axis=-1, keepdims=True)
            stats.append((m, pc, pp, den))
        outs = []
        for (u, h), (m, pc, pp, den) in zip(heads, stats):
            _, _, _, _, _, vc, vp = loaded[u]
            inv = 1.0 / den
            outs.append((_dot((pc * inv).astype(BF16), vc) + _dot((pp * inv).astype(BF16), vp), m + jnp.log(den)))
        for u in range(len(units)):
            o_acc = jnp.zeros((blk, LANES), F32)
            lse_acc = jnp.zeros((blk, LANES), F32)
            for h in range(n_lane_heads):
                oh, lse = outs[u * n_lane_heads + h]
                o_acc = jnp.where(lane_head == h, oh, o_acc)
                lse_acc = jnp.where(lane_head == h, lse, lse_acc)
            o_ref[0, loaded[u][0], :] = o_acc
            lse_ref[0, loaded[u][0], :] = lse_acc

    if dilation == 1:
        for j in range(0, nbs, width):
            process([(j + w, 0) for w in range(width)])
    else:
        step = dilation // width
        for j in range(nbs):
            def body(r, _, j=j):
                process([(j, r + w * step) for w in range(width)])
                return 0
            lax.fori_loop(0, step, body, 0)


def _dilated_group(seg_a, g, dilation, width=4):
    B, S, _ = seg_a.shape
    unit = DIL_BLOCK * dilation
    nbs = max(1, 512 // unit)
    tr = unit * nbs
    ncol = 2 * (g * 3)

    def cur(which):
        return pl.BlockSpec((1, tr, LANES), lambda b, p, n: (b, n, ncol + 2 * which + p))

    def prev(which):
        return pl.BlockSpec((1, unit, LANES), lambda b, p, n: (b, jnp.maximum(n * nbs - 1, 0), ncol + 2 * which + p))

    out_spec = pl.BlockSpec((1, tr, LANES), lambda b, p, n: (b, n, p))
    return pl.pallas_call(
        functools.partial(_dilated_body, dilation=dilation, nbs=nbs, width=width),
        grid=(B, MIX_W // LANES, S // tr),
        in_specs=[cur(0), cur(1), prev(1), cur(2), prev(2)],
        out_specs=[out_spec, out_spec],
        out_shape=[jax.ShapeDtypeStruct((B, S, MIX_W), F32)] * 2,
        compiler_params=_cparams(3),
        name="dilated_attention",
    )(seg_a, seg_a, seg_a, seg_a, seg_a)


def _diff_body(q_ref, k_ref, v_ref, lam_ref, g_ref, o_ref, vt_ref, a1_ref, a2_ref, sa_ref, sb_ref, *, tq, kc, lam_init):
    qi = pl.program_id(2)
    n_kv = vt_ref.shape[0]

    @pl.when(qi == 0)
    def _():
        for c in range(n_kv):
            vt_ref[c] = v_ref[0, c * kc:(c + 1) * kc, :].astype(F32).T.astype(BF16)

    q = q_ref[0].astype(F32) * (HEAD_DIM ** -0.5 * LOG2E)
    lane = lax.broadcasted_iota(jnp.int32, (1, 2 * HEAD_DIM), 1)
    q1 = jnp.where(lane < HEAD_DIM, q, 0.0).astype(BF16)
    q2 = jnp.where(lane >= HEAD_DIM, q, 0.0).astype(BF16)
    key = lax.broadcasted_iota(jnp.int32, (kc, tq), 0)
    qry = lax.broadcasted_iota(jnp.int32, (kc, tq), 1)
    a1_ref[...] = jnp.zeros_like(a1_ref)
    a2_ref[...] = jnp.zeros_like(a2_ref)

    def scores(kb, s_ref):
        k = k_ref[0, pl.ds(pl.multiple_of(kb * kc, kc), kc), :]
        s_ref[0] = _dot_nt(k, q1)
        s_ref[1] = _dot_nt(k, q2)

    def softmax_pv(kb, s_ref, carry, masked):
        m1, l1, m2, l2 = carry
        vt = vt_ref[kb]
        s1 = s_ref[0]
        s2 = s_ref[1]
        if masked:
            s1 = jnp.where(key <= qry, s1, -jnp.inf)
            s2 = jnp.where(key <= qry, s2, -jnp.inf)
        n1 = jnp.maximum(m1, jnp.max(s1, axis=0, keepdims=True))
        n2 = jnp.maximum(m2, jnp.max(s2, axis=0, keepdims=True))
        p1 = jnp.exp2(s1 - n1)
        p2 = jnp.exp2(s2 - n2)
        al1 = jnp.exp2(m1 - n1)
        al2 = jnp.exp2(m2 - n2)
        pv1 = _dot(vt, p1.astype(BF16))
        pv2 = _dot(vt, p2.astype(BF16))
        l1 = l1 * al1 + jnp.sum(p1.reshape(kc // 8, 8, tq), axis=0)
        l2 = l2 * al2 + jnp.sum(p2.reshape(kc // 8, 8, tq), axis=0)
        a1_ref[...] = a1_ref[...] * al1 + pv1
        a2_ref[...] = a2_ref[...] * al2 + pv2
        return n1, l1, n2, l2

    def pair(i, carry):
        scores(2 * i + 1, sb_ref)
        carry = softmax_pv(2 * i, sa_ref, carry, False)
        scores(2 * i + 2, sa_ref)
        return softmax_pv(2 * i + 1, sb_ref, carry, False)

    def finish(carry):
        _, l1, _, l2 = carry
        lv = lam_ref[...]
        lam = (jnp.exp(jnp.sum(lv[0:1] * lv[1:2], axis=-1, keepdims=True))
               - jnp.exp(jnp.sum(lv[2:3] * lv[3:4], axis=-1, keepdims=True)) + lam_init)
        o_t = (a1_ref[...] / jnp.sum(l1, axis=0, keepdims=True)
               - lam * (a2_ref[...] / jnp.sum(l2, axis=0, keepdims=True)))
        o_ref[0] = _rms(o_t.T, g_ref[...]) * (1.0 - lam_init)

    neg = jnp.full((1, tq), -jnp.inf, F32)
    zero = jnp.zeros((8, tq), F32)
    init = (neg, zero, neg, zero)
    scores(0, sa_ref)

    @pl.when(qi % 2 == 0)
    def _():
        carry = lax.fori_loop(0, qi // 2, pair, init)
        finish(softmax_pv(qi, sa_ref, carry, True))

    @pl.when(qi % 2 == 1)
    def _():
        carry = lax.fori_loop(0, qi // 2, pair, init)
        scores(qi, sb_ref)
        carry = softmax_pv(qi - 1, sa_ref, carry, False)
        finish(softmax_pv(qi, sb_ref, carry, True))


def _diff_attention(seg_b, lam_vecs, norm_g, lam_init, tq=512):
    B, S, _ = seg_b.shape
    tq = min(S, tq)
    kc = tq
    hw = 2 * HEAD_DIM
    return pl.pallas_call(
        functools.partial(_diff_body, tq=tq, kc=kc, lam_init=lam_init),
        grid=(B, N_HEADS, S // tq),
        in_specs=[pl.BlockSpec((1, tq, hw), lambda b, h, i: (b, i, h)),
                  pl.BlockSpec((1, S, hw), lambda b, h, i: (b, 0, N_HEADS + h)),
                  pl.BlockSpec((1, S, hw), lambda b, h, i: (b, 0, 2 * N_HEADS + h)),
                  pl.BlockSpec((4, HEAD_DIM), lambda b, h, i: (0, 0)),
                  pl.BlockSpec((1, hw), lambda b, h, i: (0, 0))],
        out_specs=pl.BlockSpec((1, tq, hw), lambda b, h, i: (b, i, h)),
        out_shape=jax.ShapeDtypeStruct((B, S, DIFF_W), F32),
        scratch_shapes=[pltpu.VMEM((S // kc, hw, kc), BF16), pltpu.VMEM((hw, tq), F32), pltpu.VMEM((hw, tq), F32),
                        pltpu.VMEM((2, kc, tq), F32), pltpu.VMEM((2, kc, tq), F32)],
        compiler_params=_cparams(3),
        name="diff_attention",
    )(seg_b, seg_b, seg_b, lam_vecs, norm_g.reshape(1, hw))


def _hgrn_body(x_ref, xg_ref, w_ref, la_ref, lc_ref, oml_ref, ng_ref, o_ref, st_ref, seg_ref, *, tr, unroll):
    t = pl.program_id(1)
    c = SUB
    W = MIX_W

    @pl.when(t == 0)
    def _():
        st_ref[...] = jnp.zeros_like(st_ref)

    seg_ref[...] = _dot(_rms(x_ref[0], xg_ref[...]).astype(BF16), w_ref[...])

    same_head = _head_sum_matrix(W)
    e_bf = jnp.where(same_head, 1.0, 0.0).astype(BF16)
    row = lax.broadcasted_iota(jnp.int32, (c, W), 0)

    def sub(j, _):
        rows = pl.ds(pl.multiple_of(j * c, c), c)
        qr = seg_ref[rows, 0:W]
        f = seg_ref[rows, W:2 * W]
        iv = seg_ref[rows, 2 * W:3 * W]
        gr = seg_ref[rows, 3 * W:4 * W]
        q = qr * _sigmoid(qr)
        log_sig = jnp.minimum(f, 0.0) - jnp.log1p(jnp.exp(-jnp.abs(f)))
        x1 = la_ref[...]
        x2 = lc_ref[...] + log_sig
        log_f = jnp.maximum(x1, x2) + jnp.log1p(jnp.exp(-jnp.abs(x1 - x2)))
        k = oml_ref[...] * _sigmoid(-f)
        b = _cumsum_rows(log_f)
        st = st_ref[...]
        o = _dot_nt((q * jnp.exp(b)).astype(BF16), st.astype(BF16))
        ps = []
        for s in range(c):
            e = jnp.exp(jnp.where(row >= s, b - b[s:s + 1], -jnp.inf))
            ps.append((q * e * k[s:s + 1]).astype(BF16))
        a = _dot(jnp.concatenate(ps, axis=0), e_bf)
        for s in range(c):
            o = o + a[s * c:(s + 1) * c] * iv[s:s + 1]
        b_last = b[c - 1:c]
        upd = _dot_tn(iv.astype(BF16), (k * jnp.exp(b_last - b)).astype(BF16))
        st_ref[...] = st * jnp.exp(b_last) + jnp.where(same_head, upd, 0.0)
        ms = _sum_heads(o * o, e_bf) * (1.0 / HEAD_DIM)
        o_ref[0, rows, :] = o * lax.rsqrt(ms + NORM_EPS) * ng_ref[...] * (gr * _sigmoid(gr))
        return 0

    lax.fori_loop(0, tr // c, sub, 0, unroll=unroll)


def _hgrn(x, mix_g, w_c, lb, norm_g, tr=256, unroll=4):
    B, S, D = x.shape
    tr = min(S, tr)
    W = MIX_W
    vec = pl.BlockSpec((1, W), lambda b, t: (0, 0))
    lb = lb.reshape(1, W)
    return pl.pallas_call(
        functools.partial(_hgrn_body, tr=tr, unroll=unroll),
        grid=(B, S // tr),
        in_specs=[pl.BlockSpec((1, tr, D), lambda b, t: (b, t, 0)),
                  pl.BlockSpec((1, D), lambda b, t: (0, 0)),
                  pl.BlockSpec((D, SEG_C), lambda b, t: (0, 0)),
                  vec, vec, vec, vec],
        out_specs=pl.BlockSpec((1, tr, W), lambda b, t: (b, t, 0)),
        out_shape=jax.ShapeDtypeStruct((B, S, W), F32),
        scratch_shapes=[pltpu.VMEM((W, W), F32), pltpu.VMEM((tr, SEG_C), F32)],
        compiler_params=_cparams(2),
        name="hgrn2",
    )(x, mix_g.reshape(1, D), w_c, jnp.log(lb), jnp.log1p(-lb), 1.0 - lb,
      jnp.tile(norm_g.reshape(1, HEAD_DIM), (1, N_HEADS)))


def _rwkv_body(*refs, tr, first_layer):
    if first_layer:
        (x_ref, mu_ref, w0_ref, w2_ref, a0_ref, a2_ref, g2_ref, kk_ref, ka_ref, rk_ref, lg_ref, lb_ref,
         y_ref, vf_out_ref,
         carry_ref, st_ref) = refs
    else:
        (x_ref, vf_ref, mu_ref, w0_ref, w2_ref, a0_ref, a2_ref, g2_ref, kk_ref, ka_ref, rk_ref, lg_ref, lb_ref,
         v0_ref, v1_ref, v2_ref,
         y_ref,
         carry_ref, st_ref) = refs
    t = pl.program_id(1)
    C = CHUNK
    W = MIX_W

    @pl.when(t == 0)
    def _():
        st_ref[...] = jnp.zeros_like(st_ref)
        carry_ref[...] = jnp.zeros_like(carry_ref)

    same_head = _head_sum_matrix(W)
    e_bf = jnp.where(same_head, 1.0, 0.0).astype(BF16)
    head = lax.broadcasted_iota(jnp.int32, (1, W), 1) // HEAD_DIM

    xs = x_ref[0]
    rowi = lax.broadcasted_iota(jnp.int32, xs.shape, 0)
    prev = jnp.where(rowi == 0, carry_ref[0:1, :], pltpu.roll(xs, 1, 0))
    carry_ref[0:1, :] = xs[tr - 1:tr, :]
    xm = xs + (prev - xs) * mu_ref[...]
    r = xm[:, 0:W]
    k = xm[:, W:2 * W]
    v = xm[:, 2 * W:3 * W]
    w_low = xm[:, 3 * W:3 * W + 64]
    a_low = xm[:, 3 * W + 64:3 * W + 128]
    g_low = xm[:, 3 * W + 128:3 * W + 256]
    wlog = -_softplus(-(w0_ref[...] + _dot(jnp.tanh(w_low).astype(BF16), w2_ref[...]))) - 0.5
    a = _sigmoid(a0_ref[...] + _dot(a_low.astype(BF16), a2_ref[...]))
    g = _dot(_sigmoid(g_low).astype(BF16), g2_ref[...])
    kkr = k * kk_ref[...]
    kn = kkr / jnp.maximum(jnp.sqrt(_sum_heads(kkr * kkr, e_bf)), 1e-12)
    k = k * (1.0 + (a - 1.0) * ka_ref[...])
    if first_layer:
        vf_out_ref[0] = v
    else:
        mix = _dot(_dot(v.astype(BF16), v1_ref[...]).astype(BF16), v2_ref[...])
        v = v + (vf_ref[0] - v) * _sigmoid(v0_ref[...] + mix)

    hc = N_HEADS * C
    tt = lax.broadcasted_iota(jnp.int32, (hc, hc), 0) % C
    ss = lax.broadcasted_iota(jnp.int32, (hc, hc), 1) % C
    strict = ss < tt
    incl = ss <= tt
    eye = jnp.where(lax.broadcasted_iota(jnp.int32, (hc, hc), 0) == lax.broadcasted_iota(jnp.int32, (hc, hc), 1),
                    1.0, 0.0)

    def stack(x):
        return jnp.concatenate([jnp.where(head == h, x, 0.0) for h in range(N_HEADS)], axis=0).astype(BF16)

    def fold(x):
        return x[0:C] + x[C:2 * C] + x[2 * C:3 * C] + x[3 * C:4 * C]

    n_ch = tr // C
    chunks = range(n_ch)
    sl = lambda x, c: x[c * C:(c + 1) * C]
    lw_all = -jnp.exp(wlog)
    bb = kn * a
    cs = [_cumsum_rows(sl(lw_all, c)) for c in chunks]
    r_st = [stack(sl(r, c) * jnp.exp(cs[c])) for c in chunks]
    n_st = [stack(sl(kn, c) * jnp.exp(cs[c] - sl(lw_all, c))) for c in chunks]
    kb_st = [jnp.concatenate([stack(sl(k, c) * jnp.exp(-cs[c])), stack(sl(bb, c) * jnp.exp(-cs[c]))], axis=0)
             for c in chunks]
    v_st = [stack(sl(v, c)) for c in chunks]
    gram_n = [_dot_nt(n_st[c], kb_st[c]) for c in chunks]
    gram_r = [_dot_nt(r_st[c], kb_st[c]) for c in chunks]
    l_k = [jnp.where(strict, g[:, :hc], 0.0).astype(BF16) for g in gram_n]
    l_b = [jnp.where(strict, g[:, hc:], 0.0) for g in gram_n]
    m_k = [jnp.where(incl, g[:, :hc], 0.0).astype(BF16) for g in gram_r]
    m_b = [jnp.where(incl, g[:, hc:], 0.0).astype(BF16) for g in gram_r]
    inv_t = [eye - x for x in l_b]
    pw = [_dot(x.astype(BF16), x.astype(BF16)) for x in l_b]
    n = 2
    while n < C:
        inv_t = [t_ + _dot(t_.astype(BF16), p_.astype(BF16)) for t_, p_ in zip(inv_t, pw)]
        n *= 2
        if n < C:
            pw = [_dot(p_.astype(BF16), p_.astype(BF16)) for p_ in pw]
    lkv = [_dot(l_k[c], v_st[c]) for c in chunks]
    mkv = [_dot(m_k[c], v_st[c]) for c in chunks]
    tw = [_dot(inv_t[c].astype(BF16), jnp.concatenate([n_st[c], lkv[c].astype(BF16)], axis=1)) for c in chunks]
    mw = [_dot(m_b[c], tw[c].astype(BF16)) for c in chunks]
    wnr = [jnp.concatenate([tw[c][:, :W], r_st[c].astype(F32) - mw[c][:, :W]], axis=0).astype(BF16) for c in chunks]
    u_a = [fold(tw[c][:, W:]) for c in chunks]
    y_a = [fold(mkv[c] - mw[c][:, W:]) for c in chunks]
    st = st_ref[...]
    ys = []
    for c in chunks:
        x = _dot_nt(wnr[c], st.astype(BF16))
        u = fold(x[:hc]) + u_a[c]
        ys.append(fold(x[hc:]) + y_a[c])
        c_last = cs[c][C - 1:C, :]
        dec = jnp.exp(c_last - cs[c])
        vu = jnp.concatenate([sl(v, c), -u], axis=0).astype(BF16)
        kb_end = jnp.concatenate([sl(k, c) * dec, sl(bb, c) * dec], axis=0).astype(BF16)
        st = st * jnp.exp(c_last) + jnp.where(same_head, _dot_tn(vu, kb_end), 0.0)
    st_ref[...] = st
    y = jnp.concatenate(ys, axis=0)

    mean = _sum_heads(y, e_bf) * (1.0 / HEAD_DIM)
    yc = y - mean
    var = _sum_heads(yc * yc, e_bf) * (1.0 / HEAD_DIM)
    yn = yc * lax.rsqrt(var + RWKV_GN_EPS) * lg_ref[...] + lb_ref[...]
    bonus = _sum_heads(r * k * rk_ref[...], e_bf) * v
    y_ref[0] = (yn + bonus) * g


def _rwkv(seg_d, p, v_first, tr=256):
    B, S, _ = seg_d.shape
    tr = min(S, tr)
    W = MIX_W
    first = v_first is None
    row = lambda a: a.reshape(1, -1)
    full = lambda a: pl.BlockSpec(a.shape, lambda b, t: (0,) * a.ndim)
    tile = lambda w: pl.BlockSpec((1, tr, w), lambda b, t: (b, t, 0))
    params = [row(p['mu']), row(p['w0']), p['w2'].astype(BF16), row(p['a0']), p['a2'].astype(BF16),
              p['g2'].astype(BF16), row(p['k_k']), row(p['k_a']), row(p['r_k']), row(p['lnx_g']), row(p['lnx_b'])]
    args = [seg_d]
    in_specs = [tile(SEG_D)]
    if not first:
        args.append(v_first)
        in_specs.append(tile(W))
        params += [row(p['v0']), p['v1'].astype(BF16), p['v2'].astype(BF16)]
    args += params
    in_specs += [full(a) for a in params]
    n_out = 2 if first else 1
    outs = pl.pallas_call(
        functools.partial(_rwkv_body, tr=tr, first_layer=first),
        grid=(B, S // tr),
        in_specs=in_specs,
        out_specs=[tile(W)] * n_out,
        out_shape=[jax.ShapeDtypeStruct((B, S, W), F32)] * n_out,
        scratch_shapes=[pltpu.VMEM((8, SEG_D), F32), pltpu.VMEM((W, W), F32)],
        compiler_params=_cparams(2),
        name="rwkv7",
    )(*args)
    return (outs[0], outs[1]) if first else (outs[0], v_first)


def _merge_body(o0, l0, o1, l1, o2, l2, yb, yc, yd, x_ref, ng_ref, wg, pa, pb, pc, pd, wo, out_ref):
    x = x_ref[...]
    D = x.shape[-1]
    hn = _rms(x, ng_ref[...]).astype(BF16)
    la, lb, lc = l0[...], l1[...], l2[...]
    m = jnp.maximum(jnp.maximum(la, lb), lc)
    e0, e1, e2 = jnp.exp(la - m), jnp.exp(lb - m), jnp.exp(lc - m)
    inv = 1.0 / (e0 + e1 + e2)
    y_a = (e0 * inv) * o0[...] + (e1 * inv) * o1[...] + (e2 * inv) * o2[...]
    merged = jnp.zeros_like(x)
    for j, (y, p) in enumerate(((y_a, pa), (yb[...], pb), (yc[...], pc), (yd[...], pd))):
        gate = _sigmoid(_dot(hn, wg[:, j * D:(j + 1) * D]))
        merged = merged + gate * _dot(y.astype(BF16), p[...])
    out_ref[...] = x + _dot(merged.astype(BF16), wo[...])


def _merge(dil, y_b, y_c, y_d, x2, norm_g, w_gate, p_a, p_b, p_c, p_d, w_out):
    T, D = x2.shape
    tm = min(T, 512)
    rows = lambda w: pl.BlockSpec((tm, w), lambda i: (i, 0))
    full = lambda a: pl.BlockSpec(a.shape, lambda i: (0, 0))
    flat = lambda a: a.reshape(T, a.shape[-1])
    acts = []
    for o, lse in dil:
        acts += [flat(o), flat(lse)]
    acts += [flat(y_b), flat(y_c), flat(y_d)]
    weights = [norm_g.reshape(1, D), w_gate] + [w.astype(BF16) for w in (p_a, p_b, p_c, p_d, w_out)]
    in_specs = ([rows(MIX_W)] * 6 + [rows(DIFF_W), rows(MIX_W), rows(MIX_W), rows(D)] + [full(w) for w in weights])
    return pl.pallas_call(
        _merge_body,
        grid=(T // tm,),
        in_specs=in_specs,
        out_specs=rows(D),
        out_shape=jax.ShapeDtypeStruct((T, D), F32),
        compiler_params=_cparams(1),
        name="gated_merge",
    )(*acts, x2, *weights)


def _mem_body(x_ref, g_ref, wq_ref, kv_ref, wo_ref, out_ref, *, n_heads):
    x = x_ref[0]
    D = x.shape[-1]
    dm = D // n_heads
    q = _dot(_rms(x, g_ref[...]).astype(BF16), wq_ref[...]) * (dm ** -0.5)
    kv = kv_ref[0].astype(BF16)
    outs = []
    for h in range(n_heads):
        s = _dot_nt(q[:, h * dm:(h + 1) * dm].astype(BF16), kv[:, h * dm:(h + 1) * dm])
        p = jnp.exp(s - jnp.max(s, axis=-1, keepdims=True))
        p = p / jnp.sum(p, axis=-1, keepdims=True)
        outs.append(_dot(p.astype(BF16), kv[:, D + h * dm:D + (h + 1) * dm]))
    o = jnp.concatenate(outs, axis=-1)
    out_ref[0] = x + _dot(o.astype(BF16), wo_ref[...])


def _mem_attention(x, g, w_q, kv, w_o, n_heads=4):
    B, S, D = x.shape
    M = kv.shape[1]
    tm = min(S, 512)
    full = lambda a: pl.BlockSpec(a.shape, lambda b, i: (0, 0))
    wq, wo = w_q.astype(BF16), w_o.astype(BF16)
    g = g.reshape(1, D)
    return pl.pallas_call(
        functools.partial(_mem_body, n_heads=n_heads),
        grid=(B, S // tm),
        in_specs=[pl.BlockSpec((1, tm, D), lambda b, i: (b, i, 0)), full(g), full(wq),
                  pl.BlockSpec((1, M, 2 * D), lambda b, i: (b, 0, 0)), full(wo)],
        out_specs=pl.BlockSpec((1, tm, D), lambda b, i: (b, i, 0)),
        out_shape=jax.ShapeDtypeStruct((B, S, D), F32),
        compiler_params=_cparams(2),
        name="mem_attention",
    )(x, g, wq, kv, wo)


def _ffn_body(x_ref, halo_ref, g_ref, wg_ref, wv_ref, cwg_ref, cwv_ref, cbg_ref, cbv_ref, wo_ref, fg_ref,
              out_ref, hn_ref, hh_ref, acc_ref, *, final_norm):
    i = pl.program_id(1)
    c = pl.program_id(2)
    tm = x_ref.shape[1]

    @pl.when(c == 0)
    def _():
        hn_ref[...] = _rms(x_ref[0], g_ref[...]).astype(BF16)
        hh_ref[...] = _rms(halo_ref[0], g_ref[...]).astype(BF16)
        acc_ref[...] = jnp.zeros_like(acc_ref)

    live = jnp.where(i > 0, 1.0, 0.0)
    row = lax.broadcasted_iota(jnp.int32, (tm, 1), 0)

    def conv(w_ref, cw_ref, cb_ref):
        u = _dot(hn_ref[...], w_ref[...])
        uh = _dot(hh_ref[...], w_ref[...]) * live
        u1 = jnp.where(row == 0, uh[7:8], pltpu.roll(u, 1, 0))
        u2 = jnp.where(row == 0, uh[6:7], jnp.where(row == 1, uh[7:8], pltpu.roll(u, 2, 0)))
        cw = cw_ref[...]
        return cb_ref[...] + u2 * cw[0:1] + u1 * cw[1:2] + u * cw[2:3]

    gate = conv(wg_ref, cwg_ref, cbg_ref)
    val = conv(wv_ref, cwv_ref, cbv_ref)
    act = (gate * _sigmoid(gate) * val).astype(BF16)
    acc_ref[...] += _dot(act, wo_ref[...])

    @pl.when(c == pl.num_programs(2) - 1)
    def _():
        y = x_ref[0] + acc_ref[...]
        out_ref[0] = _rms(y, fg_ref[...]) if final_norm else y


def _ffn(x, g, w_in, conv_w, conv_b, w_out, final_g):
    B, S, D = x.shape
    d_ff = w_out.shape[0]
    tm = min(S, 512)
    fc = d_ff // 2 if (d_ff // 2) % LANES == 0 else d_ff
    nf = d_ff // fc
    w_in, w_out = w_in.astype(BF16), w_out.astype(BF16)
    conv_b = conv_b.reshape(1, 2 * d_ff)
    vec = pl.BlockSpec((1, D), lambda b, i, c: (0, 0))
    fg = (final_g if final_g is not None else g).reshape(1, D)
    return pl.pallas_call(
        functools.partial(_ffn_body, final_norm=final_g is not None),
        grid=(B, S // tm, nf),
        in_specs=[pl.BlockSpec((1, tm, D), lambda b, i, c: (b, i, 0)),
                  pl.BlockSpec((1, 8, D), lambda b, i, c: (b, jnp.maximum(i * (tm // 8) - 1, 0), 0)),
                  vec,
                  pl.BlockSpec((D, fc), lambda b, i, c: (0, c)),
                  pl.BlockSpec((D, fc), lambda b, i, c: (0, nf + c)),
                  pl.BlockSpec((3, fc), lambda b, i, c: (0, c)),
                  pl.BlockSpec((3, fc), lambda b, i, c: (0, nf + c)),
                  pl.BlockSpec((1, fc), lambda b, i, c: (0, c)),
                  pl.BlockSpec((1, fc), lambda b, i, c: (0, nf + c)),
                  pl.BlockSpec((fc, D), lambda b, i, c: (c, 0)),
                  vec],
        out_specs=pl.BlockSpec((1, tm, D), lambda b, i, c: (b, i, 0)),
        out_shape=jax.ShapeDtypeStruct((B, S, D), F32),
        scratch_shapes=[pltpu.VMEM((tm, D), BF16), pltpu.VMEM((8, D), BF16), pltpu.VMEM((tm, D), F32)],
        compiler_params=_cparams(3),
        name="conv_ffn",
    )(x, x, g.reshape(1, D), w_in, w_in, conv_w, conv_w, conv_b, conv_b, w_out, fg)


def kernel(x, mem, positions, mix_norm_g, w_in, diff_lam, diff_norm_g, hgrn_lb_logits, hgrn_norm_g, rwkv_mu, rwkv_w0, rwkv_w2, rwkv_a0, rwkv_a2, rwkv_g2, rwkv_k_k, rwkv_k_a, rwkv_r_k, rwkv_lnx_g, rwkv_lnx_b, rwkv_v0, rwkv_v1, rwkv_v2, p_a, p_b, p_c, p_d, w_mix_out, mem_q_norm_g, mem_kv_norm_g, w_mem_q, w_mem_kv, w_mem_o, ffn_norm_g, w_ffn_in, ffn_conv_w, ffn_conv_b, w_ffn_out, final_norm_g):
    B, S, D = x.shape
    M = mem.shape[1]
    T = B * S
    depth = w_in.shape[0]
    assert S % (DIL_PATTERNS[-1][1] * DIL_BLOCK) == 0 and S % CHUNK == 0

    half = ROPE_DIMS // 2
    inv_freq = ROPE_THETA ** (-jnp.arange(half, dtype=F32) / half)
    d = jnp.arange(LANES) % HEAD_DIM
    invf_lanes = jnp.where(d < ROPE_DIMS, inv_freq[d % half], 0.0).reshape(1, LANES)
    rope = (positions.reshape(T, 1), invf_lanes)
    lb_all = jnp.cumsum(jax.nn.softmax(hgrn_lb_logits.astype(F32), axis=0), axis=0)
    lb_all = lb_all - lb_all[0:1]
    offs = (0, SEG_A, SEG_A + SEG_B, SEG_A + SEG_B + SEG_C, SEG_A + SEG_B + SEG_C + SEG_D, w_in.shape[2])
    mem2 = mem.reshape(B * M, D)
    qkv_slabs = lambda w: (True,) * (2 * w // LANES) + (False,) * (w // LANES)

    v_first = None
    for l in range(depth):
        lam_init = 0.8 - 0.6 * math.exp(-0.3 * l)
        w_l = w_in[l].astype(BF16)
        x2 = x.reshape(T, D)
        seg = lambda s: w_l[:, offs[s]:offs[s + 1]]
        seg_a = _norm_matmul(x2, mix_norm_g[l], seg(0), 3 * MIX_W, rope, qkv_slabs(MIX_W)).reshape(B, S, SEG_A)
        seg_b = _norm_matmul(x2, mix_norm_g[l], seg(1), SEG_B, rope, qkv_slabs(DIFF_W), BF16).reshape(B, S, SEG_B)
        seg_d = _norm_matmul(x2, mix_norm_g[l], seg(3), SEG_D).reshape(B, S, SEG_D)
        dil = [_dilated_group(seg_a, g, dilation) for g, (_, dilation) in enumerate(DIL_PATTERNS)]
        y_b = _diff_attention(seg_b, diff_lam[l], diff_norm_g[l], lam_init)
        y_c = _hgrn(x, mix_norm_g[l], seg(2), lb_all[l], hgrn_norm_g[l])
        rp = dict(mu=rwkv_mu[l], w0=rwkv_w0[l], w2=rwkv_w2[l], a0=rwkv_a0[l], a2=rwkv_a2[l], g2=rwkv_g2[l],
                  k_k=rwkv_k_k[l], k_a=rwkv_k_a[l], r_k=rwkv_r_k[l], lnx_g=rwkv_lnx_g[l], lnx_b=rwkv_lnx_b[l])
        if l > 0:
            rp.update(v0=rwkv_v0[l - 1], v1=rwkv_v1[l - 1], v2=rwkv_v2[l - 1])
        y_d, v_first = _rwkv(seg_d, rp, v_first)
        x2 = _merge(dil, y_b, y_c, y_d, x2, mix_norm_g[l], seg(4), p_a[l], p_b[l], p_c[l], p_d[l], w_mix_out[l])
        kv = _norm_matmul(mem2, mem_kv_norm_g[l], w_mem_kv[l].astype(BF16), D).reshape(B, M, 2 * D)
        x = _mem_attention(x2.reshape(B, S, D), mem_q_norm_g[l], w_mem_q[l], kv, w_mem_o[l])
        x = _ffn(x, ffn_norm_g[l], w_ffn_in[l], ffn_conv_w[l], ffn_conv_b[l], w_ffn_out[l],
                 final_norm_g if l == depth - 1 else None)
    return x
```

```python
import functools
import math

import jax
import jax.numpy as jnp
from jax import lax
from jax.experimental import pallas as pl
from jax.experimental.pallas import tpu as pltpu

F32 = jnp.float32
BF16 = jnp.bfloat16

NORM_EPS = 1e-5
HEAD_DIM = 64
ROPE_THETA = 500000.0
ROPE_DIMS = HEAD_DIM // 4
DIL_PATTERNS = ((128, 1), (512, 4), (2048, 16))
DIL_BLOCK = 128
N_HEADS = 4
MIX_W = N_HEADS * HEAD_DIM
DIFF_W = 2 * MIX_W
SEG_A = 3 * 3 * MIX_W
SEG_B = 3 * DIFF_W
SEG_C = 4 * MIX_W
SEG_D = 3 * MIX_W + 64 + 64 + 128
RWKV_GN_EPS = 1e-5 * HEAD_DIM
CHUNK = 64
SUB = 16
LANES = 128
VMEM_LIMIT = 56 * 1024 * 1024
LOG2E = math.log2(math.e)


def _cparams(n_axes):
    return pltpu.CompilerParams(dimension_semantics=("arbitrary",) * n_axes,
                                vmem_limit_bytes=VMEM_LIMIT)


def _dot(a, b):
    return jnp.dot(a, b, preferred_element_type=F32)


def _dot_nt(a, b):
    return lax.dot_general(a, b, (((1,), (1,)), ((), ())), preferred_element_type=F32)


def _dot_tn(a, b):
    return lax.dot_general(a, b, (((0,), (0,)), ((), ())), preferred_element_type=F32)


def _sigmoid(x):
    return 1.0 / (1.0 + jnp.exp(-x))


def _softplus(x):
    return jnp.maximum(x, 0.0) + jnp.log1p(jnp.exp(-jnp.abs(x)))


def _rms(x, g):
    ms = jnp.mean(x * x, axis=-1, keepdims=True)
    return x * lax.rsqrt(ms + NORM_EPS) * g


def _head_sum_matrix(width):
    r = lax.broadcasted_iota(jnp.int32, (width, width), 0) // HEAD_DIM
    c = lax.broadcasted_iota(jnp.int32, (width, width), 1) // HEAD_DIM
    return r == c


def _cumsum_rows(x):
    n = x.shape[0]
    row = lax.broadcasted_iota(jnp.int32, x.shape, 0)
    d = 1
    while d < n:
        x = x + jnp.where(row >= d, pltpu.roll(x, d, 0), 0.0)
        d *= 2
    return x


def _sum_heads(x, e_bf):
    hi = x.astype(BF16)
    lo = (x - hi.astype(F32)).astype(BF16)
    return _dot(hi, e_bf) + _dot(lo, e_bf)


def _norm_matmul_body(*refs, rope_slabs):
    if rope_slabs:
        x_ref, g_ref, w_ref, pos_ref, invf_ref, o_ref, hn_ref, cos_ref, sa_ref, sb_ref = refs
    else:
        x_ref, g_ref, w_ref, o_ref, hn_ref = refs
    j = pl.program_id(1)

    @pl.when(j == 0)
    def _():
        hn_ref[...] = _rms(x_ref[...], g_ref[...]).astype(BF16)
        if rope_slabs:
            ang = pos_ref[...].astype(F32) * invf_ref[...]
            d = lax.broadcasted_iota(jnp.int32, (1, LANES), 1) % HEAD_DIM
            s = jnp.sin(ang)
            cos_ref[...] = jnp.cos(ang)
            sa_ref[...] = jnp.where(d < ROPE_DIMS // 2, -s, 0.0)
            sb_ref[...] = jnp.where((d >= ROPE_DIMS // 2) & (d < ROPE_DIMS), s, 0.0)

    acc = _dot(hn_ref[...], w_ref[...])
    if not rope_slabs:
        o_ref[...] = acc.astype(o_ref.dtype)
        return
    half = ROPE_DIMS // 2
    for c, roped in enumerate(rope_slabs):
        t = acc[:, c * LANES:(c + 1) * LANES]
        if roped:
            t = (t * cos_ref[...] + pltpu.roll(t, LANES - half, 1) * sa_ref[...]
                 + pltpu.roll(t, half, 1) * sb_ref[...])
        o_ref[:, c * LANES:(c + 1) * LANES] = t.astype(o_ref.dtype)


def _norm_matmul(x2, g, w_bf, tn, rope=None, rope_slabs=(), out_dtype=F32):
    T, D = x2.shape
    N = w_bf.shape[1]
    tm = min(T, 1024)
    assert len(rope_slabs) in (0, tn // LANES)
    in_specs = [pl.BlockSpec((tm, D), lambda i, j: (i, 0)),
                pl.BlockSpec((1, D), lambda i, j: (0, 0)),
                pl.BlockSpec((D, tn), lambda i, j: (0, j))]
    args = [x2, g.reshape(1, D), w_bf]
    scratch = [pltpu.VMEM((tm, D), BF16)]
    if rope_slabs:
        in_specs += [pl.BlockSpec((tm, 1), lambda i, j: (i, 0)),
                     pl.BlockSpec((1, LANES), lambda i, j: (0, 0))]
        args += list(rope)
        scratch += [pltpu.VMEM((tm, LANES), F32)] * 3
    return pl.pallas_call(
        functools.partial(_norm_matmul_body, rope_slabs=tuple(rope_slabs)),
        grid=(T // tm, N // tn),
        in_specs=in_specs,
        out_specs=pl.BlockSpec((tm, tn), lambda i, j: (i, j)),
        out_shape=jax.ShapeDtypeStruct((T, N), out_dtype),
        scratch_shapes=scratch,
        compiler_params=_cparams(2),
        name="norm_matmul_rope" if rope_slabs else "norm_matmul",
    )(*args)


def _dilated_body(q_ref, kc_ref, kp_ref, vc_ref, vp_ref, o_ref, lse_ref, *, dilation, nbs, width):
    n = pl.program_id(2)
    blk = DIL_BLOCK
    unit = blk * dilation
    n_lane_heads = LANES // HEAD_DIM
    lane_head = lax.broadcasted_iota(jnp.int32, (1, LANES), 1) // HEAD_DIM
    qi = lax.broadcasted_iota(jnp.int32, (blk, blk), 0)
    kj = lax.broadcasted_iota(jnp.int32, (blk, blk), 1)
    cur_ok = kj <= qi
    first_prev_ok = (kj - qi) >= jnp.where(n > 0, 0, 2 * blk)
    later_prev_ok = kj >= qi
    ones = jnp.ones((blk, LANES), BF16)

    def rows_of(j, r):
        start = j * unit + r
        return pl.ds(start, blk) if dilation == 1 else pl.ds(start, blk, stride=dilation)

    def process(units):
        loaded = []
        for j, r in units:
            rows = rows_of(j, r)
            prows = rows_of(max(j - 1, 0), r)
            kpr, vpr = (kp_ref, vp_ref) if j == 0 else (kc_ref, vc_ref)
            loaded.append((rows, first_prev_ok if j == 0 else later_prev_ok,
                           q_ref[0, rows, :] * (HEAD_DIM ** -0.5),
                           kc_ref[0, rows, :].astype(BF16), kpr[0, prows, :].astype(BF16),
                           jnp.concatenate([vc_ref[0, rows, :].astype(BF16), ones], axis=1),
                           jnp.concatenate([vpr[0, prows, :].astype(BF16), ones], axis=1)))
        heads = [(u, h) for u in range(len(units)) for h in range(n_lane_heads)]
        scores = []
        for u, h in heads:
            _, prev_ok, q, kc, kp, _, _ = loaded[u]
            qh = jnp.where(lane_head == h, q, 0.0).astype(BF16)
            scores.append((jnp.where(cur_ok, _dot_nt(qh, kc), -jnp.inf), jnp.where(prev_ok, _dot_nt(qh, kp), -jnp.inf)))
        probs = []
        for sc, sp in scores:
            m = jnp.max(jnp.maximum(sc, sp), axis=-1, keepdims=True)
            probs.append((m, jnp.exp(sc - m).astype(BF16), jnp.exp(sp - m).astype(BF16)))
        outs = []
        for (u, h), (m, pc, pp) in zip(heads, probs):
            ext = _dot(pc, loaded[u][5]) + _dot(pp, loaded[u][6])
            den = ext[:, LANES:]
            outs.append((ext[:, :LANES] / den, m + jnp.log(den)))
        for u in range(len(units)):
            o_acc = jnp.zeros((blk, LANES), F32)
            lse_acc = jnp.zeros((blk, LANES), F32)
            for h in range(n_lane_heads):
                oh, lse = outs[u * n_lane_heads + h]
                o_acc = jnp.where(lane_head == h, oh, o_acc)
                lse_acc = jnp.where(lane_head == h, lse, lse_acc)
            o_ref[0, loaded[u][0], :] = o_acc
            lse_ref[0, loaded[u][0], :] = lse_acc

    if dilation == 1:
        for j in range(0, nbs, width):
            process([(j + w, 0) for w in range(width)])
    else:
        step = dilation // width
        for j in range(nbs):
            def body(r, _, j=j):
                process([(j, r + w * step) for w in range(width)])
                return 0
            lax.fori_loop(0, step, body, 0)


def _dilated_group(seg_a, g, dilation, width=4):
    B, S, _ = seg_a.shape
    unit = DIL_BLOCK * dilation
    nbs = max(1, 512 // unit)
    tr = unit * nbs
    ncol = 2 * (g * 3)

    def cur(which):
        return pl.BlockSpec((1, tr, LANES), lambda b, p, n: (b, n, ncol + 2 * which + p))

    def prev(which):
        return pl.BlockSpec((1, unit, LANES), lambda b, p, n: (b, jnp.maximum(n * nbs - 1, 0), ncol + 2 * which + p))

    out_spec = pl.BlockSpec((1, tr, LANES), lambda b, p, n: (b, n, p))
    return pl.pallas_call(
        functools.partial(_dilated_body, dilation=dilation, nbs=nbs, width=width),
        grid=(B, MIX_W // LANES, S // tr),
        in_specs=[cur(0), cur(1), prev(1), cur(2), prev(2)],
        out_specs=[out_spec, out_spec],
        out_shape=[jax.ShapeDtypeStruct((B, S, MIX_W), F32)] * 2,
        compiler_params=_cparams(3),
        name="dilated_attention",
    )(seg_a, seg_a, seg_a, seg_a, seg_a)


def _diff_body(q_ref, k_ref, v_ref, lam_ref, g_ref, o_ref, vt_ref, a1_ref, a2_ref, sa_ref, sb_ref, *, tq, kc, lam_init):
    qi = pl.program_id(2)
    n_kv = vt_ref.shape[0]

    @pl.when(qi == 0)
    def _():
        for c in range(n_kv):
            vt_ref[c] = v_ref[0, c * kc:(c + 1) * kc, :].astype(F32).T.astype(BF16)

    q = q_ref[0].astype(F32) * (HEAD_DIM ** -0.5 * LOG2E)
    lane = lax.broadcasted_iota(jnp.int32, (1, 2 * HEAD_DIM), 1)
    q1 = jnp.where(lane < HEAD_DIM, q, 0.0).astype(BF16)
    q2 = jnp.where(lane >= HEAD_DIM, q, 0.0).astype(BF16)
    key = lax.broadcasted_iota(jnp.int32, (kc, tq), 0)
    qry = lax.broadcasted_iota(jnp.int32, (kc, tq), 1)
    a1_ref[...] = jnp.zeros_like(a1_ref)
    a2_ref[...] = jnp.zeros_like(a2_ref)

    def scores(kb, s_ref):
        k = k_ref[0, pl.ds(pl.multiple_of(kb * kc, kc), kc), :]
        s_ref[0] = _dot_nt(k, q1)
        s_ref[1] = _dot_nt(k, q2)

    def softmax_pv(kb, s_ref, carry, masked):
        m1, l1, m2, l2 = carry
        vt = vt_ref[kb]
        s1 = s_ref[0]
        s2 = s_ref[1]
        if masked:
            s1 = jnp.where(key <= qry, s1, -jnp.inf)
            s2 = jnp.where(key <= qry, s2, -jnp.inf)
        n1 = jnp.maximum(m1, jnp.max(s1, axis=0, keepdims=True))
        n2 = jnp.maximum(m2, jnp.max(s2, axis=0, keepdims=True))
        p1 = jnp.exp2(s1 - n1)
        p2 = jnp.exp2(s2 - n2)
        al1 = jnp.exp2(m1 - n1)
        al2 = jnp.exp2(m2 - n2)
        pv1 = _dot(vt, p1.astype(BF16))
        pv2 = _dot(vt, p2.astype(BF16))
        l1 = l1 * al1 + jnp.sum(p1.reshape(kc // 8, 8, tq), axis=0)
        l2 = l2 * al2 + jnp.sum(p2.reshape(kc // 8, 8, tq), axis=0)
        a1_ref[...] = a1_ref[...] * al1 + pv1
        a2_ref[...] = a2_ref[...] * al2 + pv2
        return n1, l1, n2, l2

    def pair(i, carry):
        scores(2 * i + 1, sb_ref)
        carry = softmax_pv(2 * i, sa_ref, carry, False)
        scores(2 * i + 2, sa_ref)
        return softmax_pv(2 * i + 1, sb_ref, carry, False)

    def finish(carry):
        _, l1, _, l2 = carry
        lv = lam_ref[...]
        lam = (jnp.exp(jnp.sum(lv[0:1] * lv[1:2], axis=-1, keepdims=True))
               - jnp.exp(jnp.sum(lv[2:3] * lv[3:4], axis=-1, keepdims=True)) + lam_init)
        o_t = (a1_ref[...] / jnp.sum(l1, axis=0, keepdims=True)
               - lam * (a2_ref[...] / jnp.sum(l2, axis=0, keepdims=True)))
        o_ref[0] = _rms(o_t.T, g_ref[...]) * (1.0 - lam_init)

    neg = jnp.full((1, tq), -jnp.inf, F32)
    zero = jnp.zeros((8, tq), F32)
    init = (neg, zero, neg, zero)
    scores(0, sa_ref)

    @pl.when(qi % 2 == 0)
    def _():
        carry = lax.fori_loop(0, qi // 2, pair, init)
        finish(softmax_pv(qi, sa_ref, carry, True))

    @pl.when(qi % 2 == 1)
    def _():
        carry = lax.fori_loop(0, qi // 2, pair, init)
        scores(qi, sb_ref)
        carry = softmax_pv(qi - 1, sa_ref, carry, False)
        finish(softmax_pv(qi, sb_ref, carry, True))


def _diff_attention(seg_b, lam_vecs, norm_g, lam_init, tq=512):
    B, S, _ = seg_b.shape
    tq = min(S, tq)
    kc = tq
    hw = 2 * HEAD_DIM
    return pl.pallas_call(
        functools.partial(_diff_body, tq=tq, kc=kc, lam_init=lam_init),
        grid=(B, N_HEADS, S // tq),
        in_specs=[pl.BlockSpec((1, tq, hw), lambda b, h, i: (b, i, h)),
                  pl.BlockSpec((1, S, hw), lambda b, h, i: (b, 0, N_HEADS + h)),
                  pl.BlockSpec((1, S, hw), lambda b, h, i: (b, 0, 2 * N_HEADS + h)),
                  pl.BlockSpec((4, HEAD_DIM), lambda b, h, i: (0, 0)),
                  pl.BlockSpec((1, hw), lambda b, h, i: (0, 0))],
        out_specs=pl.BlockSpec((1, tq, hw), lambda b, h, i: (b, i, h)),
        out_shape=jax.ShapeDtypeStruct((B, S, DIFF_W), F32),
        scratch_shapes=[pltpu.VMEM((S // kc, hw, kc), BF16), pltpu.VMEM((hw, tq), F32), pltpu.VMEM((hw, tq), F32),
                        pltpu.VMEM((2, kc, tq), F32), pltpu.VMEM((2, kc, tq), F32)],
        compiler_params=_cparams(3),
        name="diff_attention",
    )(seg_b, seg_b, seg_b, lam_vecs, norm_g.reshape(1, hw))


def _hgrn_body(x_ref, xg_ref, w_ref, la_ref, lc_ref, oml_ref, ng_ref, o_ref, st_ref, seg_ref, *, tr, unroll):
    t = pl.program_id(1)
    c = SUB
    W = MIX_W

    @pl.when(t == 0)
    def _():
        st_ref[...] = jnp.zeros_like(st_ref)

    seg_ref[...] = _dot(_rms(x_ref[0], xg_ref[...]).astype(BF16), w_ref[...])

    e_bf = jnp.where(_head_sum_matrix(W), 1.0, 0.0).astype(BF16)
    row = lax.broadcasted_iota(jnp.int32, (c, W), 0)
    head = lax.broadcasted_iota(jnp.int32, (1, W), 1) // HEAD_DIM

    def stack(x):
        return jnp.concatenate([jnp.where(head == h, x, 0.0) for h in range(N_HEADS)], axis=0).astype(BF16)

    def group(j, _):
        base = j * (c * unroll)
        rows = [pl.ds(pl.multiple_of(base + u * c, c), c) for u in range(unroll)]
        us = range(unroll)
        qr = [seg_ref[r, 0:W] for r in rows]
        f = [seg_ref[r, W:2 * W] for r in rows]
        iv = [seg_ref[r, 2 * W:3 * W] for r in rows]
        gr = [seg_ref[r, 3 * W:4 * W] for r in rows]
        q = [x * _sigmoid(x) for x in qr]
        log_sig = [jnp.minimum(x, 0.0) - jnp.log1p(jnp.exp(-jnp.abs(x))) for x in f]
        x1 = la_ref[...]
        x2 = [lc_ref[...] + x for x in log_sig]
        log_f = [jnp.maximum(x1, x) + jnp.log1p(jnp.exp(-jnp.abs(x1 - x))) for x in x2]
        k = [oml_ref[...] * _sigmoid(-x) for x in f]
        b = [_cumsum_rows(x) for x in log_f]
        qe_st = [stack(q[u] * jnp.exp(b[u])) for u in us]
        pmat = []
        for u in us:
            ps = []
            for s in range(c):
                e = jnp.exp(jnp.where(row >= s, b[u] - b[u][s:s + 1], -jnp.inf))
                ps.append((q[u] * e * k[u][s:s + 1]).astype(BF16))
            pmat.append(jnp.concatenate(ps, axis=0))
        a = [_dot(p, e_bf) for p in pmat]
        o_intra = []
        for u in us:
            o = a[u][0:c] * iv[u][0:1]
            for s in range(1, c):
                o = o + a[u][s * c:(s + 1) * c] * iv[u][s:s + 1]
            o_intra.append(o)
        b_last = [x[c - 1:c] for x in b]
        kt_st = [stack(k[u] * jnp.exp(b_last[u] - b[u])) for u in us]
        i_heads = [jnp.concatenate([x[:, h * HEAD_DIM:(h + 1) * HEAD_DIM] for h in range(N_HEADS)], axis=0).astype(BF16)
                   for x in iv]
        upd = [_dot_tn(i_heads[u], kt_st[u]) for u in us]
        dec = [jnp.exp(x) for x in b_last]
        st = st_ref[...]
        o_heads = []
        for u in us:
            o_heads.append(_dot_nt(qe_st[u], st.astype(BF16)))
            st = st * dec[u] + upd[u]
        st_ref[...] = st
        for u in us:
            o = o_intra[u] + jnp.concatenate([o_heads[u][h * c:(h + 1) * c] for h in range(N_HEADS)], axis=1)
            ms = _sum_heads(o * o, e_bf) * (1.0 / HEAD_DIM)
            o_ref[0, rows[u], :] = o * lax.rsqrt(ms + NORM_EPS) * ng_ref[...] * (gr[u] * _sigmoid(gr[u]))
        return 0

    lax.fori_loop(0, tr // (c * unroll), group, 0)


def _hgrn(x, mix_g, w_c, lb, norm_g, tr=256, unroll=16):
    B, S, D = x.shape
    tr = min(S, tr)
    W = MIX_W
    vec = pl.BlockSpec((1, W), lambda b, t: (0, 0))
    lb = lb.reshape(1, W)
    return pl.pallas_call(
        functools.partial(_hgrn_body, tr=tr, unroll=unroll),
        grid=(B, S // tr),
        in_specs=[pl.BlockSpec((1, tr, D), lambda b, t: (b, t, 0)),
                  pl.BlockSpec((1, D), lambda b, t: (0, 0)),
                  pl.BlockSpec((D, SEG_C), lambda b, t: (0, 0)),
                  vec, vec, vec, vec],
        out_specs=pl.BlockSpec((1, tr, W), lambda b, t: (b, t, 0)),
        out_shape=jax.ShapeDtypeStruct((B, S, W), F32),
        scratch_shapes=[pltpu.VMEM((HEAD_DIM, W), F32), pltpu.VMEM((tr, SEG_C), F32)],
        compiler_params=_cparams(2),
        name="hgrn2",
    )(x, mix_g.reshape(1, D), w_c, jnp.log(lb), jnp.log1p(-lb), 1.0 - lb,
      jnp.tile(norm_g.reshape(1, HEAD_DIM), (1, N_HEADS)))


def _rwkv_body(*refs, tr, first_layer):
    if first_layer:
        (x_ref, mu_ref, w0_ref, w2_ref, a0_ref, a2_ref, g2_ref, kk_ref, ka_ref, rk_ref, lg_ref, lb_ref,
         y_ref, vf_out_ref,
         carry_ref, st_ref) = refs
    else:
        (x_ref, vf_ref, mu_ref, w0_ref, w2_ref, a0_ref, a2_ref, g2_ref, kk_ref, ka_ref, rk_ref, lg_ref, lb_ref,
         v0_ref, v1_ref, v2_ref,
         y_ref,
         carry_ref, st_ref) = refs
    t = pl.program_id(1)
    C = CHUNK
    W = MIX_W

    @pl.when(t == 0)
    def _():
        st_ref[...] = jnp.zeros_like(st_ref)
        carry_ref[...] = jnp.zeros_like(carry_ref)

    same_head = _head_sum_matrix(W)
    e_bf = jnp.where(same_head, 1.0, 0.0).astype(BF16)
    head = lax.broadcasted_iota(jnp.int32, (1, W), 1) // HEAD_DIM

    xs = x_ref[0]
    rowi = lax.broadcasted_iota(jnp.int32, xs.shape, 0)
    prev = jnp.where(rowi == 0, carry_ref[0:1, :], pltpu.roll(xs, 1, 0))
    carry_ref[0:1, :] = xs[tr - 1:tr, :]
    xm = xs + (prev - xs) * mu_ref[...]
    r = xm[:, 0:W]
    k = xm[:, W:2 * W]
    v = xm[:, 2 * W:3 * W]
    w_low = xm[:, 3 * W:3 * W + 64]
    a_low = xm[:, 3 * W + 64:3 * W + 128]
    g_low = xm[:, 3 * W + 128:3 * W + 256]
    wlog = -_softplus(-(w0_ref[...] + _dot(jnp.tanh(w_low).astype(BF16), w2_ref[...]))) - 0.5
    a = _sigmoid(a0_ref[...] + _dot(a_low.astype(BF16), a2_ref[...]))
    g = _dot(_sigmoid(g_low).astype(BF16), g2_ref[...])
    kkr = k * kk_ref[...]
    kn = kkr / jnp.maximum(jnp.sqrt(_sum_heads(kkr * kkr, e_bf)), 1e-12)
    k = k * (1.0 + (a - 1.0) * ka_ref[...])
    if first_layer:
        vf_out_ref[0] = v
    else:
        mix = _dot(_dot(v.astype(BF16), v1_ref[...]).astype(BF16), v2_ref[...])
        v = v + (vf_ref[0] - v) * _sigmoid(v0_ref[...] + mix)

    hc = N_HEADS * C
    tt = lax.broadcasted_iota(jnp.int32, (hc, hc), 0) % C
    ss = lax.broadcasted_iota(jnp.int32, (hc, hc), 1) % C
    strict = ss < tt
    incl = ss <= tt
    eye = jnp.where(lax.broadcasted_iota(jnp.int32, (hc, hc), 0) == lax.broadcasted_iota(jnp.int32, (hc, hc), 1),
                    1.0, 0.0)

    def stack(x):
        return jnp.concatenate([jnp.where(head == h, x, 0.0) for h in range(N_HEADS)], axis=0).astype(BF16)

    def fold(x):
        return x[0:C] + x[C:2 * C] + x[2 * C:3 * C] + x[3 * C:4 * C]

    n_ch = tr // C
    chunks = range(n_ch)
    sl = lambda x, c: x[c * C:(c + 1) * C]
    lw_all = -jnp.exp(wlog)
    bb = kn * a
    cs = [_cumsum_rows(sl(lw_all, c)) for c in chunks]
    r_st = [stack(sl(r, c) * jnp.exp(cs[c])) for c in chunks]
    n_st = [stack(sl(kn, c) * jnp.exp(cs[c] - sl(lw_all, c))) for c in chunks]
    kb_st = [jnp.concatenate([stack(sl(k, c) * jnp.exp(-cs[c])), stack(sl(bb, c) * jnp.exp(-cs[c]))], axis=0)
             for c in chunks]
    v_st = [stack(sl(v, c)) for c in chunks]
    gram_n = [_dot_nt(n_st[c], kb_st[c]) for c in chunks]
    gram_r = [_dot_nt(r_st[c], kb_st[c]) for c in chunks]
    l_k = [jnp.where(strict, g[:, :hc], 0.0).astype(BF16) for g in gram_n]
    l_b = [jnp.where(strict, g[:, hc:], 0.0) for g in gram_n]
    m_k = [jnp.where(incl, g[:, :hc], 0.0).astype(BF16) for g in gram_r]
    m_b = [jnp.where(incl, g[:, hc:], 0.0).astype(BF16) for g in gram_r]
    inv_t = [eye - x for x in l_b]
    pw = [_dot(x.astype(BF16), x.astype(BF16)) for x in l_b]
    n = 2
    while n < C:
        inv_t = [t_ + _dot(t_.astype(BF16), p_.astype(BF16)) for t_, p_ in zip(inv_t, pw)]
        n *= 2
        if n < C:
            pw = [_dot(p_.astype(BF16), p_.astype(BF16)) for p_ in pw]
    lkv = [_dot(l_k[c], v_st[c]) for c in chunks]
    mkv = [_dot(m_k[c], v_st[c]) for c in chunks]
    tw = [_dot(inv_t[c].astype(BF16), jnp.concatenate([n_st[c], lkv[c].astype(BF16)], axis=1)) for c in chunks]
    mw = [_dot(m_b[c], tw[c].astype(BF16)) for c in chunks]
    wnr = [jnp.concatenate([tw[c][:, :W], r_st[c].astype(F32) - mw[c][:, :W]], axis=0).astype(BF16) for c in chunks]
    u_a = [fold(tw[c][:, W:]) for c in chunks]
    y_a = [fold(mkv[c] - mw[c][:, W:]) for c in chunks]
    st = st_ref[...]
    ys = []
    for c in chunks:
        x = _dot_nt(wnr[c], st.astype(BF16))
        u = fold(x[:hc]) + u_a[c]
        ys.append(fold(x[hc:]) + y_a[c])
        c_last = cs[c][C - 1:C, :]
        dec = jnp.exp(c_last - cs[c])
        vu = jnp.concatenate([sl(v, c), -u], axis=0).astype(BF16)
        kb_end = jnp.concatenate([sl(k, c) * dec, sl(bb, c) * dec], axis=0).astype(BF16)
        st = st * jnp.exp(c_last) + jnp.where(same_head, _dot_tn(vu, kb_end), 0.0)
    st_ref[...] = st
    y = jnp.concatenate(ys, axis=0)

    mean = _sum_heads(y, e_bf) * (1.0 / HEAD_DIM)
    yc = y - mean
    var = _sum_heads(yc * yc, e_bf) * (1.0 / HEAD_DIM)
    yn = yc * lax.rsqrt(var + RWKV_GN_EPS) * lg_ref[...] + lb_ref[...]
    bonus = _sum_heads(r * k * rk_ref[...], e_bf) * v
    y_ref[0] = (yn + bonus) * g


def _rwkv(seg_d, p, v_first, tr=256):
    B, S, _ = seg_d.shape
    tr = min(S, tr)
    W = MIX_W
    first = v_first is None
    row = lambda a: a.reshape(1, -1)
    full = lambda a: pl.BlockSpec(a.shape, lambda b, t: (0,) * a.ndim)
    tile = lambda w: pl.BlockSpec((1, tr, w), lambda b, t: (b, t, 0))
    params = [row(p['mu']), row(p['w0']), p['w2'].astype(BF16), row(p['a0']), p['a2'].astype(BF16),
              p['g2'].astype(BF16), row(p['k_k']), row(p['k_a']), row(p['r_k']), row(p['lnx_g']), row(p['lnx_b'])]
    args = [seg_d]
    in_specs = [tile(SEG_D)]
    if not first:
        args.append(v_first)
        in_specs.append(tile(W))
        params += [row(p['v0']), p['v1'].astype(BF16), p['v2'].astype(BF16)]
    args += params
    in_specs += [full(a) for a in params]
    n_out = 2 if first else 1
    outs = pl.pallas_call(
        functools.partial(_rwkv_body, tr=tr, first_layer=first),
        grid=(B, S // tr),
        in_specs=in_specs,
        out_specs=[tile(W)] * n_out,
        out_shape=[jax.ShapeDtypeStruct((B, S, W), F32)] * n_out,
        scratch_shapes=[pltpu.VMEM((8, SEG_D), F32), pltpu.VMEM((W, W), F32)],
        compiler_params=_cparams(2),
        name="rwkv7",
    )(*args)
    return (outs[0], outs[1]) if first else (outs[0], v_first)


def _merge_body(o0, l0, o1, l1, o2, l2, yb, yc, yd, x_ref, ng_ref, wg, pa, pb, pc, pd, wo, out_ref):
    x = x_ref[...]
    D = x.shape[-1]
    hn = _rms(x, ng_ref[...]).astype(BF16)
    la, lb, lc = l0[...], l1[...], l2[...]
    m = jnp.maximum(jnp.maximum(la, lb), lc)
    e0, e1, e2 = jnp.exp(la - m), jnp.exp(lb - m), jnp.exp(lc - m)
    inv = 1.0 / (e0 + e1 + e2)
    y_a = (e0 * inv) * o0[...] + (e1 * inv) * o1[...] + (e2 * inv) * o2[...]
    merged = jnp.zeros_like(x)
    for j, (y, p) in enumerate(((y_a, pa), (yb[...], pb), (yc[...], pc), (yd[...], pd))):
        gate = _sigmoid(_dot(hn, wg[:, j * D:(j + 1) * D]))
        merged = merged + gate * _dot(y.astype(BF16), p[...])
    out_ref[...] = x + _dot(merged.astype(BF16), wo[...])


def _merge(dil, y_b, y_c, y_d, x2, norm_g, w_gate, p_a, p_b, p_c, p_d, w_out):
    T, D = x2.shape
    tm = min(T, 512)
    rows = lambda w: pl.BlockSpec((tm, w), lambda i: (i, 0))
    full = lambda a: pl.BlockSpec(a.shape, lambda i: (0, 0))
    flat = lambda a: a.reshape(T, a.shape[-1])
    acts = []
    for o, lse in dil:
        acts += [flat(o), flat(lse)]
    acts += [flat(y_b), flat(y_c), flat(y_d)]
    weights = [norm_g.reshape(1, D), w_gate] + [w.astype(BF16) for w in (p_a, p_b, p_c, p_d, w_out)]
    in_specs = ([rows(MIX_W)] * 6 + [rows(DIFF_W), rows(MIX_W), rows(MIX_W), rows(D)] + [full(w) for w in weights])
    return pl.pallas_call(
        _merge_body,
        grid=(T // tm,),
        in_specs=in_specs,
        out_specs=rows(D),
        out_shape=jax.ShapeDtypeStruct((T, D), F32),
        compiler_params=_cparams(1),
        name="gated_merge",
    )(*acts, x2, *weights)


def _mem_body(x_ref, g_ref, wq_ref, kv_ref, wo_ref, out_ref, *, n_heads):
    x = x_ref[0]
    D = x.shape[-1]
    dm = D // n_heads
    q = _dot(_rms(x, g_ref[...]).astype(BF16), wq_ref[...]) * (dm ** -0.5)
    kv = kv_ref[0].astype(BF16)
    outs = []
    for h in range(n_heads):
        s = _dot_nt(q[:, h * dm:(h + 1) * dm].astype(BF16), kv[:, h * dm:(h + 1) * dm])
        p = jnp.exp(s - jnp.max(s, axis=-1, keepdims=True))
        p = p / jnp.sum(p, axis=-1, keepdims=True)
        outs.append(_dot(p.astype(BF16), kv[:, D + h * dm:D + (h + 1) * dm]))
    o = jnp.concatenate(outs, axis=-1)
    out_ref[0] = x + _dot(o.astype(BF16), wo_ref[...])


def _mem_attention(x, g, w_q, kv, w_o, n_heads=4):
    B, S, D = x.shape
    M = kv.shape[1]
    tm = min(S, 512)
    full = lambda a: pl.BlockSpec(a.shape, lambda b, i: (0, 0))
    wq, wo = w_q.astype(BF16), w_o.astype(BF16)
    g = g.reshape(1, D)
    return pl.pallas_call(
        functools.partial(_mem_body, n_heads=n_heads),
        grid=(B, S // tm),
        in_specs=[pl.BlockSpec((1, tm, D), lambda b, i: (b, i, 0)), full(g), full(wq),
                  pl.BlockSpec((1, M, 2 * D), lambda b, i: (b, 0, 0)), full(wo)],
        out_specs=pl.BlockSpec((1, tm, D), lambda b, i: (b, i, 0)),
        out_shape=jax.ShapeDtypeStruct((B, S, D), F32),
        compiler_params=_cparams(2),
        name="mem_attention",
    )(x, g, wq, kv, wo)


def _ffn_body(x_ref, halo_ref, g_ref, wg_ref, wv_ref, cwg_ref, cwv_ref, cbg_ref, cbv_ref, wo_ref, fg_ref,
              out_ref, hn_ref, hh_ref, acc_ref, *, final_norm):
    i = pl.program_id(1)
    c = pl.program_id(2)
    tm = x_ref.shape[1]

    @pl.when(c == 0)
    def _():
        hn_ref[...] = _rms(x_ref[0], g_ref[...]).astype(BF16)
        hh_ref[...] = _rms(halo_ref[0], g_ref[...]).astype(BF16)
        acc_ref[...] = jnp.zeros_like(acc_ref)

    live = jnp.where(i > 0, 1.0, 0.0)
    row = lax.broadcasted_iota(jnp.int32, (tm, 1), 0)

    def conv(w_ref, cw_ref, cb_ref):
        u = _dot(hn_ref[...], w_ref[...])
        uh = _dot(hh_ref[...], w_ref[...]) * live
        u1 = jnp.where(row == 0, uh[7:8], pltpu.roll(u, 1, 0))
        u2 = jnp.where(row == 0, uh[6:7], jnp.where(row == 1, uh[7:8], pltpu.roll(u, 2, 0)))
        cw = cw_ref[...]
        return cb_ref[...] + u2 * cw[0:1] + u1 * cw[1:2] + u * cw[2:3]

    gate = conv(wg_ref, cwg_ref, cbg_ref)
    val = conv(wv_ref, cwv_ref, cbv_ref)
    act = (gate * _sigmoid(gate) * val).astype(BF16)
    acc_ref[...] += _dot(act, wo_ref[...])

    @pl.when(c == pl.num_programs(2) - 1)
    def _():
        y = x_ref[0] + acc_ref[...]
        out_ref[0] = _rms(y, fg_ref[...]) if final_norm else y


def _ffn(x, g, w_in, conv_w, conv_b, w_out, final_g):
    B, S, D = x.shape
    d_ff = w_out.shape[0]
    tm = min(S, 1024)
    fc = d_ff
    nf = d_ff // fc
    w_in, w_out = w_in.astype(BF16), w_out.astype(BF16)
    conv_b = conv_b.reshape(1, 2 * d_ff)
    vec = pl.BlockSpec((1, D), lambda b, i, c: (0, 0))
    fg = (final_g if final_g is not None else g).reshape(1, D)
    return pl.pallas_call(
        functools.partial(_ffn_body, final_norm=final_g is not None),
        grid=(B, S // tm, nf),
        in_specs=[pl.BlockSpec((1, tm, D), lambda b, i, c: (b, i, 0)),
                  pl.BlockSpec((1, 8, D), lambda b, i, c: (b, jnp.maximum(i * (tm // 8) - 1, 0), 0)),
                  vec,
                  pl.BlockSpec((D, fc), lambda b, i, c: (0, c)),
                  pl.BlockSpec((D, fc), lambda b, i, c: (0, nf + c)),
                  pl.BlockSpec((3, fc), lambda b, i, c: (0, c)),
                  pl.BlockSpec((3, fc), lambda b, i, c: (0, nf + c)),
                  pl.BlockSpec((1, fc), lambda b, i, c: (0, c)),
                  pl.BlockSpec((1, fc), lambda b, i, c: (0, nf + c)),
                  pl.BlockSpec((fc, D), lambda b, i, c: (c, 0)),
                  vec],
        out_specs=pl.BlockSpec((1, tm, D), lambda b, i, c: (b, i, 0)),
        out_shape=jax.ShapeDtypeStruct((B, S, D), F32),
        scratch_shapes=[pltpu.VMEM((tm, D), BF16), pltpu.VMEM((8, D), BF16), pltpu.VMEM((tm, D), F32)],
        compiler_params=_cparams(3),
        name="conv_ffn",
    )(x, x, g.reshape(1, D), w_in, w_in, conv_w, conv_w, conv_b, conv_b, w_out, fg)


def kernel(x, mem, positions, mix_norm_g, w_in, diff_lam, diff_norm_g, hgrn_lb_logits, hgrn_norm_g, rwkv_mu, rwkv_w0, rwkv_w2, rwkv_a0, rwkv_a2, rwkv_g2, rwkv_k_k, rwkv_k_a, rwkv_r_k, rwkv_lnx_g, rwkv_lnx_b, rwkv_v0, rwkv_v1, rwkv_v2, p_a, p_b, p_c, p_d, w_mix_out, mem_q_norm_g, mem_kv_norm_g, w_mem_q, w_mem_kv, w_mem_o, ffn_norm_g, w_ffn_in, ffn_conv_w, ffn_conv_b, w_ffn_out, final_norm_g):
    B, S, D = x.shape
    M = mem.shape[1]
    T = B * S
    depth = w_in.shape[0]
    assert S % (DIL_PATTERNS[-1][1] * DIL_BLOCK) == 0 and S % CHUNK == 0

    half = ROPE_DIMS // 2
    inv_freq = ROPE_THETA ** (-jnp.arange(half, dtype=F32) / half)
    d = jnp.arange(LANES) % HEAD_DIM
    invf_lanes = jnp.where(d < ROPE_DIMS, inv_freq[d % half], 0.0).reshape(1, LANES)
    rope = (positions.reshape(T, 1), invf_lanes)
    lb_all = jnp.cumsum(jax.nn.softmax(hgrn_lb_logits.astype(F32), axis=0), axis=0)
    lb_all = lb_all - lb_all[0:1]
    offs = (0, SEG_A, SEG_A + SEG_B, SEG_A + SEG_B + SEG_C, SEG_A + SEG_B + SEG_C + SEG_D, w_in.shape[2])
    mem2 = mem.reshape(B * M, D)
    qkv_slabs = lambda w: (True,) * (2 * w // LANES) + (False,) * (w // LANES)

    v_first = None
    for l in range(depth):
        lam_init = 0.8 - 0.6 * math.exp(-0.3 * l)
        w_l = w_in[l].astype(BF16)
        x2 = x.reshape(T, D)
        seg = lambda s: w_l[:, offs[s]:offs[s + 1]]
        seg_a = _norm_matmul(x2, mix_norm_g[l], seg(0), 3 * MIX_W, rope, qkv_slabs(MIX_W)).reshape(B, S, SEG_A)
        seg_b = _norm_matmul(x2, mix_norm_g[l], seg(1), SEG_B, rope, qkv_slabs(DIFF_W), BF16).reshape(B, S, SEG_B)
        seg_d = _norm_matmul(x2, mix_norm_g[l], seg(3), SEG_D).reshape(B, S, SEG_D)
        dil = [_dilated_group(seg_a, g, dilation) for g, (_, dilation) in enumerate(DIL_PATTERNS)]
        y_b = _diff_attention(seg_b, diff_lam[l], diff_norm_g[l], lam_init)
        y_c = _hgrn(x, mix_norm_g[l], seg(2), lb_all[l], hgrn_norm_g[l])
        rp = dict(mu=rwkv_mu[l], w0=rwkv_w0[l], w2=rwkv_w2[l], a0=rwkv_a0[l], a2=rwkv_a2[l], g2=rwkv_g2[l],
                  k_k=rwkv_k_k[l], k_a=rwkv_k_a[l], r_k=rwkv_r_k[l], lnx_g=rwkv_lnx_g[l], lnx_b=rwkv_lnx_b[l])
        if l > 0:
            rp.update(v0=rwkv_v0[l - 1], v1=rwkv_v1[l - 1], v2=rwkv_v2[l - 1])
        y_d, v_first = _rwkv(seg_d, rp, v_first)
        x2 = _merge(dil, y_b, y_c, y_d, x2, mix_norm_g[l], seg(4), p_a[l], p_b[l], p_c[l], p_d[l], w_mix_out[l])
        kv = _norm_matmul(mem2, mem_kv_norm_g[l], w_mem_kv[l].astype(BF16), D).reshape(B, M, 2 * D)
        x = _mem_attention(x2.reshape(B, S, D), mem_q_norm_g[l], w_mem_q[l], kv, w_mem_o[l])
        x = _ffn(x, ffn_norm_g[l], w_ffn_in[l], ffn_conv_w[l], ffn_conv_b[l], w_ffn_out[l],
                 final_norm_g if l == depth - 1 else None)
    return x
```

```python
import functools
import math

import jax
import jax.numpy as jnp
from jax import lax
from jax.experimental import pallas as pl
from jax.experimental.pallas import tpu as pltpu

F32 = jnp.float32
BF16 = jnp.bfloat16

NORM_EPS = 1e-5
HEAD_DIM = 64
ROPE_THETA = 500000.0
ROPE_DIMS = HEAD_DIM // 4
DIL_PATTERNS = ((128, 1), (512, 4), (2048, 16))
DIL_BLOCK = 128
N_HEADS = 4
MIX_W = N_HEADS * HEAD_DIM
DIFF_W = 2 * MIX_W
SEG_A = 3 * 3 * MIX_W
SEG_B = 3 * DIFF_W
SEG_C = 4 * MIX_W
SEG_D = 3 * MIX_W + 64 + 64 + 128
RWKV_GN_EPS = 1e-5 * HEAD_DIM
CHUNK = 64
SUB = 16
LANES = 128
VMEM_LIMIT = 56 * 1024 * 1024
LOG2E = math.log2(math.e)


def _cparams(n_axes):
    return pltpu.CompilerParams(dimension_semantics=("arbitrary",) * n_axes,
                                vmem_limit_bytes=VMEM_LIMIT)


def _dot(a, b):
    return jnp.dot(a, b, preferred_element_type=F32)


def _dot_nt(a, b):
    return lax.dot_general(a, b, (((1,), (1,)), ((), ())), preferred_element_type=F32)


def _dot_tn(a, b):
    return lax.dot_general(a, b, (((0,), (0,)), ((), ())), preferred_element_type=F32)


def _sigmoid(x):
    return 1.0 / (1.0 + jnp.exp(-x))


def _softplus(x):
    return jnp.maximum(x, 0.0) + jnp.log1p(jnp.exp(-jnp.abs(x)))


def _rms(x, g):
    ms = jnp.mean(x * x, axis=-1, keepdims=True)
    return x * lax.rsqrt(ms + NORM_EPS) * g


def _head_sum_matrix(width):
    r = lax.broadcasted_iota(jnp.int32, (width, width), 0) // HEAD_DIM
    c = lax.broadcasted_iota(jnp.int32, (width, width), 1) // HEAD_DIM
    return r == c


def _cumsum_rows(x):
    n = x.shape[0]
    row = lax.broadcasted_iota(jnp.int32, x.shape, 0)
    d = 1
    while d < n:
        x = x + jnp.where(row >= d, pltpu.roll(x, d, 0), 0.0)
        d *= 2
    return x


def _sum_heads(x, e_bf):
    hi = x.astype(BF16)
    lo = (x - hi.astype(F32)).astype(BF16)
    return _dot(hi, e_bf) + _dot(lo, e_bf)


def _rope_tables_body(pos_ref, invf_ref, cos_ref, sin_ref):
    half = ROPE_DIMS // 2
    d = lax.broadcasted_iota(jnp.int32, (1, LANES), 1) % HEAD_DIM
    ang = pos_ref[...].astype(F32) * invf_ref[...]
    s = jnp.sin(ang)
    cos_ref[...] = jnp.cos(ang)
    sin_ref[...] = jnp.where(d < half, -s, s)


def _rope_tables(positions, invf_lanes):
    T = positions.shape[0]
    tm = min(T, 2048)
    tab = pl.BlockSpec((tm, LANES), lambda i: (i, 0))
    return pl.pallas_call(
        _rope_tables_body,
        grid=(T // tm,),
        in_specs=[pl.BlockSpec((tm, 1), lambda i: (i, 0)), pl.BlockSpec((1, LANES), lambda i: (0, 0))],
        out_specs=[tab, tab],
        out_shape=[jax.ShapeDtypeStruct((T, LANES), F32)] * 2,
        compiler_params=_cparams(1),
        name="rope_tables",
    )(positions, invf_lanes)


def _norm_matmul_body(*refs, rope_slabs, slab_major):
    if rope_slabs:
        x_ref, g_ref, w_ref, cos_ref, sin_ref, o_ref, hn_ref = refs
    else:
        x_ref, g_ref, w_ref, o_ref, hn_ref = refs
    j = pl.program_id(1)
    half = ROPE_DIMS // 2
    d = lax.broadcasted_iota(jnp.int32, (1, LANES), 1) % HEAD_DIM

    @pl.when(j == 0)
    def _():
        hn_ref[...] = _rms(x_ref[...], g_ref[...]).astype(BF16)

    acc = _dot(hn_ref[...], w_ref[...])
    if not rope_slabs:
        o_ref[...] = acc.astype(o_ref.dtype)
        return
    for c, roped in enumerate(rope_slabs):
        t = acc[:, c * LANES:(c + 1) * LANES]
        if roped:
            partner = jnp.where(d < half, pltpu.roll(t, LANES - half, 1), pltpu.roll(t, half, 1))
            t = t * cos_ref[...] + partner * sin_ref[...]
        if slab_major:
            o_ref[c] = t.astype(o_ref.dtype)
        else:
            o_ref[:, c * LANES:(c + 1) * LANES] = t.astype(o_ref.dtype)


def _norm_matmul(x2, g, w_bf, tn, rope=None, rope_slabs=(), out_dtype=F32, slab_major=False):
    T, D = x2.shape
    N = w_bf.shape[1]
    tm = min(T, 1024)
    assert len(rope_slabs) in (0, tn // LANES)
    if slab_major:
        out_spec = pl.BlockSpec((tn // LANES, tm, LANES), lambda i, j: (j, i, 0))
        out_shape = jax.ShapeDtypeStruct((N // LANES, T, LANES), out_dtype)
    else:
        out_spec = pl.BlockSpec((tm, tn), lambda i, j: (i, j))
        out_shape = jax.ShapeDtypeStruct((T, N), out_dtype)
    in_specs = [pl.BlockSpec((tm, D), lambda i, j: (i, 0)),
                pl.BlockSpec((1, D), lambda i, j: (0, 0)),
                pl.BlockSpec((D, tn), lambda i, j: (0, j))]
    args = [x2, g.reshape(1, D), w_bf]
    scratch = [pltpu.VMEM((tm, D), BF16)]
    if rope_slabs:
        in_specs += [pl.BlockSpec((tm, LANES), lambda i, j: (i, 0))] * 2
        args += list(rope)
    return pl.pallas_call(
        functools.partial(_norm_matmul_body, rope_slabs=tuple(rope_slabs), slab_major=slab_major),
        grid=(T // tm, N // tn),
        in_specs=in_specs,
        out_specs=out_spec,
        out_shape=out_shape,
        scratch_shapes=scratch,
        compiler_params=_cparams(2),
        name="norm_matmul_rope" if rope_slabs else "norm_matmul",
    )(*args)


def _dilated_body(q_ref, kc_ref, kp_ref, vc_ref, vp_ref, o_ref, lse_ref, *, dilation, nbs, width):
    n = pl.program_id(2)
    blk = DIL_BLOCK
    unit = blk * dilation
    n_lane_heads = LANES // HEAD_DIM
    lane_head = lax.broadcasted_iota(jnp.int32, (1, LANES), 1) // HEAD_DIM
    qi = lax.broadcasted_iota(jnp.int32, (blk, blk), 0)
    kj = lax.broadcasted_iota(jnp.int32, (blk, blk), 1)
    cur_ok = kj <= qi
    first_prev_ok = (kj - qi) >= jnp.where(n > 0, 0, 2 * blk)
    later_prev_ok = kj >= qi
    ones = jnp.ones((blk, LANES), BF16)

    def rows_of(j, r):
        start = j * unit + r
        return pl.ds(start, blk) if dilation == 1 else pl.ds(start, blk, stride=dilation)

    def process(units):
        loaded = []
        for j, r in units:
            rows = rows_of(j, r)
            prows = rows_of(max(j - 1, 0), r)
            kpr, vpr = (kp_ref, vp_ref) if j == 0 else (kc_ref, vc_ref)
            loaded.append((rows, first_prev_ok if j == 0 else later_prev_ok,
                           q_ref[0, rows, :] * (HEAD_DIM ** -0.5),
                           kc_ref[0, rows, :].astype(BF16), kpr[0, prows, :].astype(BF16),
                           jnp.concatenate([vc_ref[0, rows, :].astype(BF16), ones], axis=1),
                           jnp.concatenate([vpr[0, prows, :].astype(BF16), ones], axis=1)))
        heads = [(u, h) for u in range(len(units)) for h in range(n_lane_heads)]
        scores = []
        for u, h in heads:
            _, prev_ok, q, kc, kp, _, _ = loaded[u]
            qh = jnp.where(lane_head == h, q, 0.0).astype(BF16)
            scores.append((jnp.where(cur_ok, _dot_nt(qh, kc), -jnp.inf), jnp.where(prev_ok, _dot_nt(qh, kp), -jnp.inf)))
        probs = []
        for sc, sp in scores:
            m = jnp.max(jnp.maximum(sc, sp), axis=-1, keepdims=True)
            probs.append((m, jnp.exp(sc - m).astype(BF16), jnp.exp(sp - m).astype(BF16)))
        outs = []
        for (u, h), (m, pc, pp) in zip(heads, probs):
            ext = _dot(pc, loaded[u][5]) + _dot(pp, loaded[u][6])
            den = ext[:, LANES:]
            outs.append((ext[:, :LANES] / den, m + jnp.log(den)))
        for u in range(len(units)):
            o_acc = jnp.zeros((blk, LANES), F32)
            lse_acc = jnp.zeros((blk, LANES), F32)
            for h in range(n_lane_heads):
                oh, lse = outs[u * n_lane_heads + h]
                o_acc = jnp.where(lane_head == h, oh, o_acc)
                lse_acc = jnp.where(lane_head == h, lse, lse_acc)
            o_ref[0, loaded[u][0], :] = o_acc
            lse_ref[0, loaded[u][0], :] = lse_acc

    if dilation == 1:
        for j in range(0, nbs, width):
            process([(j + w, 0) for w in range(width)])
    else:
        step = dilation // width
        for j in range(nbs):
            def body(r, _, j=j):
                process([(j, r + w * step) for w in range(width)])
                return 0
            lax.fori_loop(0, step, body, 0)


def _dilated_group(seg_a, B, g, dilation, width=4):
    _, T, _ = seg_a.shape
    S = T // B
    unit = DIL_BLOCK * dilation
    nbs = max(1, 512 // unit)
    tr = unit * nbs
    slab = 2 * (g * 3)

    def cur(which):
        return pl.BlockSpec((1, tr, LANES), lambda b, p, n: (slab + 2 * which + p, b * (S // tr) + n, 0))

    def prev(which):
        return pl.BlockSpec((1, unit, LANES),
                            lambda b, p, n: (slab + 2 * which + p, b * (S // unit) + jnp.maximum(n * nbs - 1, 0), 0))

    out_spec = pl.BlockSpec((1, tr, LANES), lambda b, p, n: (p, b * (S // tr) + n, 0))
    return pl.pallas_call(
        functools.partial(_dilated_body, dilation=dilation, nbs=nbs, width=width),
        grid=(B, MIX_W // LANES, S // tr),
        in_specs=[cur(0), cur(1), prev(1), cur(2), prev(2)],
        out_specs=[out_spec, out_spec],
        out_shape=[jax.ShapeDtypeStruct((MIX_W // LANES, T, LANES), F32)] * 2,
        compiler_params=_cparams(3),
        name="dilated_attention",
    )(seg_a, seg_a, seg_a, seg_a, seg_a)


def _diff_body(q_ref, k_ref, v_ref, lam_ref, g_ref, o_ref, vt_ref, a1_ref, a2_ref, sa_ref, sb_ref, *, tq, kc, lam_init):
    qi = pl.program_id(2)
    n_kv = vt_ref.shape[0]

    @pl.when(qi == 0)
    def _():
        for c in range(n_kv):
            vt_ref[c] = v_ref[0, c * kc:(c + 1) * kc, :].astype(F32).T.astype(BF16)

    q = q_ref[0].astype(F32) * (HEAD_DIM ** -0.5 * LOG2E)
    lane = lax.broadcasted_iota(jnp.int32, (1, 2 * HEAD_DIM), 1)
    q1 = jnp.where(lane < HEAD_DIM, q, 0.0).astype(BF16)
    q2 = jnp.where(lane >= HEAD_DIM, q, 0.0).astype(BF16)
    key = lax.broadcasted_iota(jnp.int32, (kc, tq), 0)
    qry = lax.broadcasted_iota(jnp.int32, (kc, tq), 1)
    a1_ref[...] = jnp.zeros_like(a1_ref)
    a2_ref[...] = jnp.zeros_like(a2_ref)

    def scores(kb, s_ref):
        k = k_ref[0, pl.ds(pl.multiple_of(kb * kc, kc), kc), :]
        s_ref[0] = _dot_nt(k, q1)
        s_ref[1] = _dot_nt(k, q2)

    def softmax_pv(kb, s_ref, carry, masked):
        m1, l1, m2, l2 = carry
        vt = vt_ref[kb]
        s1 = s_ref[0]
        s2 = s_ref[1]
        if masked:
            s1 = jnp.where(key <= qry, s1, -jnp.inf)
            s2 = jnp.where(key <= qry, s2, -jnp.inf)
        n1 = jnp.maximum(m1, jnp.max(s1, axis=0, keepdims=True))
        n2 = jnp.maximum(m2, jnp.max(s2, axis=0, keepdims=True))
        p1 = jnp.exp2(s1 - n1)
        p2 = jnp.exp2(s2 - n2)
        al1 = jnp.exp2(m1 - n1)
        al2 = jnp.exp2(m2 - n2)
        pv1 = _dot(vt, p1.astype(BF16))
        pv2 = _dot(vt, p2.astype(BF16))
        l1 = l1 * al1 + jnp.sum(p1.reshape(kc // 8, 8, tq), axis=0)
        l2 = l2 * al2 + jnp.sum(p2.reshape(kc // 8, 8, tq), axis=0)
        a1_ref[...] = a1_ref[...] * al1 + pv1
        a2_ref[...] = a2_ref[...] * al2 + pv2
        return n1, l1, n2, l2

    def pair(i, carry):
        scores(2 * i + 1, sb_ref)
        carry = softmax_pv(2 * i, sa_ref, carry, False)
        scores(2 * i + 2, sa_ref)
        return softmax_pv(2 * i + 1, sb_ref, carry, False)

    def finish(carry):
        _, l1, _, l2 = carry
        lv = lam_ref[...]
        lam = (jnp.exp(jnp.sum(lv[0:1] * lv[1:2], axis=-1, keepdims=True))
               - jnp.exp(jnp.sum(lv[2:3] * lv[3:4], axis=-1, keepdims=True)) + lam_init)
        o_t = (a1_ref[...] / jnp.sum(l1, axis=0, keepdims=True)
               - lam * (a2_ref[...] / jnp.sum(l2, axis=0, keepdims=True)))
        o_ref[0] = _rms(o_t.T, g_ref[...]) * (1.0 - lam_init)

    neg = jnp.full((1, tq), -jnp.inf, F32)
    zero = jnp.zeros((8, tq), F32)
    init = (neg, zero, neg, zero)
    scores(0, sa_ref)

    @pl.when(qi % 2 == 0)
    def _():
        carry = lax.fori_loop(0, qi // 2, pair, init)
        finish(softmax_pv(qi, sa_ref, carry, True))

    @pl.when(qi % 2 == 1)
    def _():
        carry = lax.fori_loop(0, qi // 2, pair, init)
        scores(qi, sb_ref)
        carry = softmax_pv(qi - 1, sa_ref, carry, False)
        finish(softmax_pv(qi, sb_ref, carry, True))


def _diff_attention(seg_b, lam_vecs, norm_g, lam_init, tq=512):
    B, S, _ = seg_b.shape
    tq = min(S, tq)
    kc = tq
    hw = 2 * HEAD_DIM
    return pl.pallas_call(
        functools.partial(_diff_body, tq=tq, kc=kc, lam_init=lam_init),
        grid=(B, N_HEADS, S // tq),
        in_specs=[pl.BlockSpec((1, tq, hw), lambda b, h, i: (b, i, h)),
                  pl.BlockSpec((1, S, hw), lambda b, h, i: (b, 0, N_HEADS + h)),
                  pl.BlockSpec((1, S, hw), lambda b, h, i: (b, 0, 2 * N_HEADS + h)),
                  pl.BlockSpec((4, HEAD_DIM), lambda b, h, i: (0, 0)),
                  pl.BlockSpec((1, hw), lambda b, h, i: (0, 0))],
        out_specs=pl.BlockSpec((1, tq, hw), lambda b, h, i: (b, i, h)),
        out_shape=jax.ShapeDtypeStruct((B, S, DIFF_W), F32),
        scratch_shapes=[pltpu.VMEM((S // kc, hw, kc), BF16), pltpu.VMEM((hw, tq), F32), pltpu.VMEM((hw, tq), F32),
                        pltpu.VMEM((2, kc, tq), F32), pltpu.VMEM((2, kc, tq), F32)],
        compiler_params=_cparams(3),
        name="diff_attention",
    )(seg_b, seg_b, seg_b, lam_vecs, norm_g.reshape(1, hw))


def _hgrn_body(x_ref, xg_ref, w_ref, la_ref, lc_ref, oml_ref, ng_ref, o_ref, st_ref, seg_ref, *, tr, unroll):
    t = pl.program_id(1)
    c = SUB
    W = MIX_W

    @pl.when(t == 0)
    def _():
        st_ref[...] = jnp.zeros_like(st_ref)

    seg_ref[...] = _dot(_rms(x_ref[0], xg_ref[...]).astype(BF16), w_ref[...])

    e_bf = jnp.where(_head_sum_matrix(W), 1.0, 0.0).astype(BF16)
    row = lax.broadcasted_iota(jnp.int32, (c, W), 0)
    head = lax.broadcasted_iota(jnp.int32, (1, W), 1) // HEAD_DIM

    def stack(x):
        return jnp.concatenate([jnp.where(head == h, x, 0.0) for h in range(N_HEADS)], axis=0).astype(BF16)

    def group(j, _):
        base = j * (c * unroll)
        rows = [pl.ds(pl.multiple_of(base + u * c, c), c) for u in range(unroll)]
        us = range(unroll)
        qr = [seg_ref[r, 0:W] for r in rows]
        f = [seg_ref[r, W:2 * W] for r in rows]
        iv = [seg_ref[r, 2 * W:3 * W] for r in rows]
        gr = [seg_ref[r, 3 * W:4 * W] for r in rows]
        q = [x * _sigmoid(x) for x in qr]
        log_sig = [jnp.minimum(x, 0.0) - jnp.log1p(jnp.exp(-jnp.abs(x))) for x in f]
        x1 = la_ref[...]
        x2 = [lc_ref[...] + x for x in log_sig]
        log_f = [jnp.maximum(x1, x) + jnp.log1p(jnp.exp(-jnp.abs(x1 - x))) for x in x2]
        k = [oml_ref[...] * _sigmoid(-x) for x in f]
        b = [_cumsum_rows(x) for x in log_f]
        qe_st = [stack(q[u] * jnp.exp(b[u])) for u in us]
        pmat = []
        for u in us:
            ps = []
            for s in range(c):
                e = jnp.exp(jnp.where(row >= s, b[u] - b[u][s:s + 1], -jnp.inf))
                ps.append((q[u] * e * k[u][s:s + 1]).astype(BF16))
            pmat.append(jnp.concatenate(ps, axis=0))
        a = [_dot(p, e_bf) for p in pmat]
        o_intra = []
        for u in us:
            o = a[u][0:c] * iv[u][0:1]
            for s in range(1, c):
                o = o + a[u][s * c:(s + 1) * c] * iv[u][s:s + 1]
            o_intra.append(o)
        b_last = [x[c - 1:c] for x in b]
        kt_st = [stack(k[u] * jnp.exp(b_last[u] - b[u])) for u in us]
        i_heads = [jnp.concatenate([x[:, h * HEAD_DIM:(h + 1) * HEAD_DIM] for h in range(N_HEADS)], axis=0).astype(BF16)
                   for x in iv]
        upd = [_dot_tn(i_heads[u], kt_st[u]) for u in us]
        dec = [jnp.exp(x) for x in b_last]
        st = st_ref[...]
        o_heads = []
        for u in us:
            o_heads.append(_dot_nt(qe_st[u], st.astype(BF16)))
            st = st * dec[u] + upd[u]
        st_ref[...] = st
        for u in us:
            o = o_intra[u] + jnp.concatenate([o_heads[u][h * c:(h + 1) * c] for h in range(N_HEADS)], axis=1)
            ms = _sum_heads(o * o, e_bf) * (1.0 / HEAD_DIM)
            o_ref[0, rows[u], :] = o * lax.rsqrt(ms + NORM_EPS) * ng_ref[...] * (gr[u] * _sigmoid(gr[u]))
        return 0

    lax.fori_loop(0, tr // (c * unroll), group, 0)


def _hgrn(x, mix_g, w_c, lb, norm_g, tr=256, unroll=16):
    B, S, D = x.shape
    tr = min(S, tr)
    W = MIX_W
    vec = pl.BlockSpec((1, W), lambda b, t: (0, 0))
    lb = lb.reshape(1, W)
    return pl.pallas_call(
        functools.partial(_hgrn_body, tr=tr, unroll=unroll),
        grid=(B, S // tr),
        in_specs=[pl.BlockSpec((1, tr, D), lambda b, t: (b, t, 0)),
                  pl.BlockSpec((1, D), lambda b, t: (0, 0)),
                  pl.BlockSpec((D, SEG_C), lambda b, t: (0, 0)),
                  vec, vec, vec, vec],
        out_specs=pl.BlockSpec((1, tr, W), lambda b, t: (b, t, 0)),
        out_shape=jax.ShapeDtypeStruct((B, S, W), F32),
        scratch_shapes=[pltpu.VMEM((HEAD_DIM, W), F32), pltpu.VMEM((tr, SEG_C), F32)],
        compiler_params=_cparams(2),
        name="hgrn2",
    )(x, mix_g.reshape(1, D), w_c, jnp.log(lb), jnp.log1p(-lb), 1.0 - lb,
      jnp.tile(norm_g.reshape(1, HEAD_DIM), (1, N_HEADS)))


def _rwkv_body(*refs, tr, first_layer):
    if first_layer:
        (x_ref, mu_ref, w0_ref, w2_ref, a0_ref, a2_ref, g2_ref, kk_ref, ka_ref, rk_ref, lg_ref, lb_ref,
         y_ref, vf_out_ref,
         carry_ref, st_ref) = refs
    else:
        (x_ref, vf_ref, mu_ref, w0_ref, w2_ref, a0_ref, a2_ref, g2_ref, kk_ref, ka_ref, rk_ref, lg_ref, lb_ref,
         v0_ref, v1_ref, v2_ref,
         y_ref,
         carry_ref, st_ref) = refs
    t = pl.program_id(1)
    C = CHUNK
    W = MIX_W

    @pl.when(t == 0)
    def _():
        st_ref[...] = jnp.zeros_like(st_ref)
        carry_ref[...] = jnp.zeros_like(carry_ref)

    same_head = _head_sum_matrix(W)
    e_bf = jnp.where(same_head, 1.0, 0.0).astype(BF16)
    head = lax.broadcasted_iota(jnp.int32, (1, W), 1) // HEAD_DIM

    xs = x_ref[0]
    rowi = lax.broadcasted_iota(jnp.int32, xs.shape, 0)
    prev = jnp.where(rowi == 0, carry_ref[0:1, :], pltpu.roll(xs, 1, 0))
    carry_ref[0:1, :] = xs[tr - 1:tr, :]
    xm = xs + (prev - xs) * mu_ref[...]
    r = xm[:, 0:W]
    k = xm[:, W:2 * W]
    v = xm[:, 2 * W:3 * W]
    w_low = xm[:, 3 * W:3 * W + 64]
    a_low = xm[:, 3 * W + 64:3 * W + 128]
    g_low = xm[:, 3 * W + 128:3 * W + 256]
    wlog = -_softplus(-(w0_ref[...] + _dot(jnp.tanh(w_low).astype(BF16), w2_ref[...]))) - 0.5
    a = _sigmoid(a0_ref[...] + _dot(a_low.astype(BF16), a2_ref[...]))
    g = _dot(_sigmoid(g_low).astype(BF16), g2_ref[...])
    kkr = k * kk_ref[...]
    kn = kkr / jnp.maximum(jnp.sqrt(_sum_heads(kkr * kkr, e_bf)), 1e-12)
    k = k * (1.0 + (a - 1.0) * ka_ref[...])
    if first_layer:
        vf_out_ref[0] = v
    else:
        mix = _dot(_dot(v.astype(BF16), v1_ref[...]).astype(BF16), v2_ref[...])
        v = v + (vf_ref[0] - v) * _sigmoid(v0_ref[...] + mix)

    hc = N_HEADS * C
    tt = lax.broadcasted_iota(jnp.int32, (hc, hc), 0) % C
    ss = lax.broadcasted_iota(jnp.int32, (hc, hc), 1) % C
    strict = ss < tt
    incl = ss <= tt
    eye = jnp.where(lax.broadcasted_iota(jnp.int32, (hc, hc), 0) == lax.broadcasted_iota(jnp.int32, (hc, hc), 1),
                    1.0, 0.0)

    def stack(x):
        return jnp.concatenate([jnp.where(head == h, x, 0.0) for h in range(N_HEADS)], axis=0).astype(BF16)

    def fold(x):
        return x[0:C] + x[C:2 * C] + x[2 * C:3 * C] + x[3 * C:4 * C]

    n_ch = tr // C
    chunks = range(n_ch)
    sl = lambda x, c: x[c * C:(c + 1) * C]
    lw_all = -jnp.exp(wlog)
    bb = kn * a
    cs = [_cumsum_rows(sl(lw_all, c)) for c in chunks]
    r_st = [stack(sl(r, c) * jnp.exp(cs[c])) for c in chunks]
    n_st = [stack(sl(kn, c) * jnp.exp(cs[c] - sl(lw_all, c))) for c in chunks]
    kb_st = [jnp.concatenate([stack(sl(k, c) * jnp.exp(-cs[c])), stack(sl(bb, c) * jnp.exp(-cs[c]))], axis=0)
             for c in chunks]
    v_st = [stack(sl(v, c)) for c in chunks]
    gram_n = [_dot_nt(n_st[c], kb_st[c]) for c in chunks]
    gram_r = [_dot_nt(r_st[c], kb_st[c]) for c in chunks]
    l_k = [jnp.where(strict, g[:, :hc], 0.0).astype(BF16) for g in gram_n]
    l_b = [jnp.where(strict, g[:, hc:], 0.0) for g in gram_n]
    m_k = [jnp.where(incl, g[:, :hc], 0.0).astype(BF16) for g in gram_r]
    m_b = [jnp.where(incl, g[:, hc:], 0.0).astype(BF16) for g in gram_r]
    inv_t = [eye - x for x in l_b]
    pw = [_dot(x.astype(BF16), x.astype(BF16)) for x in l_b]
    n = 2
    while n < C:
        inv_t = [t_ + _dot(t_.astype(BF16), p_.astype(BF16)) for t_, p_ in zip(inv_t, pw)]
        n *= 2
        if n < C:
            pw = [_dot(p_.astype(BF16), p_.astype(BF16)) for p_ in pw]
    lkv = [_dot(l_k[c], v_st[c]) for c in chunks]
    mkv = [_dot(m_k[c], v_st[c]) for c in chunks]
    tw = [_dot(inv_t[c].astype(BF16), jnp.concatenate([n_st[c], lkv[c].astype(BF16)], axis=1)) for c in chunks]
    mw = [_dot(m_b[c], tw[c].astype(BF16)) for c in chunks]
    wnr = [jnp.concatenate([tw[c][:, :W], r_st[c].astype(F32) - mw[c][:, :W]], axis=0).astype(BF16) for c in chunks]
    u_a = [fold(tw[c][:, W:]) for c in chunks]
    y_a = [fold(mkv[c] - mw[c][:, W:]) for c in chunks]
    st = st_ref[...]
    ys = []
    for c in chunks:
        x = _dot_nt(wnr[c], st.astype(BF16))
        u = fold(x[:hc]) + u_a[c]
        ys.append(fold(x[hc:]) + y_a[c])
        c_last = cs[c][C - 1:C, :]
        dec = jnp.exp(c_last - cs[c])
        vu = jnp.concatenate([sl(v, c), -u], axis=0).astype(BF16)
        kb_end = jnp.concatenate([sl(k, c) * dec, sl(bb, c) * dec], axis=0).astype(BF16)
        st = st * jnp.exp(c_last) + jnp.where(same_head, _dot_tn(vu, kb_end), 0.0)
    st_ref[...] = st
    y = jnp.concatenate(ys, axis=0)

    mean = _sum_heads(y, e_bf) * (1.0 / HEAD_DIM)
    yc = y - mean
    var = _sum_heads(yc * yc, e_bf) * (1.0 / HEAD_DIM)
    yn = yc * lax.rsqrt(var + RWKV_GN_EPS) * lg_ref[...] + lb_ref[...]
    bonus = _sum_heads(r * k * rk_ref[...], e_bf) * v
    y_ref[0] = (yn + bonus) * g


def _rwkv(seg_d, p, v_first, tr=256):
    B, S, _ = seg_d.shape
    tr = min(S, tr)
    W = MIX_W
    first = v_first is None
    row = lambda a: a.reshape(1, -1)
    full = lambda a: pl.BlockSpec(a.shape, lambda b, t: (0,) * a.ndim)
    tile = lambda w: pl.BlockSpec((1, tr, w), lambda b, t: (b, t, 0))
    params = [row(p['mu']), row(p['w0']), p['w2'].astype(BF16), row(p['a0']), p['a2'].astype(BF16),
              p['g2'].astype(BF16), row(p['k_k']), row(p['k_a']), row(p['r_k']), row(p['lnx_g']), row(p['lnx_b'])]
    args = [seg_d]
    in_specs = [tile(SEG_D)]
    if not first:
        args.append(v_first)
        in_specs.append(tile(W))
        params += [row(p['v0']), p['v1'].astype(BF16), p['v2'].astype(BF16)]
    args += params
    in_specs += [full(a) for a in params]
    n_out = 2 if first else 1
    outs = pl.pallas_call(
        functools.partial(_rwkv_body, tr=tr, first_layer=first),
        grid=(B, S // tr),
        in_specs=in_specs,
        out_specs=[tile(W)] * n_out,
        out_shape=[jax.ShapeDtypeStruct((B, S, W), F32)] * n_out,
        scratch_shapes=[pltpu.VMEM((8, SEG_D), F32), pltpu.VMEM((W, W), F32)],
        compiler_params=_cparams(2),
        name="rwkv7",
    )(*args)
    return (outs[0], outs[1]) if first else (outs[0], v_first)


def _merge_body(o0, l0, o1, l1, o2, l2, yb, yc, yd, x_ref, ng_ref, wg, pa, pb, pc, pd, wo, out_ref):
    x = x_ref[...]
    D = x.shape[-1]
    hn = _rms(x, ng_ref[...]).astype(BF16)
    wide = lambda ref: jnp.concatenate([ref[0], ref[1]], axis=1)
    la, lb, lc = wide(l0), wide(l1), wide(l2)
    m = jnp.maximum(jnp.maximum(la, lb), lc)
    e0, e1, e2 = jnp.exp(la - m), jnp.exp(lb - m), jnp.exp(lc - m)
    inv = 1.0 / (e0 + e1 + e2)
    y_a = (e0 * inv) * wide(o0) + (e1 * inv) * wide(o1) + (e2 * inv) * wide(o2)
    merged = jnp.zeros_like(x)
    for j, (y, p) in enumerate(((y_a, pa), (yb[...], pb), (yc[...], pc), (yd[...], pd))):
        gate = _sigmoid(_dot(hn, wg[:, j * D:(j + 1) * D]))
        merged = merged + gate * _dot(y.astype(BF16), p[...])
    out_ref[...] = x + _dot(merged.astype(BF16), wo[...])


def _merge(dil, y_b, y_c, y_d, x2, norm_g, w_gate, p_a, p_b, p_c, p_d, w_out):
    T, D = x2.shape
    tm = min(T, 512)
    rows = lambda w: pl.BlockSpec((tm, w), lambda i: (i, 0))
    full = lambda a: pl.BlockSpec(a.shape, lambda i: (0, 0))
    flat = lambda a: a.reshape(T, a.shape[-1])
    acts = []
    for o, lse in dil:
        acts += [o, lse]
    acts += [flat(y_b), flat(y_c), flat(y_d)]
    weights = [norm_g.reshape(1, D), w_gate] + [w.astype(BF16) for w in (p_a, p_b, p_c, p_d, w_out)]
    slabs = pl.BlockSpec((MIX_W // LANES, tm, LANES), lambda i: (0, i, 0))
    in_specs = ([slabs] * 6 + [rows(DIFF_W), rows(MIX_W), rows(MIX_W), rows(D)] + [full(w) for w in weights])
    return pl.pallas_call(
        _merge_body,
        grid=(T // tm,),
        in_specs=in_specs,
        out_specs=rows(D),
        out_shape=jax.ShapeDtypeStruct((T, D), F32),
        compiler_params=_cparams(1),
        name="gated_merge",
    )(*acts, x2, *weights)


def _mem_body(x_ref, g_ref, wq_ref, kv_ref, wo_ref, out_ref, *, n_heads):
    x = x_ref[0]
    D = x.shape[-1]
    dm = D // n_heads
    q = _dot(_rms(x, g_ref[...]).astype(BF16), wq_ref[...]) * (dm ** -0.5)
    kv = kv_ref[0].astype(BF16)
    outs = []
    for h in range(n_heads):
        s = _dot_nt(q[:, h * dm:(h + 1) * dm].astype(BF16), kv[:, h * dm:(h + 1) * dm])
        p = jnp.exp(s - jnp.max(s, axis=-1, keepdims=True))
        p = p / jnp.sum(p, axis=-1, keepdims=True)
        outs.append(_dot(p.astype(BF16), kv[:, D + h * dm:D + (h + 1) * dm]))
    o = jnp.concatenate(outs, axis=-1)
    out_ref[0] = x + _dot(o.astype(BF16), wo_ref[...])


def _mem_attention(x, g, w_q, kv, w_o, n_heads=4):
    B, S, D = x.shape
    M = kv.shape[1]
    tm = min(S, 512)
    full = lambda a: pl.BlockSpec(a.shape, lambda b, i: (0, 0))
    wq, wo = w_q.astype(BF16), w_o.astype(BF16)
    g = g.reshape(1, D)
    return pl.pallas_call(
        functools.partial(_mem_body, n_heads=n_heads),
        grid=(B, S // tm),
        in_specs=[pl.BlockSpec((1, tm, D), lambda b, i: (b, i, 0)), full(g), full(wq),
                  pl.BlockSpec((1, M, 2 * D), lambda b, i: (b, 0, 0)), full(wo)],
        out_specs=pl.BlockSpec((1, tm, D), lambda b, i: (b, i, 0)),
        out_shape=jax.ShapeDtypeStruct((B, S, D), F32),
        compiler_params=_cparams(2),
        name="mem_attention",
    )(x, g, wq, kv, wo)


def _ffn_body(x_ref, halo_ref, g_ref, wg_ref, wv_ref, cwg_ref, cwv_ref, cbg_ref, cbv_ref, wo_ref, fg_ref,
              out_ref, hn_ref, hh_ref, acc_ref, *, final_norm):
    i = pl.program_id(1)
    c = pl.program_id(2)
    tm = x_ref.shape[1]

    @pl.when(c == 0)
    def _():
        hn_ref[...] = _rms(x_ref[0], g_ref[...]).astype(BF16)
        hh_ref[...] = _rms(halo_ref[0], g_ref[...]).astype(BF16)
        acc_ref[...] = jnp.zeros_like(acc_ref)

    live = jnp.where(i > 0, 1.0, 0.0)
    row = lax.broadcasted_iota(jnp.int32, (tm, 1), 0)

    def conv(w_ref, cw_ref, cb_ref):
        u = _dot(hn_ref[...], w_ref[...])
        uh = _dot(hh_ref[...], w_ref[...]) * live
        u1 = jnp.where(row == 0, uh[7:8], pltpu.roll(u, 1, 0))
        u2 = jnp.where(row == 0, uh[6:7], jnp.where(row == 1, uh[7:8], pltpu.roll(u, 2, 0)))
        cw = cw_ref[...]
        return cb_ref[...] + u2 * cw[0:1] + u1 * cw[1:2] + u * cw[2:3]

    gate = conv(wg_ref, cwg_ref, cbg_ref)
    val = conv(wv_ref, cwv_ref, cbv_ref)
    act = (gate * _sigmoid(gate) * val).astype(BF16)
    acc_ref[...] += _dot(act, wo_ref[...])

    @pl.when(c == pl.num_programs(2) - 1)
    def _():
        y = x_ref[0] + acc_ref[...]
        out_ref[0] = _rms(y, fg_ref[...]) if final_norm else y


def _ffn(x, g, w_in, conv_w, conv_b, w_out, final_g):
    B, S, D = x.shape
    d_ff = w_out.shape[0]
    tm = min(S, 1024)
    fc = d_ff
    nf = d_ff // fc
    w_in, w_out = w_in.astype(BF16), w_out.astype(BF16)
    conv_b = conv_b.reshape(1, 2 * d_ff)
    vec = pl.BlockSpec((1, D), lambda b, i, c: (0, 0))
    fg = (final_g if final_g is not None else g).reshape(1, D)
    return pl.pallas_call(
        functools.partial(_ffn_body, final_norm=final_g is not None),
        grid=(B, S // tm, nf),
        in_specs=[pl.BlockSpec((1, tm, D), lambda b, i, c: (b, i, 0)),
                  pl.BlockSpec((1, 8, D), lambda b, i, c: (b, jnp.maximum(i * (tm // 8) - 1, 0), 0)),
                  vec,
                  pl.BlockSpec((D, fc), lambda b, i, c: (0, c)),
                  pl.BlockSpec((D, fc), lambda b, i, c: (0, nf + c)),
                  pl.BlockSpec((3, fc), lambda b, i, c: (0, c)),
                  pl.BlockSpec((3, fc), lambda b, i, c: (0, nf + c)),
                  pl.BlockSpec((1, fc), lambda b, i, c: (0, c)),
                  pl.BlockSpec((1, fc), lambda b, i, c: (0, nf + c)),
                  pl.BlockSpec((fc, D), lambda b, i, c: (c, 0)),
                  vec],
        out_specs=pl.BlockSpec((1, tm, D), lambda b, i, c: (b, i, 0)),
        out_shape=jax.ShapeDtypeStruct((B, S, D), F32),
        scratch_shapes=[pltpu.VMEM((tm, D), BF16), pltpu.VMEM((8, D), BF16), pltpu.VMEM((tm, D), F32)],
        compiler_params=_cparams(3),
        name="conv_ffn",
    )(x, x, g.reshape(1, D), w_in, w_in, conv_w, conv_w, conv_b, conv_b, w_out, fg)


def kernel(x, mem, positions, mix_norm_g, w_in, diff_lam, diff_norm_g, hgrn_lb_logits, hgrn_norm_g, rwkv_mu, rwkv_w0, rwkv_w2, rwkv_a0, rwkv_a2, rwkv_g2, rwkv_k_k, rwkv_k_a, rwkv_r_k, rwkv_lnx_g, rwkv_lnx_b, rwkv_v0, rwkv_v1, rwkv_v2, p_a, p_b, p_c, p_d, w_mix_out, mem_q_norm_g, mem_kv_norm_g, w_mem_q, w_mem_kv, w_mem_o, ffn_norm_g, w_ffn_in, ffn_conv_w, ffn_conv_b, w_ffn_out, final_norm_g):
    B, S, D = x.shape
    M = mem.shape[1]
    T = B * S
    depth = w_in.shape[0]
    assert S % (DIL_PATTERNS[-1][1] * DIL_BLOCK) == 0 and S % CHUNK == 0

    half = ROPE_DIMS // 2
    inv_freq = ROPE_THETA ** (-jnp.arange(half, dtype=F32) / half)
    d = jnp.arange(LANES) % HEAD_DIM
    invf_lanes = jnp.where(d < ROPE_DIMS, inv_freq[d % half], 0.0).reshape(1, LANES)
    rope = _rope_tables(positions.reshape(T, 1), invf_lanes)
    lb_all = jnp.cumsum(jax.nn.softmax(hgrn_lb_logits.astype(F32), axis=0), axis=0)
    lb_all = lb_all - lb_all[0:1]
    offs = (0, SEG_A, SEG_A + SEG_B, SEG_A + SEG_B + SEG_C, SEG_A + SEG_B + SEG_C + SEG_D, w_in.shape[2])
    mem2 = mem.reshape(B * M, D)
    qkv_slabs = lambda w: (True,) * (2 * w // LANES) + (False,) * (w // LANES)

    v_first = None
    for l in range(depth):
        lam_init = 0.8 - 0.6 * math.exp(-0.3 * l)
        w_l = w_in[l].astype(BF16)
        x2 = x.reshape(T, D)
        seg = lambda s: w_l[:, offs[s]:offs[s + 1]]
        seg_a = _norm_matmul(x2, mix_norm_g[l], seg(0), 3 * MIX_W, rope, qkv_slabs(MIX_W), slab_major=True)
        seg_b = _norm_matmul(x2, mix_norm_g[l], seg(1), SEG_B, rope, qkv_slabs(DIFF_W), BF16).reshape(B, S, SEG_B)
        seg_d = _norm_matmul(x2, mix_norm_g[l], seg(3), SEG_D).reshape(B, S, SEG_D)
        dil = [_dilated_group(seg_a, B, g, dilation) for g, (_, dilation) in enumerate(DIL_PATTERNS)]
        y_b = _diff_attention(seg_b, diff_lam[l], diff_norm_g[l], lam_init)
        y_c = _hgrn(x, mix_norm_g[l], seg(2), lb_all[l], hgrn_norm_g[l])
        rp = dict(mu=rwkv_mu[l], w0=rwkv_w0[l], w2=rwkv_w2[l], a0=rwkv_a0[l], a2=rwkv_a2[l], g2=rwkv_g2[l],
                  k_k=rwkv_k_k[l], k_a=rwkv_k_a[l], r_k=rwkv_r_k[l], lnx_g=rwkv_lnx_g[l], lnx_b=rwkv_lnx_b[l])
        if l > 0:
            rp.update(v0=rwkv_v0[l - 1], v1=rwkv_v1[l - 1], v2=rwkv_v2[l - 1])
        y_d, v_first = _rwkv(seg_d, rp, v_first)
        x2 = _merge(dil, y_b, y_c, y_d, x2, mix_norm_g[l], seg(4), p_a[l], p_b[l], p_c[l], p_d[l], w_mix_out[l])
        kv = _norm_matmul(mem2, mem_kv_norm_g[l], w_mem_kv[l].astype(BF16), D).reshape(B, M, 2 * D)
        x = _mem_attention(x2.reshape(B, S, D), mem_q_norm_g[l], w_mem_q[l], kv, w_mem_o[l])
        x = _ffn(x, ffn_norm_g[l], w_ffn_in[l], ffn_conv_w[l], ffn_conv_b[l], w_ffn_out[l],
                 final_norm_g if l == depth - 1 else None)
    return x
```

```python
import functools
import math

import jax
import jax.numpy as jnp
from jax import lax
from jax.experimental import pallas as pl
from jax.experimental.pallas import tpu as pltpu

F32 = jnp.float32
BF16 = jnp.bfloat16

NORM_EPS = 1e-5
HEAD_DIM = 64
ROPE_THETA = 500000.0
ROPE_DIMS = HEAD_DIM // 4
DIL_PATTERNS = ((128, 1), (512, 4), (2048, 16))
DIL_BLOCK = 128
N_HEADS = 4
MIX_W = N_HEADS * HEAD_DIM
DIFF_W = 2 * MIX_W
SEG_A = 3 * 3 * MIX_W
SEG_B = 3 * DIFF_W
SEG_C = 4 * MIX_W
SEG_D = 3 * MIX_W + 64 + 64 + 128
RWKV_GN_EPS = 1e-5 * HEAD_DIM
CHUNK = 64
SUB = 16
LANES = 128
VMEM_LIMIT = 56 * 1024 * 1024
LOG2E = math.log2(math.e)


def _cparams(n_axes):
    return pltpu.CompilerParams(dimension_semantics=("arbitrary",) * n_axes,
                                vmem_limit_bytes=VMEM_LIMIT)


def _dot(a, b):
    return jnp.dot(a, b, preferred_element_type=F32)


def _dot_nt(a, b):
    return lax.dot_general(a, b, (((1,), (1,)), ((), ())), preferred_element_type=F32)


def _dot_tn(a, b):
    return lax.dot_general(a, b, (((0,), (0,)), ((), ())), preferred_element_type=F32)


def _sigmoid(x):
    return 1.0 / (1.0 + jnp.exp(-x))


def _softplus(x):
    return jnp.maximum(x, 0.0) + jnp.log1p(jnp.exp(-jnp.abs(x)))


def _rms(x, g):
    ms = jnp.mean(x * x, axis=-1, keepdims=True)
    return x * lax.rsqrt(ms + NORM_EPS) * g


def _head_sum_matrix(width):
    r = lax.broadcasted_iota(jnp.int32, (width, width), 0) // HEAD_DIM
    c = lax.broadcasted_iota(jnp.int32, (width, width), 1) // HEAD_DIM
    return r == c


def _cumsum_rows(x):
    n = x.shape[0]
    row = lax.broadcasted_iota(jnp.int32, x.shape, 0)
    d = 1
    while d < n:
        x = x + jnp.where(row >= d, pltpu.roll(x, d, 0), 0.0)
        d *= 2
    return x


def _sum_heads(x, e_bf):
    hi = x.astype(BF16)
    lo = (x - hi.astype(F32)).astype(BF16)
    return _dot(hi, e_bf) + _dot(lo, e_bf)


def _rope_tables_body(pos_ref, invf_ref, cos_ref, sin_ref):
    half = ROPE_DIMS // 2
    d = lax.broadcasted_iota(jnp.int32, (1, LANES), 1) % HEAD_DIM
    ang = pos_ref[...].astype(F32) * invf_ref[...]
    s = jnp.sin(ang)
    cos_ref[...] = jnp.cos(ang)
    sin_ref[...] = jnp.where(d < half, -s, s)


def _rope_tables(positions, invf_lanes):
    T = positions.shape[0]
    tm = min(T, 2048)
    tab = pl.BlockSpec((tm, LANES), lambda i: (i, 0))
    return pl.pallas_call(
        _rope_tables_body,
        grid=(T // tm,),
        in_specs=[pl.BlockSpec((tm, 1), lambda i: (i, 0)), pl.BlockSpec((1, LANES), lambda i: (0, 0))],
        out_specs=[tab, tab],
        out_shape=[jax.ShapeDtypeStruct((T, LANES), F32)] * 2,
        compiler_params=_cparams(1),
        name="rope_tables",
    )(positions, invf_lanes)


def _norm_matmul_body(*refs, rope_slabs, slab_major):
    if rope_slabs:
        x_ref, g_ref, w_ref, cos_ref, sin_ref, o_ref, hn_ref = refs
    else:
        x_ref, g_ref, w_ref, o_ref, hn_ref = refs
    j = pl.program_id(1)
    half = ROPE_DIMS // 2
    d = lax.broadcasted_iota(jnp.int32, (1, LANES), 1) % HEAD_DIM

    @pl.when(j == 0)
    def _():
        hn_ref[...] = _rms(x_ref[...], g_ref[...]).astype(BF16)

    acc = _dot(hn_ref[...], w_ref[...])
    if not rope_slabs:
        o_ref[...] = acc.astype(o_ref.dtype)
        return
    for c, roped in enumerate(rope_slabs):
        t = acc[:, c * LANES:(c + 1) * LANES]
        if roped:
            partner = jnp.where(d < half, pltpu.roll(t, LANES - half, 1), pltpu.roll(t, half, 1))
            t = t * cos_ref[...] + partner * sin_ref[...]
        if slab_major:
            o_ref[c] = t.astype(o_ref.dtype)
        else:
            o_ref[:, c * LANES:(c + 1) * LANES] = t.astype(o_ref.dtype)


def _norm_matmul(x2, g, w_bf, tn, rope=None, rope_slabs=(), out_dtype=F32, slab_major=False):
    T, D = x2.shape
    N = w_bf.shape[1]
    tm = min(T, 1024)
    assert len(rope_slabs) in (0, tn // LANES)
    if slab_major:
        out_spec = pl.BlockSpec((tn // LANES, tm, LANES), lambda i, j: (j, i, 0))
        out_shape = jax.ShapeDtypeStruct((N // LANES, T, LANES), out_dtype)
    else:
        out_spec = pl.BlockSpec((tm, tn), lambda i, j: (i, j))
        out_shape = jax.ShapeDtypeStruct((T, N), out_dtype)
    in_specs = [pl.BlockSpec((tm, D), lambda i, j: (i, 0)),
                pl.BlockSpec((1, D), lambda i, j: (0, 0)),
                pl.BlockSpec((D, tn), lambda i, j: (0, j))]
    args = [x2, g.reshape(1, D), w_bf]
    scratch = [pltpu.VMEM((tm, D), BF16)]
    if rope_slabs:
        in_specs += [pl.BlockSpec((tm, LANES), lambda i, j: (i, 0))] * 2
        args += list(rope)
    return pl.pallas_call(
        functools.partial(_norm_matmul_body, rope_slabs=tuple(rope_slabs), slab_major=slab_major),
        grid=(T // tm, N // tn),
        in_specs=in_specs,
        out_specs=out_spec,
        out_shape=out_shape,
        scratch_shapes=scratch,
        compiler_params=_cparams(2),
        name="norm_matmul_rope" if rope_slabs else "norm_matmul",
    )(*args)


def _dilated_body(q_ref, kc_ref, kp_ref, vc_ref, vp_ref, o_ref, lse_ref, *, dilation, nbs, width):
    n = pl.program_id(2)
    blk = DIL_BLOCK
    unit = blk * dilation
    n_lane_heads = LANES // HEAD_DIM
    lane_head = lax.broadcasted_iota(jnp.int32, (1, LANES), 1) // HEAD_DIM
    qi = lax.broadcasted_iota(jnp.int32, (blk, blk), 0)
    kj = lax.broadcasted_iota(jnp.int32, (blk, blk), 1)
    cur_ok = kj <= qi
    first_prev_ok = (kj - qi) >= jnp.where(n > 0, 0, 2 * blk)
    later_prev_ok = kj >= qi
    ones = jnp.ones((blk, LANES), BF16)

    def rows_of(j, r):
        start = j * unit + r
        return pl.ds(start, blk) if dilation == 1 else pl.ds(start, blk, stride=dilation)

    def process(units):
        loaded = []
        for j, r in units:
            rows = rows_of(j, r)
            prows = rows_of(max(j - 1, 0), r)
            kpr, vpr = (kp_ref, vp_ref) if j == 0 else (kc_ref, vc_ref)
            loaded.append((rows, first_prev_ok if j == 0 else later_prev_ok,
                           q_ref[0, rows, :] * (HEAD_DIM ** -0.5),
                           kc_ref[0, rows, :].astype(BF16), kpr[0, prows, :].astype(BF16),
                           jnp.concatenate([vc_ref[0, rows, :].astype(BF16), ones], axis=1),
                           jnp.concatenate([vpr[0, prows, :].astype(BF16), ones], axis=1)))
        heads = [(u, h) for u in range(len(units)) for h in range(n_lane_heads)]
        scores = []
        for u, h in heads:
            _, prev_ok, q, kc, kp, _, _ = loaded[u]
            qh = jnp.where(lane_head == h, q, 0.0).astype(BF16)
            scores.append((jnp.where(cur_ok, _dot_nt(qh, kc), -jnp.inf), jnp.where(prev_ok, _dot_nt(qh, kp), -jnp.inf)))
        probs = []
        for sc, sp in scores:
            m = jnp.max(jnp.maximum(sc, sp), axis=-1, keepdims=True)
            probs.append((m, jnp.exp(sc - m).astype(BF16), jnp.exp(sp - m).astype(BF16)))
        outs = []
        for (u, h), (m, pc, pp) in zip(heads, probs):
            ext = _dot(pc, loaded[u][5]) + _dot(pp, loaded[u][6])
            den = ext[:, LANES:]
            outs.append((ext[:, :LANES] / den, m + jnp.log(den)))
        for u in range(len(units)):
            o_acc = jnp.zeros((blk, LANES), F32)
            lse_acc = jnp.zeros((blk, LANES), F32)
            for h in range(n_lane_heads):
                oh, lse = outs[u * n_lane_heads + h]
                o_acc = jnp.where(lane_head == h, oh, o_acc)
                lse_acc = jnp.where(lane_head == h, lse, lse_acc)
            o_ref[0, loaded[u][0], :] = o_acc
            lse_ref[0, loaded[u][0], :] = lse_acc

    if dilation == 1:
        for j in range(0, nbs, width):
            process([(j + w, 0) for w in range(width)])
    else:
        step = dilation // width
        for j in range(nbs):
            def body(r, _, j=j):
                process([(j, r + w * step) for w in range(width)])
                return 0
            lax.fori_loop(0, step, body, 0)


def _dilated_group(seg_a, B, g, dilation):
    _, T, _ = seg_a.shape
    S = T // B
    unit = DIL_BLOCK * dilation
    width = 8 if dilation == 1 else 4
    nbs = max(1, width * DIL_BLOCK // unit)
    tr = unit * nbs
    slab = 2 * (g * 3)

    def cur(which):
        return pl.BlockSpec((1, tr, LANES), lambda b, p, n: (slab + 2 * which + p, b * (S // tr) + n, 0))

    def prev(which):
        return pl.BlockSpec((1, unit, LANES),
                            lambda b, p, n: (slab + 2 * which + p, b * (S // unit) + jnp.maximum(n * nbs - 1, 0), 0))

    out_spec = pl.BlockSpec((1, tr, LANES), lambda b, p, n: (p, b * (S // tr) + n, 0))
    return pl.pallas_call(
        functools.partial(_dilated_body, dilation=dilation, nbs=nbs, width=width),
        grid=(B, MIX_W // LANES, S // tr),
        in_specs=[cur(0), cur(1), prev(1), cur(2), prev(2)],
        out_specs=[out_spec, out_spec],
        out_shape=[jax.ShapeDtypeStruct((MIX_W // LANES, T, LANES), F32)] * 2,
        compiler_params=_cparams(3),
        name="dilated_attention",
    )(seg_a, seg_a, seg_a, seg_a, seg_a)


def _diff_body(q_ref, k_ref, v_ref, lam_ref, g_ref, o_ref, vt_ref, a1_ref, a2_ref, sa_ref, sb_ref, *, tq, kc, lam_init):
    qi = pl.program_id(2)
    n_kv = vt_ref.shape[0]

    @pl.when(qi == 0)
    def _():
        for c in range(n_kv):
            vt_ref[c] = v_ref[0, c * kc:(c + 1) * kc, :].astype(F32).T.astype(BF16)

    q = q_ref[0].astype(F32) * (HEAD_DIM ** -0.5 * LOG2E)
    lane = lax.broadcasted_iota(jnp.int32, (1, 2 * HEAD_DIM), 1)
    q1 = jnp.where(lane < HEAD_DIM, q, 0.0).astype(BF16)
    q2 = jnp.where(lane >= HEAD_DIM, q, 0.0).astype(BF16)
    key = lax.broadcasted_iota(jnp.int32, (kc, tq), 0)
    qry = lax.broadcasted_iota(jnp.int32, (kc, tq), 1)
    a1_ref[...] = jnp.zeros_like(a1_ref)
    a2_ref[...] = jnp.zeros_like(a2_ref)

    def scores(kb, s_ref):
        k = k_ref[0, pl.ds(pl.multiple_of(kb * kc, kc), kc), :]
        s_ref[0] = _dot_nt(k, q1)
        s_ref[1] = _dot_nt(k, q2)

    def softmax_pv(kb, s_ref, carry, masked):
        m1, l1, m2, l2 = carry
        vt = vt_ref[kb]
        s1 = s_ref[0]
        s2 = s_ref[1]
        if masked:
            s1 = jnp.where(key <= qry, s1, -jnp.inf)
            s2 = jnp.where(key <= qry, s2, -jnp.inf)
        n1 = jnp.maximum(m1, jnp.max(s1, axis=0, keepdims=True))
        n2 = jnp.maximum(m2, jnp.max(s2, axis=0, keepdims=True))
        p1 = jnp.exp2(s1 - n1)
        p2 = jnp.exp2(s2 - n2)
        al1 = jnp.exp2(m1 - n1)
        al2 = jnp.exp2(m2 - n2)
        pv1 = _dot(vt, p1.astype(BF16))
        pv2 = _dot(vt, p2.astype(BF16))
        l1 = l1 * al1 + jnp.sum(p1.reshape(kc // 8, 8, tq), axis=0)
        l2 = l2 * al2 + jnp.sum(p2.reshape(kc // 8, 8, tq), axis=0)
        a1_ref[...] = a1_ref[...] * al1 + pv1
        a2_ref[...] = a2_ref[...] * al2 + pv2
        return n1, l1, n2, l2

    def pair(i, carry):
        scores(2 * i + 1, sb_ref)
        carry = softmax_pv(2 * i, sa_ref, carry, False)
        scores(2 * i + 2, sa_ref)
        return softmax_pv(2 * i + 1, sb_ref, carry, False)

    def finish(carry):
        _, l1, _, l2 = carry
        lv = lam_ref[...]
        lam = (jnp.exp(jnp.sum(lv[0:1] * lv[1:2], axis=-1, keepdims=True))
               - jnp.exp(jnp.sum(lv[2:3] * lv[3:4], axis=-1, keepdims=True)) + lam_init)
        o_t = (a1_ref[...] / jnp.sum(l1, axis=0, keepdims=True)
               - lam * (a2_ref[...] / jnp.sum(l2, axis=0, keepdims=True)))
        o_ref[0] = _rms(o_t.T, g_ref[...]) * (1.0 - lam_init)

    neg = jnp.full((1, tq), -jnp.inf, F32)
    zero = jnp.zeros((8, tq), F32)
    init = (neg, zero, neg, zero)
    scores(0, sa_ref)

    @pl.when(qi % 2 == 0)
    def _():
        carry = lax.fori_loop(0, qi // 2, pair, init)
        finish(softmax_pv(qi, sa_ref, carry, True))

    @pl.when(qi % 2 == 1)
    def _():
        carry = lax.fori_loop(0, qi // 2, pair, init)
        scores(qi, sb_ref)
        carry = softmax_pv(qi - 1, sa_ref, carry, False)
        finish(softmax_pv(qi, sb_ref, carry, True))


def _diff_attention(seg_b, B, lam_vecs, norm_g, lam_init, tq=512):
    _, T, hw = seg_b.shape
    S = T // B
    tq = min(S, tq)
    kc = tq
    return pl.pallas_call(
        functools.partial(_diff_body, tq=tq, kc=kc, lam_init=lam_init),
        grid=(B, N_HEADS, S // tq),
        in_specs=[pl.BlockSpec((1, tq, hw), lambda b, h, i: (h, b * (S // tq) + i, 0)),
                  pl.BlockSpec((1, S, hw), lambda b, h, i: (N_HEADS + h, b, 0)),
                  pl.BlockSpec((1, S, hw), lambda b, h, i: (2 * N_HEADS + h, b, 0)),
                  pl.BlockSpec((4, HEAD_DIM), lambda b, h, i: (0, 0)),
                  pl.BlockSpec((1, hw), lambda b, h, i: (0, 0))],
        out_specs=pl.BlockSpec((1, tq, hw), lambda b, h, i: (b, i, h)),
        out_shape=jax.ShapeDtypeStruct((B, S, DIFF_W), F32),
        scratch_shapes=[pltpu.VMEM((S // kc, hw, kc), BF16), pltpu.VMEM((hw, tq), F32), pltpu.VMEM((hw, tq), F32),
                        pltpu.VMEM((2, kc, tq), F32), pltpu.VMEM((2, kc, tq), F32)],
        compiler_params=_cparams(3),
        name="diff_attention",
    )(seg_b, seg_b, seg_b, lam_vecs, norm_g.reshape(1, hw))


def _hgrn_body(x_ref, xg_ref, w_ref, la_ref, lc_ref, oml_ref, ng_ref, o_ref, st_ref, seg_ref, *, tr, unroll):
    t = pl.program_id(1)
    c = SUB
    W = MIX_W

    @pl.when(t == 0)
    def _():
        st_ref[...] = jnp.zeros_like(st_ref)

    seg_ref[...] = _dot(_rms(x_ref[0], xg_ref[...]).astype(BF16), w_ref[...])

    e_bf = jnp.where(_head_sum_matrix(W), 1.0, 0.0).astype(BF16)
    row = lax.broadcasted_iota(jnp.int32, (c, W), 0)
    head = lax.broadcasted_iota(jnp.int32, (1, W), 1) // HEAD_DIM

    def stack(x):
        return jnp.concatenate([jnp.where(head == h, x, 0.0) for h in range(N_HEADS)], axis=0).astype(BF16)

    def group(j, _):
        base = j * (c * unroll)
        rows = [pl.ds(pl.multiple_of(base + u * c, c), c) for u in range(unroll)]
        us = range(unroll)
        qr = [seg_ref[r, 0:W] for r in rows]
        f = [seg_ref[r, W:2 * W] for r in rows]
        iv = [seg_ref[r, 2 * W:3 * W] for r in rows]
        gr = [seg_ref[r, 3 * W:4 * W] for r in rows]
        q = [x * _sigmoid(x) for x in qr]
        log_sig = [jnp.minimum(x, 0.0) - jnp.log1p(jnp.exp(-jnp.abs(x))) for x in f]
        x1 = la_ref[...]
        x2 = [lc_ref[...] + x for x in log_sig]
        log_f = [jnp.maximum(x1, x) + jnp.log1p(jnp.exp(-jnp.abs(x1 - x))) for x in x2]
        k = [oml_ref[...] * _sigmoid(-x) for x in f]
        b = [_cumsum_rows(x) for x in log_f]
        qe_st = [stack(q[u] * jnp.exp(b[u])) for u in us]
        b2 = [x * LOG2E for x in b]
        pmat = []
        for u in us:
            ps = []
            for s in range(c):
                e = jnp.exp2(jnp.where(row >= s, b2[u] - b2[u][s:s + 1], -jnp.inf))
                ps.append((q[u] * e * k[u][s:s + 1]).astype(BF16))
            pmat.append(jnp.concatenate(ps, axis=0))
        a = [_dot(p, e_bf) for p in pmat]
        o_intra = []
        for u in us:
            o = a[u][0:c] * iv[u][0:1]
            for s in range(1, c):
                o = o + a[u][s * c:(s + 1) * c] * iv[u][s:s + 1]
            o_intra.append(o)
        b_last = [x[c - 1:c] for x in b]
        kt_st = [stack(k[u] * jnp.exp(b_last[u] - b[u])) for u in us]
        i_heads = [jnp.concatenate([x[:, h * HEAD_DIM:(h + 1) * HEAD_DIM] for h in range(N_HEADS)], axis=0).astype(BF16)
                   for x in iv]
        upd = [_dot_tn(i_heads[u], kt_st[u]) for u in us]
        dec = [jnp.exp(x) for x in b_last]
        st = st_ref[...]
        o_heads = []
        for u in us:
            o_heads.append(_dot_nt(qe_st[u], st.astype(BF16)))
            st = st * dec[u] + upd[u]
        st_ref[...] = st
        for u in us:
            o = o_intra[u] + jnp.concatenate([o_heads[u][h * c:(h + 1) * c] for h in range(N_HEADS)], axis=1)
            ms = _sum_heads(o * o, e_bf) * (1.0 / HEAD_DIM)
            o_ref[0, rows[u], :] = o * lax.rsqrt(ms + NORM_EPS) * ng_ref[...] * (gr[u] * _sigmoid(gr[u]))
        return 0

    lax.fori_loop(0, tr // (c * unroll), group, 0)


def _hgrn(x, mix_g, w_c, lb, norm_g, tr=256, unroll=16):
    B, S, D = x.shape
    tr = min(S, tr)
    W = MIX_W
    vec = pl.BlockSpec((1, W), lambda b, t: (0, 0))
    lb = lb.reshape(1, W)
    return pl.pallas_call(
        functools.partial(_hgrn_body, tr=tr, unroll=unroll),
        grid=(B, S // tr),
        in_specs=[pl.BlockSpec((1, tr, D), lambda b, t: (b, t, 0)),
                  pl.BlockSpec((1, D), lambda b, t: (0, 0)),
                  pl.BlockSpec((D, SEG_C), lambda b, t: (0, 0)),
                  vec, vec, vec, vec],
        out_specs=pl.BlockSpec((1, tr, W), lambda b, t: (b, t, 0)),
        out_shape=jax.ShapeDtypeStruct((B, S, W), F32),
        scratch_shapes=[pltpu.VMEM((HEAD_DIM, W), F32), pltpu.VMEM((tr, SEG_C), F32)],
        compiler_params=_cparams(2),
        name="hgrn2",
    )(x, mix_g.reshape(1, D), w_c, jnp.log(lb), jnp.log1p(-lb), 1.0 - lb,
      jnp.tile(norm_g.reshape(1, HEAD_DIM), (1, N_HEADS)))


def _rwkv_body(*refs, tr, first_layer):
    if first_layer:
        (x_ref, mu_ref, w0_ref, w2_ref, a0_ref, a2_ref, g2_ref, kk_ref, ka_ref, rk_ref, lg_ref, lb_ref,
         y_ref, vf_out_ref,
         carry_ref, st_ref) = refs
    else:
        (x_ref, vf_ref, mu_ref, w0_ref, w2_ref, a0_ref, a2_ref, g2_ref, kk_ref, ka_ref, rk_ref, lg_ref, lb_ref,
         v0_ref, v1_ref, v2_ref,
         y_ref,
         carry_ref, st_ref) = refs
    t = pl.program_id(1)
    C = CHUNK
    W = MIX_W

    @pl.when(t == 0)
    def _():
        st_ref[...] = jnp.zeros_like(st_ref)
        carry_ref[...] = jnp.zeros_like(carry_ref)

    same_head = _head_sum_matrix(W)
    e_bf = jnp.where(same_head, 1.0, 0.0).astype(BF16)
    head = lax.broadcasted_iota(jnp.int32, (1, W), 1) // HEAD_DIM

    xs = x_ref[0]
    rowi = lax.broadcasted_iota(jnp.int32, xs.shape, 0)
    prev = jnp.where(rowi == 0, carry_ref[0:1, :], pltpu.roll(xs, 1, 0))
    carry_ref[0:1, :] = xs[tr - 1:tr, :]
    xm = xs + (prev - xs) * mu_ref[...]
    r = xm[:, 0:W]
    k = xm[:, W:2 * W]
    v = xm[:, 2 * W:3 * W]
    w_low = xm[:, 3 * W:3 * W + 64]
    a_low = xm[:, 3 * W + 64:3 * W + 128]
    g_low = xm[:, 3 * W + 128:3 * W + 256]
    wlog = -_softplus(-(w0_ref[...] + _dot(jnp.tanh(w_low).astype(BF16), w2_ref[...]))) - 0.5
    a = _sigmoid(a0_ref[...] + _dot(a_low.astype(BF16), a2_ref[...]))
    g = _dot(_sigmoid(g_low).astype(BF16), g2_ref[...])
    kkr = k * kk_ref[...]
    kn = kkr / jnp.maximum(jnp.sqrt(_sum_heads(kkr * kkr, e_bf)), 1e-12)
    k = k * (1.0 + (a - 1.0) * ka_ref[...])
    if first_layer:
        vf_out_ref[0] = v
    else:
        mix = _dot(_dot(v.astype(BF16), v1_ref[...]).astype(BF16), v2_ref[...])
        v = v + (vf_ref[0] - v) * _sigmoid(v0_ref[...] + mix)

    hc = N_HEADS * C
    tt = lax.broadcasted_iota(jnp.int32, (hc, hc), 0) % C
    ss = lax.broadcasted_iota(jnp.int32, (hc, hc), 1) % C
    strict = ss < tt
    incl = ss <= tt
    eye = jnp.where(lax.broadcasted_iota(jnp.int32, (hc, hc), 0) == lax.broadcasted_iota(jnp.int32, (hc, hc), 1),
                    1.0, 0.0)

    def stack(x):
        return jnp.concatenate([jnp.where(head == h, x, 0.0) for h in range(N_HEADS)], axis=0).astype(BF16)

    def fold(x):
        return x[0:C] + x[C:2 * C] + x[2 * C:3 * C] + x[3 * C:4 * C]

    n_ch = tr // C
    chunks = range(n_ch)
    sl = lambda x, c: x[c * C:(c + 1) * C]
    lw_all = -jnp.exp(wlog)
    bb = kn * a
    cs = [_cumsum_rows(sl(lw_all, c)) for c in chunks]
    r_st = [stack(sl(r, c) * jnp.exp(cs[c])) for c in chunks]
    n_st = [stack(sl(kn, c) * jnp.exp(cs[c] - sl(lw_all, c))) for c in chunks]
    kb_st = [jnp.concatenate([stack(sl(k, c) * jnp.exp(-cs[c])), stack(sl(bb, c) * jnp.exp(-cs[c]))], axis=0)
             for c in chunks]
    v_st = [stack(sl(v, c)) for c in chunks]
    gram_n = [_dot_nt(n_st[c], kb_st[c]) for c in chunks]
    gram_r = [_dot_nt(r_st[c], kb_st[c]) for c in chunks]
    l_k = [jnp.where(strict, g[:, :hc], 0.0).astype(BF16) for g in gram_n]
    l_b = [jnp.where(strict, g[:, hc:], 0.0) for g in gram_n]
    m_k = [jnp.where(incl, g[:, :hc], 0.0).astype(BF16) for g in gram_r]
    m_b = [jnp.where(incl, g[:, hc:], 0.0).astype(BF16) for g in gram_r]
    inv_t = [eye - x for x in l_b]
    pw = [_dot(x.astype(BF16), x.astype(BF16)) for x in l_b]
    n = 2
    while n < C:
        inv_t = [t_ + _dot(t_.astype(BF16), p_.astype(BF16)) for t_, p_ in zip(inv_t, pw)]
        n *= 2
        if n < C:
            pw = [_dot(p_.astype(BF16), p_.astype(BF16)) for p_ in pw]
    lkv = [_dot(l_k[c], v_st[c]) for c in chunks]
    mkv = [_dot(m_k[c], v_st[c]) for c in chunks]
    tw = [_dot(inv_t[c].astype(BF16), jnp.concatenate([n_st[c], lkv[c].astype(BF16)], axis=1)) for c in chunks]
    mw = [_dot(m_b[c], tw[c].astype(BF16)) for c in chunks]
    wnr = [jnp.concatenate([tw[c][:, :W], r_st[c].astype(F32) - mw[c][:, :W]], axis=0).astype(BF16) for c in chunks]
    u_a = [fold(tw[c][:, W:]) for c in chunks]
    y_a = [fold(mkv[c] - mw[c][:, W:]) for c in chunks]
    st = st_ref[...]
    ys = []
    for c in chunks:
        x = _dot_nt(wnr[c], st.astype(BF16))
        u = fold(x[:hc]) + u_a[c]
        ys.append(fold(x[hc:]) + y_a[c])
        c_last = cs[c][C - 1:C, :]
        dec = jnp.exp(c_last - cs[c])
        vu = jnp.concatenate([sl(v, c), -u], axis=0).astype(BF16)
        kb_end = jnp.concatenate([sl(k, c) * dec, sl(bb, c) * dec], axis=0).astype(BF16)
        st = st * jnp.exp(c_last) + jnp.where(same_head, _dot_tn(vu, kb_end), 0.0)
    st_ref[...] = st
    y = jnp.concatenate(ys, axis=0)

    mean = _sum_heads(y, e_bf) * (1.0 / HEAD_DIM)
    yc = y - mean
    var = _sum_heads(yc * yc, e_bf) * (1.0 / HEAD_DIM)
    yn = yc * lax.rsqrt(var + RWKV_GN_EPS) * lg_ref[...] + lb_ref[...]
    bonus = _sum_heads(r * k * rk_ref[...], e_bf) * v
    y_ref[0] = (yn + bonus) * g


def _rwkv(seg_d, p, v_first, tr=256):
    B, S, _ = seg_d.shape
    tr = min(S, tr)
    W = MIX_W
    first = v_first is None
    row = lambda a: a.reshape(1, -1)
    full = lambda a: pl.BlockSpec(a.shape, lambda b, t: (0,) * a.ndim)
    tile = lambda w: pl.BlockSpec((1, tr, w), lambda b, t: (b, t, 0))
    params = [row(p['mu']), row(p['w0']), p['w2'].astype(BF16), row(p['a0']), p['a2'].astype(BF16),
              p['g2'].astype(BF16), row(p['k_k']), row(p['k_a']), row(p['r_k']), row(p['lnx_g']), row(p['lnx_b'])]
    args = [seg_d]
    in_specs = [tile(SEG_D)]
    if not first:
        args.append(v_first)
        in_specs.append(tile(W))
        params += [row(p['v0']), p['v1'].astype(BF16), p['v2'].astype(BF16)]
    args += params
    in_specs += [full(a) for a in params]
    n_out = 2 if first else 1
    outs = pl.pallas_call(
        functools.partial(_rwkv_body, tr=tr, first_layer=first),
        grid=(B, S // tr),
        in_specs=in_specs,
        out_specs=[tile(W)] * n_out,
        out_shape=[jax.ShapeDtypeStruct((B, S, W), F32)] * n_out,
        scratch_shapes=[pltpu.VMEM((8, SEG_D), F32), pltpu.VMEM((W, W), F32)],
        compiler_params=_cparams(2),
        name="rwkv7",
    )(*args)
    return (outs[0], outs[1]) if first else (outs[0], v_first)


def _merge_body(o0, l0, o1, l1, o2, l2, yb, yc, yd, x_ref, ng_ref, wg, pa, pb, pc, pd, wo, out_ref):
    x = x_ref[...]
    D = x.shape[-1]
    hn = _rms(x, ng_ref[...]).astype(BF16)
    wide = lambda ref: jnp.concatenate([ref[0], ref[1]], axis=1)
    la, lb, lc = wide(l0), wide(l1), wide(l2)
    m = jnp.maximum(jnp.maximum(la, lb), lc)
    e0, e1, e2 = jnp.exp(la - m), jnp.exp(lb - m), jnp.exp(lc - m)
    inv = 1.0 / (e0 + e1 + e2)
    y_a = (e0 * inv) * wide(o0) + (e1 * inv) * wide(o1) + (e2 * inv) * wide(o2)
    merged = jnp.zeros_like(x)
    for j, (y, p) in enumerate(((y_a, pa), (yb[...], pb), (yc[...], pc), (yd[...], pd))):
        gate = _sigmoid(_dot(hn, wg[:, j * D:(j + 1) * D]))
        merged = merged + gate * _dot(y.astype(BF16), p[...])
    out_ref[...] = x + _dot(merged.astype(BF16), wo[...])


def _merge(dil, y_b, y_c, y_d, x2, norm_g, w_gate, p_a, p_b, p_c, p_d, w_out):
    T, D = x2.shape
    tm = min(T, 512)
    rows = lambda w: pl.BlockSpec((tm, w), lambda i: (i, 0))
    full = lambda a: pl.BlockSpec(a.shape, lambda i: (0, 0))
    flat = lambda a: a.reshape(T, a.shape[-1])
    acts = []
    for o, lse in dil:
        acts += [o, lse]
    acts += [flat(y_b), flat(y_c), flat(y_d)]
    weights = [norm_g.reshape(1, D), w_gate] + [w.astype(BF16) for w in (p_a, p_b, p_c, p_d, w_out)]
    slabs = pl.BlockSpec((MIX_W // LANES, tm, LANES), lambda i: (0, i, 0))
    in_specs = ([slabs] * 6 + [rows(DIFF_W), rows(MIX_W), rows(MIX_W), rows(D)] + [full(w) for w in weights])
    return pl.pallas_call(
        _merge_body,
        grid=(T // tm,),
        in_specs=in_specs,
        out_specs=rows(D),
        out_shape=jax.ShapeDtypeStruct((T, D), F32),
        compiler_params=_cparams(1),
        name="gated_merge",
    )(*acts, x2, *weights)


def _mem_body(x_ref, g_ref, wq_ref, kv_ref, wo_ref, out_ref, *, n_heads):
    x = x_ref[0]
    D = x.shape[-1]
    dm = D // n_heads
    q = _dot(_rms(x, g_ref[...]).astype(BF16), wq_ref[...]) * (dm ** -0.5)
    kv = kv_ref[0].astype(BF16)
    outs = []
    for h in range(n_heads):
        s = _dot_nt(q[:, h * dm:(h + 1) * dm].astype(BF16), kv[:, h * dm:(h + 1) * dm])
        p = jnp.exp(s - jnp.max(s, axis=-1, keepdims=True))
        p = p / jnp.sum(p, axis=-1, keepdims=True)
        outs.append(_dot(p.astype(BF16), kv[:, D + h * dm:D + (h + 1) * dm]))
    o = jnp.concatenate(outs, axis=-1)
    out_ref[0] = x + _dot(o.astype(BF16), wo_ref[...])


def _mem_attention(x, g, w_q, kv, w_o, n_heads=4):
    B, S, D = x.shape
    M = kv.shape[1]
    tm = min(S, 512)
    full = lambda a: pl.BlockSpec(a.shape, lambda b, i: (0, 0))
    wq, wo = w_q.astype(BF16), w_o.astype(BF16)
    g = g.reshape(1, D)
    return pl.pallas_call(
        functools.partial(_mem_body, n_heads=n_heads),
        grid=(B, S // tm),
        in_specs=[pl.BlockSpec((1, tm, D), lambda b, i: (b, i, 0)), full(g), full(wq),
                  pl.BlockSpec((1, M, 2 * D), lambda b, i: (b, 0, 0)), full(wo)],
        out_specs=pl.BlockSpec((1, tm, D), lambda b, i: (b, i, 0)),
        out_shape=jax.ShapeDtypeStruct((B, S, D), F32),
        compiler_params=_cparams(2),
        name="mem_attention",
    )(x, g, wq, kv, wo)


def _ffn_body(x_ref, halo_ref, g_ref, wg_ref, wv_ref, cwg_ref, cwv_ref, cbg_ref, cbv_ref, wo_ref, fg_ref,
              out_ref, hn_ref, hh_ref, acc_ref, *, final_norm):
    i = pl.program_id(1)
    c = pl.program_id(2)
    tm = x_ref.shape[1]

    @pl.when(c == 0)
    def _():
        hn_ref[...] = _rms(x_ref[0], g_ref[...]).astype(BF16)
        hh_ref[...] = _rms(halo_ref[0], g_ref[...]).astype(BF16)
        acc_ref[...] = jnp.zeros_like(acc_ref)

    live = jnp.where(i > 0, 1.0, 0.0)
    row = lax.broadcasted_iota(jnp.int32, (tm, 1), 0)

    def conv(w_ref, cw_ref, cb_ref):
        u = _dot(hn_ref[...], w_ref[...])
        uh = _dot(hh_ref[...], w_ref[...]) * live
        u1 = jnp.where(row == 0, uh[7:8], pltpu.roll(u, 1, 0))
        u2 = jnp.where(row == 0, uh[6:7], jnp.where(row == 1, uh[7:8], pltpu.roll(u, 2, 0)))
        cw = cw_ref[...]
        return cb_ref[...] + u2 * cw[0:1] + u1 * cw[1:2] + u * cw[2:3]

    gate = conv(wg_ref, cwg_ref, cbg_ref)
    val = conv(wv_ref, cwv_ref, cbv_ref)
    act = (gate * _sigmoid(gate) * val).astype(BF16)
    acc_ref[...] += _dot(act, wo_ref[...])

    @pl.when(c == pl.num_programs(2) - 1)
    def _():
        y = x_ref[0] + acc_ref[...]
        out_ref[0] = _rms(y, fg_ref[...]) if final_norm else y


def _ffn(x, g, w_in, conv_w, conv_b, w_out, final_g):
    B, S, D = x.shape
    d_ff = w_out.shape[0]
    tm = min(S, 1024)
    fc = d_ff
    nf = d_ff // fc
    w_in, w_out = w_in.astype(BF16), w_out.astype(BF16)
    conv_b = conv_b.reshape(1, 2 * d_ff)
    vec = pl.BlockSpec((1, D), lambda b, i, c: (0, 0))
    fg = (final_g if final_g is not None else g).reshape(1, D)
    return pl.pallas_call(
        functools.partial(_ffn_body, final_norm=final_g is not None),
        grid=(B, S // tm, nf),
        in_specs=[pl.BlockSpec((1, tm, D), lambda b, i, c: (b, i, 0)),
                  pl.BlockSpec((1, 8, D), lambda b, i, c: (b, jnp.maximum(i * (tm // 8) - 1, 0), 0)),
                  vec,
                  pl.BlockSpec((D, fc), lambda b, i, c: (0, c)),
                  pl.BlockSpec((D, fc), lambda b, i, c: (0, nf + c)),
                  pl.BlockSpec((3, fc), lambda b, i, c: (0, c)),
                  pl.BlockSpec((3, fc), lambda b, i, c: (0, nf + c)),
                  pl.BlockSpec((1, fc), lambda b, i, c: (0, c)),
                  pl.BlockSpec((1, fc), lambda b, i, c: (0, nf + c)),
                  pl.BlockSpec((fc, D), lambda b, i, c: (c, 0)),
                  vec],
        out_specs=pl.BlockSpec((1, tm, D), lambda b, i, c: (b, i, 0)),
        out_shape=jax.ShapeDtypeStruct((B, S, D), F32),
        scratch_shapes=[pltpu.VMEM((tm, D), BF16), pltpu.VMEM((8, D), BF16), pltpu.VMEM((tm, D), F32)],
        compiler_params=_cparams(3),
        name="conv_ffn",
    )(x, x, g.reshape(1, D), w_in, w_in, conv_w, conv_w, conv_b, conv_b, w_out, fg)


def kernel(x, mem, positions, mix_norm_g, w_in, diff_lam, diff_norm_g, hgrn_lb_logits, hgrn_norm_g, rwkv_mu, rwkv_w0, rwkv_w2, rwkv_a0, rwkv_a2, rwkv_g2, rwkv_k_k, rwkv_k_a, rwkv_r_k, rwkv_lnx_g, rwkv_lnx_b, rwkv_v0, rwkv_v1, rwkv_v2, p_a, p_b, p_c, p_d, w_mix_out, mem_q_norm_g, mem_kv_norm_g, w_mem_q, w_mem_kv, w_mem_o, ffn_norm_g, w_ffn_in, ffn_conv_w, ffn_conv_b, w_ffn_out, final_norm_g):
    B, S, D = x.shape
    M = mem.shape[1]
    T = B * S
    depth = w_in.shape[0]
    assert S % (DIL_PATTERNS[-1][1] * DIL_BLOCK) == 0 and S % CHUNK == 0

    half = ROPE_DIMS // 2
    inv_freq = ROPE_THETA ** (-jnp.arange(half, dtype=F32) / half)
    d = jnp.arange(LANES) % HEAD_DIM
    invf_lanes = jnp.where(d < ROPE_DIMS, inv_freq[d % half], 0.0).reshape(1, LANES)
    rope = _rope_tables(positions.reshape(T, 1), invf_lanes)
    lb_all = jnp.cumsum(jax.nn.softmax(hgrn_lb_logits.astype(F32), axis=0), axis=0)
    lb_all = lb_all - lb_all[0:1]
    offs = (0, SEG_A, SEG_A + SEG_B, SEG_A + SEG_B + SEG_C, SEG_A + SEG_B + SEG_C + SEG_D, w_in.shape[2])
    mem2 = mem.reshape(B * M, D)
    qkv_slabs = lambda w: (True,) * (2 * w // LANES) + (False,) * (w // LANES)

    v_first = None
    for l in range(depth):
        lam_init = 0.8 - 0.6 * math.exp(-0.3 * l)
        w_l = w_in[l].astype(BF16)
        x2 = x.reshape(T, D)
        seg = lambda s: w_l[:, offs[s]:offs[s + 1]]
        seg_a = _norm_matmul(x2, mix_norm_g[l], seg(0), 3 * MIX_W, rope, qkv_slabs(MIX_W), slab_major=True)
        seg_b = _norm_matmul(x2, mix_norm_g[l], seg(1), SEG_B, rope, qkv_slabs(DIFF_W), BF16, slab_major=True)
        seg_d = _norm_matmul(x2, mix_norm_g[l], seg(3), SEG_D).reshape(B, S, SEG_D)
        dil = [_dilated_group(seg_a, B, g, dilation) for g, (_, dilation) in enumerate(DIL_PATTERNS)]
        y_b = _diff_attention(seg_b, B, diff_lam[l], diff_norm_g[l], lam_init)
        y_c = _hgrn(x, mix_norm_g[l], seg(2), lb_all[l], hgrn_norm_g[l])
        rp = dict(mu=rwkv_mu[l], w0=rwkv_w0[l], w2=rwkv_w2[l], a0=rwkv_a0[l], a2=rwkv_a2[l], g2=rwkv_g2[l],
                  k_k=rwkv_k_k[l], k_a=rwkv_k_a[l], r_k=rwkv_r_k[l], lnx_g=rwkv_lnx_g[l], lnx_b=rwkv_lnx_b[l])
        if l > 0:
            rp.update(v0=rwkv_v0[l - 1], v1=rwkv_v1[l - 1], v2=rwkv_v2[l - 1])
        y_d, v_first = _rwkv(seg_d, rp, v_first)
        x2 = _merge(dil, y_b, y_c, y_d, x2, mix_norm_g[l], seg(4), p_a[l], p_b[l], p_c[l], p_d[l], w_mix_out[l])
        kv = _norm_matmul(mem2, mem_kv_norm_g[l], w_mem_kv[l].astype(BF16), D).reshape(B, M, 2 * D)
        x = _mem_attention(x2.reshape(B, S, D), mem_q_norm_g[l], w_mem_q[l], kv, w_mem_o[l])
        x = _ffn(x, ffn_norm_g[l], w_ffn_in[l], ffn_conv_w[l], ffn_conv_b[l], w_ffn_out[l],
                 final_norm_g if l == depth - 1 else None)
    return x
```

```python
import functools
import math

import jax
import jax.numpy as jnp
from jax import lax
from jax.experimental import pallas as pl
from jax.experimental.pallas import tpu as pltpu

F32 = jnp.float32
BF16 = jnp.bfloat16

NORM_EPS = 1e-5
HEAD_DIM = 64
ROPE_THETA = 500000.0
ROPE_DIMS = HEAD_DIM // 4
DIL_PATTERNS = ((128, 1), (512, 4), (2048, 16))
DIL_BLOCK = 128
N_HEADS = 4
MIX_W = N_HEADS * HEAD_DIM
DIFF_W = 2 * MIX_W
SEG_A = 3 * 3 * MIX_W
SEG_B = 3 * DIFF_W
SEG_C = 4 * MIX_W
SEG_D = 3 * MIX_W + 64 + 64 + 128
RWKV_GN_EPS = 1e-5 * HEAD_DIM
CHUNK = 64
SUB = 16
LANES = 128
VMEM_LIMIT = 56 * 1024 * 1024
LOG2E = math.log2(math.e)


def _cparams(n_axes):
    return pltpu.CompilerParams(dimension_semantics=("arbitrary",) * n_axes,
                                vmem_limit_bytes=VMEM_LIMIT)


def _dot(a, b):
    return jnp.dot(a, b, preferred_element_type=F32)


def _dot_nt(a, b):
    return lax.dot_general(a, b, (((1,), (1,)), ((), ())), preferred_element_type=F32)


def _dot_tn(a, b):
    return lax.dot_general(a, b, (((0,), (0,)), ((), ())), preferred_element_type=F32)


def _sigmoid(x):
    return 1.0 / (1.0 + jnp.exp(-x))


def _softplus(x):
    return jnp.maximum(x, 0.0) + jnp.log1p(jnp.exp(-jnp.abs(x)))


def _rms(x, g):
    ms = jnp.mean(x * x, axis=-1, keepdims=True)
    return x * lax.rsqrt(ms + NORM_EPS) * g


def _head_sum_matrix(width):
    r = lax.broadcasted_iota(jnp.int32, (width, width), 0) // HEAD_DIM
    c = lax.broadcasted_iota(jnp.int32, (width, width), 1) // HEAD_DIM
    return r == c


def _cumsum_rows(x):
    n = x.shape[0]
    row = lax.broadcasted_iota(jnp.int32, x.shape, 0)
    d = 1
    while d < n:
        x = x + jnp.where(row >= d, pltpu.roll(x, d, 0), 0.0)
        d *= 2
    return x


def _sum_heads(x, e_bf):
    hi = x.astype(BF16)
    lo = (x - hi.astype(F32)).astype(BF16)
    return _dot(hi, e_bf) + _dot(lo, e_bf)


def _rope_tables_body(pos_ref, invf_ref, cos_ref, sin_ref):
    half = ROPE_DIMS // 2
    d = lax.broadcasted_iota(jnp.int32, (1, LANES), 1) % HEAD_DIM
    ang = pos_ref[...].astype(F32) * invf_ref[...]
    s = jnp.sin(ang)
    cos_ref[...] = jnp.cos(ang)
    sin_ref[...] = jnp.where(d < half, -s, s)


def _rope_tables(positions, invf_lanes):
    T = positions.shape[0]
    tm = min(T, 2048)
    tab = pl.BlockSpec((tm, LANES), lambda i: (i, 0))
    return pl.pallas_call(
        _rope_tables_body,
        grid=(T // tm,),
        in_specs=[pl.BlockSpec((tm, 1), lambda i: (i, 0)), pl.BlockSpec((1, LANES), lambda i: (0, 0))],
        out_specs=[tab, tab],
        out_shape=[jax.ShapeDtypeStruct((T, LANES), F32)] * 2,
        compiler_params=_cparams(1),
        name="rope_tables",
    )(positions, invf_lanes)


def _norm_matmul_body(*refs, rope_slabs, slab_major):
    if rope_slabs:
        x_ref, g_ref, w_ref, cos_ref, sin_ref, o_ref, hn_ref = refs
    else:
        x_ref, g_ref, w_ref, o_ref, hn_ref = refs
    j = pl.program_id(1)
    half = ROPE_DIMS // 2
    d = lax.broadcasted_iota(jnp.int32, (1, LANES), 1) % HEAD_DIM

    @pl.when(j == 0)
    def _():
        hn_ref[...] = _rms(x_ref[...], g_ref[...]).astype(BF16)

    acc = _dot(hn_ref[...], w_ref[...])
    if not rope_slabs:
        o_ref[...] = acc.astype(o_ref.dtype)
        return
    for c, roped in enumerate(rope_slabs):
        t = acc[:, c * LANES:(c + 1) * LANES]
        if roped:
            partner = jnp.where(d < half, pltpu.roll(t, LANES - half, 1), pltpu.roll(t, half, 1))
            t = t * cos_ref[...] + partner * sin_ref[...]
        if slab_major:
            o_ref[c] = t.astype(o_ref.dtype)
        else:
            o_ref[:, c * LANES:(c + 1) * LANES] = t.astype(o_ref.dtype)


def _norm_matmul(x2, g, w_bf, tn, rope=None, rope_slabs=(), out_dtype=F32, slab_major=False):
    T, D = x2.shape
    N = w_bf.shape[1]
    tm = min(T, 1024)
    assert len(rope_slabs) in (0, tn // LANES)
    if slab_major:
        out_spec = pl.BlockSpec((tn // LANES, tm, LANES), lambda i, j: (j, i, 0))
        out_shape = jax.ShapeDtypeStruct((N // LANES, T, LANES), out_dtype)
    else:
        out_spec = pl.BlockSpec((tm, tn), lambda i, j: (i, j))
        out_shape = jax.ShapeDtypeStruct((T, N), out_dtype)
    in_specs = [pl.BlockSpec((tm, D), lambda i, j: (i, 0)),
                pl.BlockSpec((1, D), lambda i, j: (0, 0)),
                pl.BlockSpec((D, tn), lambda i, j: (0, j))]
    args = [x2, g.reshape(1, D), w_bf]
    scratch = [pltpu.VMEM((tm, D), BF16)]
    if rope_slabs:
        in_specs += [pl.BlockSpec((tm, LANES), lambda i, j: (i, 0))] * 2
        args += list(rope)
    return pl.pallas_call(
        functools.partial(_norm_matmul_body, rope_slabs=tuple(rope_slabs), slab_major=slab_major),
        grid=(T // tm, N // tn),
        in_specs=in_specs,
        out_specs=out_spec,
        out_shape=out_shape,
        scratch_shapes=scratch,
        compiler_params=_cparams(2),
        name="norm_matmul_rope" if rope_slabs else "norm_matmul",
    )(*args)


def _dilated_body(q_ref, kc_ref, kp_ref, vc_ref, vp_ref, o_ref, lse_ref, *, dilation, nbs, width):
    n = pl.program_id(2)
    blk = DIL_BLOCK
    unit = blk * dilation
    n_lane_heads = LANES // HEAD_DIM
    lane_head = lax.broadcasted_iota(jnp.int32, (1, LANES), 1) // HEAD_DIM
    qi = lax.broadcasted_iota(jnp.int32, (blk, blk), 0)
    kj = lax.broadcasted_iota(jnp.int32, (blk, blk), 1)
    cur_ok = kj <= qi
    first_prev_ok = (kj - qi) >= jnp.where(n > 0, 0, 2 * blk)
    later_prev_ok = kj >= qi
    ones = jnp.ones((blk, LANES), BF16)

    def rows_of(j, r):
        start = j * unit + r
        return pl.ds(start, blk) if dilation == 1 else pl.ds(start, blk, stride=dilation)

    def process(units):
        loaded = []
        for j, r in units:
            rows = rows_of(j, r)
            prows = rows_of(max(j - 1, 0), r)
            kpr, vpr = (kp_ref, vp_ref) if j == 0 else (kc_ref, vc_ref)
            loaded.append((rows, first_prev_ok if j == 0 else later_prev_ok,
                           q_ref[0, rows, :] * (HEAD_DIM ** -0.5),
                           kc_ref[0, rows, :].astype(BF16), kpr[0, prows, :].astype(BF16),
                           jnp.concatenate([vc_ref[0, rows, :].astype(BF16), ones], axis=1),
                           jnp.concatenate([vpr[0, prows, :].astype(BF16), ones], axis=1)))
        heads = [(u, h) for u in range(len(units)) for h in range(n_lane_heads)]
        scores = []
        for u, h in heads:
            _, prev_ok, q, kc, kp, _, _ = loaded[u]
            qh = jnp.where(lane_head == h, q, 0.0).astype(BF16)
            scores.append((jnp.where(cur_ok, _dot_nt(qh, kc), -jnp.inf), jnp.where(prev_ok, _dot_nt(qh, kp), -jnp.inf)))
        probs = []
        for sc, sp in scores:
            m = jnp.max(jnp.maximum(sc, sp), axis=-1, keepdims=True)
            probs.append((m, jnp.exp(sc - m).astype(BF16), jnp.exp(sp - m).astype(BF16)))
        outs = []
        for (u, h), (m, pc, pp) in zip(heads, probs):
            ext = _dot(pc, loaded[u][5]) + _dot(pp, loaded[u][6])
            den = ext[:, LANES:]
            outs.append((ext[:, :LANES] / den, m + jnp.log(den)))
        for u in range(len(units)):
            o_acc = jnp.zeros((blk, LANES), F32)
            lse_acc = jnp.zeros((blk, LANES), F32)
            for h in range(n_lane_heads):
                oh, lse = outs[u * n_lane_heads + h]
                o_acc = jnp.where(lane_head == h, oh, o_acc)
                lse_acc = jnp.where(lane_head == h, lse, lse_acc)
            o_ref[0, loaded[u][0], :] = o_acc
            lse_ref[0, loaded[u][0], :] = lse_acc

    if dilation == 1:
        for j in range(0, nbs, width):
            process([(j + w, 0) for w in range(width)])
    else:
        step = dilation // width
        for j in range(nbs):
            def body(r, _, j=j):
                process([(j, r + w * step) for w in range(width)])
                return 0
            lax.fori_loop(0, step, body, 0)


def _dilated_group(seg_a, B, g, dilation):
    _, T, _ = seg_a.shape
    S = T // B
    unit = DIL_BLOCK * dilation
    width = 8 if dilation == 1 else 4
    nbs = max(1, width * DIL_BLOCK // unit)
    tr = unit * nbs
    slab = 2 * (g * 3)

    def cur(which):
        return pl.BlockSpec((1, tr, LANES), lambda b, p, n: (slab + 2 * which + p, b * (S // tr) + n, 0))

    def prev(which):
        return pl.BlockSpec((1, unit, LANES),
                            lambda b, p, n: (slab + 2 * which + p, b * (S // unit) + jnp.maximum(n * nbs - 1, 0), 0))

    out_spec = pl.BlockSpec((1, tr, LANES), lambda b, p, n: (p, b * (S // tr) + n, 0))
    return pl.pallas_call(
        functools.partial(_dilated_body, dilation=dilation, nbs=nbs, width=width),
        grid=(B, MIX_W // LANES, S // tr),
        in_specs=[cur(0), cur(1), prev(1), cur(2), prev(2)],
        out_specs=[out_spec, out_spec],
        out_shape=[jax.ShapeDtypeStruct((MIX_W // LANES, T, LANES), F32)] * 2,
        compiler_params=_cparams(3),
        name="dilated_attention",
    )(seg_a, seg_a, seg_a, seg_a, seg_a)


def _diff_body(q_ref, k_ref, v_ref, lam_ref, g_ref, o_ref, vt_ref, a1_ref, a2_ref, sa_ref, sb_ref, *, tq, kc, lam_init):
    qi = pl.program_id(2)
    n_kv = vt_ref.shape[0]

    @pl.when(qi == 0)
    def _():
        for c in range(n_kv):
            vt_ref[c] = v_ref[0, c * kc:(c + 1) * kc, :].astype(F32).T.astype(BF16)

    q = q_ref[0].astype(F32) * (HEAD_DIM ** -0.5 * LOG2E)
    lane = lax.broadcasted_iota(jnp.int32, (1, 2 * HEAD_DIM), 1)
    q1 = jnp.where(lane < HEAD_DIM, q, 0.0).astype(BF16)
    q2 = jnp.where(lane >= HEAD_DIM, q, 0.0).astype(BF16)
    key = lax.broadcasted_iota(jnp.int32, (kc, tq), 0)
    qry = lax.broadcasted_iota(jnp.int32, (kc, tq), 1)
    a1_ref[...] = jnp.zeros_like(a1_ref)
    a2_ref[...] = jnp.zeros_like(a2_ref)

    def scores(kb, s_ref):
        k = k_ref[0, pl.ds(pl.multiple_of(kb * kc, kc), kc), :]
        s_ref[0] = _dot_nt(k, q1)
        s_ref[1] = _dot_nt(k, q2)

    def softmax_pv(kb, s_ref, carry, masked):
        m1, l1, m2, l2 = carry
        vt = vt_ref[kb]
        s1 = s_ref[0]
        s2 = s_ref[1]
        if masked:
            s1 = jnp.where(key <= qry, s1, -jnp.inf)
            s2 = jnp.where(key <= qry, s2, -jnp.inf)
        n1 = jnp.maximum(m1, jnp.max(s1, axis=0, keepdims=True))
        n2 = jnp.maximum(m2, jnp.max(s2, axis=0, keepdims=True))
        p1 = jnp.exp2(s1 - n1)
        p2 = jnp.exp2(s2 - n2)
        al1 = jnp.exp2(m1 - n1)
        al2 = jnp.exp2(m2 - n2)
        pv1 = _dot(vt, p1.astype(BF16))
        pv2 = _dot(vt, p2.astype(BF16))
        l1 = l1 * al1 + jnp.sum(p1.reshape(kc // 8, 8, tq), axis=0)
        l2 = l2 * al2 + jnp.sum(p2.reshape(kc // 8, 8, tq), axis=0)
        a1_ref[...] = a1_ref[...] * al1 + pv1
        a2_ref[...] = a2_ref[...] * al2 + pv2
        return n1, l1, n2, l2

    def pair(i, carry):
        scores(2 * i + 1, sb_ref)
        carry = softmax_pv(2 * i, sa_ref, carry, False)
        scores(2 * i + 2, sa_ref)
        return softmax_pv(2 * i + 1, sb_ref, carry, False)

    def finish(carry):
        _, l1, _, l2 = carry
        lv = lam_ref[...]
        lam = (jnp.exp(jnp.sum(lv[0:1] * lv[1:2], axis=-1, keepdims=True))
               - jnp.exp(jnp.sum(lv[2:3] * lv[3:4], axis=-1, keepdims=True)) + lam_init)
        o_t = (a1_ref[...] / jnp.sum(l1, axis=0, keepdims=True)
               - lam * (a2_ref[...] / jnp.sum(l2, axis=0, keepdims=True)))
        o_ref[0] = _rms(o_t.T, g_ref[...]) * (1.0 - lam_init)

    neg = jnp.full((1, tq), -jnp.inf, F32)
    zero = jnp.zeros((8, tq), F32)
    init = (neg, zero, neg, zero)
    scores(0, sa_ref)

    @pl.when(qi % 2 == 0)
    def _():
        carry = lax.fori_loop(0, qi // 2, pair, init)
        finish(softmax_pv(qi, sa_ref, carry, True))

    @pl.when(qi % 2 == 1)
    def _():
        carry = lax.fori_loop(0, qi // 2, pair, init)
        scores(qi, sb_ref)
        carry = softmax_pv(qi - 1, sa_ref, carry, False)
        finish(softmax_pv(qi, sb_ref, carry, True))


def _diff_attention(seg_b, B, lam_vecs, norm_g, lam_init, tq=512):
    _, T, hw = seg_b.shape
    S = T // B
    tq = min(S, tq)
    kc = tq
    return pl.pallas_call(
        functools.partial(_diff_body, tq=tq, kc=kc, lam_init=lam_init),
        grid=(B, N_HEADS, S // tq),
        in_specs=[pl.BlockSpec((1, tq, hw), lambda b, h, i: (h, b * (S // tq) + i, 0)),
                  pl.BlockSpec((1, S, hw), lambda b, h, i: (N_HEADS + h, b, 0)),
                  pl.BlockSpec((1, S, hw), lambda b, h, i: (2 * N_HEADS + h, b, 0)),
                  pl.BlockSpec((4, HEAD_DIM), lambda b, h, i: (0, 0)),
                  pl.BlockSpec((1, hw), lambda b, h, i: (0, 0))],
        out_specs=pl.BlockSpec((1, tq, hw), lambda b, h, i: (b, i, h)),
        out_shape=jax.ShapeDtypeStruct((B, S, DIFF_W), F32),
        scratch_shapes=[pltpu.VMEM((S // kc, hw, kc), BF16), pltpu.VMEM((hw, tq), F32), pltpu.VMEM((hw, tq), F32),
                        pltpu.VMEM((2, kc, tq), F32), pltpu.VMEM((2, kc, tq), F32)],
        compiler_params=_cparams(3),
        name="diff_attention",
    )(seg_b, seg_b, seg_b, lam_vecs, norm_g.reshape(1, hw))


def _hgrn_body(x_ref, xg_ref, w_ref, la_ref, lc_ref, oml_ref, ng_ref, o_ref, st_ref, seg_ref, *, tr, unroll):
    t = pl.program_id(1)
    c = SUB
    W = MIX_W

    @pl.when(t == 0)
    def _():
        st_ref[...] = jnp.zeros_like(st_ref)

    seg_ref[...] = _dot(_rms(x_ref[0], xg_ref[...]).astype(BF16), w_ref[...])

    e_bf = jnp.where(_head_sum_matrix(W), 1.0, 0.0).astype(BF16)
    row = lax.broadcasted_iota(jnp.int32, (c, W), 0)
    head = lax.broadcasted_iota(jnp.int32, (1, W), 1) // HEAD_DIM

    def stack(x):
        return jnp.concatenate([jnp.where(head == h, x, 0.0) for h in range(N_HEADS)], axis=0).astype(BF16)

    def group(j, _):
        base = j * (c * unroll)
        rows = [pl.ds(pl.multiple_of(base + u * c, c), c) for u in range(unroll)]
        us = range(unroll)
        qr = [seg_ref[r, 0:W] for r in rows]
        f = [seg_ref[r, W:2 * W] for r in rows]
        iv = [seg_ref[r, 2 * W:3 * W] for r in rows]
        gr = [seg_ref[r, 3 * W:4 * W] for r in rows]
        q = [x * _sigmoid(x) for x in qr]
        log_sig = [jnp.minimum(x, 0.0) - jnp.log1p(jnp.exp(-jnp.abs(x))) for x in f]
        x1 = la_ref[...]
        x2 = [lc_ref[...] + x for x in log_sig]
        log_f = [jnp.maximum(x1, x) + jnp.log1p(jnp.exp(-jnp.abs(x1 - x))) for x in x2]
        k = [oml_ref[...] * _sigmoid(-x) for x in f]
        b = [_cumsum_rows(x) for x in log_f]
        qe_st = [stack(q[u] * jnp.exp(b[u])) for u in us]
        b2 = [x * LOG2E for x in b]
        pmat = []
        for u in us:
            ps = []
            for s in range(c):
                e = jnp.exp2(jnp.where(row >= s, b2[u] - b2[u][s:s + 1], -jnp.inf))
                ps.append((q[u] * e * k[u][s:s + 1]).astype(BF16))
            pmat.append(jnp.concatenate(ps, axis=0))
        a = [_dot(p, e_bf) for p in pmat]
        o_intra = []
        for u in us:
            o = a[u][0:c] * iv[u][0:1]
            for s in range(1, c):
                o = o + a[u][s * c:(s + 1) * c] * iv[u][s:s + 1]
            o_intra.append(o)
        b_last = [x[c - 1:c] for x in b]
        kt_st = [stack(k[u] * jnp.exp(b_last[u] - b[u])) for u in us]
        i_heads = [jnp.concatenate([x[:, h * HEAD_DIM:(h + 1) * HEAD_DIM] for h in range(N_HEADS)], axis=0).astype(BF16)
                   for x in iv]
        upd = [_dot_tn(i_heads[u], kt_st[u]) for u in us]
        dec = [jnp.exp(x) for x in b_last]
        st = st_ref[...]
        o_heads = []
        for u in us:
            o_heads.append(_dot_nt(qe_st[u], st.astype(BF16)))
            st = st * dec[u] + upd[u]
        st_ref[...] = st
        for u in us:
            o = o_intra[u] + jnp.concatenate([o_heads[u][h * c:(h + 1) * c] for h in range(N_HEADS)], axis=1)
            ms = _sum_heads(o * o, e_bf) * (1.0 / HEAD_DIM)
            o_ref[0, rows[u], :] = o * lax.rsqrt(ms + NORM_EPS) * ng_ref[...] * (gr[u] * _sigmoid(gr[u]))
        return 0

    lax.fori_loop(0, tr // (c * unroll), group, 0)


def _hgrn(x, mix_g, w_c, lb, norm_g, tr=256, unroll=16):
    B, S, D = x.shape
    tr = min(S, tr)
    W = MIX_W
    vec = pl.BlockSpec((1, W), lambda b, t: (0, 0))
    lb = lb.reshape(1, W)
    return pl.pallas_call(
        functools.partial(_hgrn_body, tr=tr, unroll=unroll),
        grid=(B, S // tr),
        in_specs=[pl.BlockSpec((1, tr, D), lambda b, t: (b, t, 0)),
                  pl.BlockSpec((1, D), lambda b, t: (0, 0)),
                  pl.BlockSpec((D, SEG_C), lambda b, t: (0, 0)),
                  vec, vec, vec, vec],
        out_specs=pl.BlockSpec((1, tr, W), lambda b, t: (b, t, 0)),
        out_shape=jax.ShapeDtypeStruct((B, S, W), F32),
        scratch_shapes=[pltpu.VMEM((HEAD_DIM, W), F32), pltpu.VMEM((tr, SEG_C), F32)],
        compiler_params=_cparams(2),
        name="hgrn2",
    )(x, mix_g.reshape(1, D), w_c, jnp.log(lb), jnp.log1p(-lb), 1.0 - lb,
      jnp.tile(norm_g.reshape(1, HEAD_DIM), (1, N_HEADS)))


def _rwkv_body(*refs, tr, first_layer):
    if first_layer:
        (x_ref, mu_ref, w0_ref, w2_ref, a0_ref, a2_ref, g2_ref, kk_ref, ka_ref, rk_ref, lg_ref, lb_ref,
         y_ref, vf_out_ref,
         carry_ref, st_ref) = refs
    else:
        (x_ref, vf_ref, mu_ref, w0_ref, w2_ref, a0_ref, a2_ref, g2_ref, kk_ref, ka_ref, rk_ref, lg_ref, lb_ref,
         v0_ref, v1_ref, v2_ref,
         y_ref,
         carry_ref, st_ref) = refs
    t = pl.program_id(1)
    C = CHUNK
    W = MIX_W

    @pl.when(t == 0)
    def _():
        st_ref[...] = jnp.zeros_like(st_ref)
        carry_ref[...] = jnp.zeros_like(carry_ref)

    e_bf = jnp.where(_head_sum_matrix(W), 1.0, 0.0).astype(BF16)

    xs = x_ref[0]
    rowi = lax.broadcasted_iota(jnp.int32, xs.shape, 0)
    prev = jnp.where(rowi == 0, carry_ref[0:1, :], pltpu.roll(xs, 1, 0))
    carry_ref[0:1, :] = xs[tr - 1:tr, :]
    xm = xs + (prev - xs) * mu_ref[...]
    r = xm[:, 0:W]
    k = xm[:, W:2 * W]
    v = xm[:, 2 * W:3 * W]
    w_low = xm[:, 3 * W:3 * W + 64]
    a_low = xm[:, 3 * W + 64:3 * W + 128]
    g_low = xm[:, 3 * W + 128:3 * W + 256]
    wlog = -_softplus(-(w0_ref[...] + _dot(jnp.tanh(w_low).astype(BF16), w2_ref[...]))) - 0.5
    a = _sigmoid(a0_ref[...] + _dot(a_low.astype(BF16), a2_ref[...]))
    g = _dot(_sigmoid(g_low).astype(BF16), g2_ref[...])
    kkr = k * kk_ref[...]
    kn = kkr / jnp.maximum(jnp.sqrt(_sum_heads(kkr * kkr, e_bf)), 1e-12)
    k = k * (1.0 + (a - 1.0) * ka_ref[...])
    if first_layer:
        vf_out_ref[0] = v
    else:
        mix = _dot(_dot(v.astype(BF16), v1_ref[...]).astype(BF16), v2_ref[...])
        v = v + (vf_ref[0] - v) * _sigmoid(v0_ref[...] + mix)

    HP = LANES // HEAD_DIM
    hc = HP * C
    n_pairs = W // LANES
    tt = lax.broadcasted_iota(jnp.int32, (hc, hc), 0) % C
    ss = lax.broadcasted_iota(jnp.int32, (hc, hc), 1) % C
    strict = ss < tt
    incl = ss <= tt
    eye = jnp.where(lax.broadcasted_iota(jnp.int32, (hc, hc), 0) == lax.broadcasted_iota(jnp.int32, (hc, hc), 1),
                    1.0, 0.0)
    same_head_p = _head_sum_matrix(LANES)
    lane_head = lax.broadcasted_iota(jnp.int32, (1, LANES), 1) // HEAD_DIM

    def stack(x):
        return jnp.concatenate([jnp.where(lane_head == j, x, 0.0) for j in range(HP)], axis=0).astype(BF16)

    def fold(x):
        return x[0:C] + x[C:2 * C]

    n_ch = tr // C
    units = [(c, p) for c in range(n_ch) for p in range(n_pairs)]
    sl = lambda x, u: x[u[0] * C:(u[0] + 1) * C, u[1] * LANES:(u[1] + 1) * LANES]
    idx = range(len(units))
    lw_all = -jnp.exp(wlog)
    bb = kn * a
    cs_c = [_cumsum_rows(lw_all[c * C:(c + 1) * C]) for c in range(n_ch)]
    cs = [cs_c[u[0]][:, u[1] * LANES:(u[1] + 1) * LANES] for u in units]
    r_st = [stack(sl(r, units[i]) * jnp.exp(cs[i])) for i in idx]
    n_st = [stack(sl(kn, units[i]) * jnp.exp(cs[i] - sl(lw_all, units[i]))) for i in idx]
    kb_st = [jnp.concatenate([stack(sl(k, units[i]) * jnp.exp(-cs[i])), stack(sl(bb, units[i]) * jnp.exp(-cs[i]))],
                             axis=0) for i in idx]
    v_st = [stack(sl(v, units[i])) for i in idx]
    gram_n = [_dot_nt(n_st[i], kb_st[i]) for i in idx]
    gram_r = [_dot_nt(r_st[i], kb_st[i]) for i in idx]
    l_k = [jnp.where(strict, g_[:, :hc], 0.0).astype(BF16) for g_ in gram_n]
    l_b = [jnp.where(strict, g_[:, hc:], 0.0) for g_ in gram_n]
    m_k = [jnp.where(incl, g_[:, :hc], 0.0).astype(BF16) for g_ in gram_r]
    m_b = [jnp.where(incl, g_[:, hc:], 0.0).astype(BF16) for g_ in gram_r]
    inv_t = [eye - x for x in l_b]
    pw = [_dot(x.astype(BF16), x.astype(BF16)) for x in l_b]
    n = 2
    while n < C:
        inv_t = [t_ + _dot(t_.astype(BF16), p_.astype(BF16)) for t_, p_ in zip(inv_t, pw)]
        n *= 2
        if n < C:
            pw = [_dot(p_.astype(BF16), p_.astype(BF16)) for p_ in pw]
    lkv = [_dot(l_k[i], v_st[i]) for i in idx]
    mkv = [_dot(m_k[i], v_st[i]) for i in idx]
    tw = [_dot(inv_t[i].astype(BF16), jnp.concatenate([n_st[i], lkv[i].astype(BF16)], axis=1)) for i in idx]
    mw = [_dot(m_b[i], tw[i].astype(BF16)) for i in idx]
    wnr = [jnp.concatenate([tw[i][:, :LANES], r_st[i].astype(F32) - mw[i][:, :LANES]], axis=0).astype(BF16)
           for i in idx]
    u_a = [fold(tw[i][:, LANES:]) for i in idx]
    y_a = [fold(mkv[i] - mw[i][:, LANES:]) for i in idx]
    st = [st_ref[p] for p in range(n_pairs)]
    ys = [[None] * n_pairs for _ in range(n_ch)]
    for i, (c, p) in enumerate(units):
        x = _dot_nt(wnr[i], st[p].astype(BF16))
        u = fold(x[:hc]) + u_a[i]
        ys[c][p] = fold(x[hc:]) + y_a[i]
        c_last = cs[i][C - 1:C, :]
        dec = jnp.exp(c_last - cs[i])
        vu = jnp.concatenate([sl(v, (c, p)), -u], axis=0).astype(BF16)
        kb_end = jnp.concatenate([sl(k, (c, p)) * dec, sl(bb, (c, p)) * dec], axis=0).astype(BF16)
        st[p] = st[p] * jnp.exp(c_last) + jnp.where(same_head_p, _dot_tn(vu, kb_end), 0.0)
    for p in range(n_pairs):
        st_ref[p] = st[p]
    y = jnp.concatenate([jnp.concatenate(row_, axis=1) for row_ in ys], axis=0)

    mean = _sum_heads(y, e_bf) * (1.0 / HEAD_DIM)
    yc = y - mean
    var = _sum_heads(yc * yc, e_bf) * (1.0 / HEAD_DIM)
    yn = yc * lax.rsqrt(var + RWKV_GN_EPS) * lg_ref[...] + lb_ref[...]
    bonus = _sum_heads(r * k * rk_ref[...], e_bf) * v
    y_ref[0] = (yn + bonus) * g


def _rwkv(seg_d, p, v_first, tr=512):
    B, S, _ = seg_d.shape
    tr = min(S, tr)
    W = MIX_W
    first = v_first is None
    row = lambda a: a.reshape(1, -1)
    full = lambda a: pl.BlockSpec(a.shape, lambda b, t: (0,) * a.ndim)
    tile = lambda w: pl.BlockSpec((1, tr, w), lambda b, t: (b, t, 0))
    params = [row(p['mu']), row(p['w0']), p['w2'].astype(BF16), row(p['a0']), p['a2'].astype(BF16),
              p['g2'].astype(BF16), row(p['k_k']), row(p['k_a']), row(p['r_k']), row(p['lnx_g']), row(p['lnx_b'])]
    args = [seg_d]
    in_specs = [tile(SEG_D)]
    if not first:
        args.append(v_first)
        in_specs.append(tile(W))
        params += [row(p['v0']), p['v1'].astype(BF16), p['v2'].astype(BF16)]
    args += params
    in_specs += [full(a) for a in params]
    n_out = 2 if first else 1
    outs = pl.pallas_call(
        functools.partial(_rwkv_body, tr=tr, first_layer=first),
        grid=(B, S // tr),
        in_specs=in_specs,
        out_specs=[tile(W)] * n_out,
        out_shape=[jax.ShapeDtypeStruct((B, S, W), F32)] * n_out,
        scratch_shapes=[pltpu.VMEM((8, SEG_D), F32), pltpu.VMEM((W // LANES, LANES, LANES), F32)],
        compiler_params=_cparams(2),
        name="rwkv7",
    )(*args)
    return (outs[0], outs[1]) if first else (outs[0], v_first)


def _merge_body(o0, l0, o1, l1, o2, l2, yb, yc, yd, x_ref, ng_ref, wg, pa, pb, pc, pd, wo, out_ref):
    x = x_ref[...]
    D = x.shape[-1]
    hn = _rms(x, ng_ref[...]).astype(BF16)
    wide = lambda ref: jnp.concatenate([ref[0], ref[1]], axis=1)
    la, lb, lc = wide(l0), wide(l1), wide(l2)
    m = jnp.maximum(jnp.maximum(la, lb), lc)
    e0, e1, e2 = jnp.exp(la - m), jnp.exp(lb - m), jnp.exp(lc - m)
    inv = 1.0 / (e0 + e1 + e2)
    y_a = (e0 * inv) * wide(o0) + (e1 * inv) * wide(o1) + (e2 * inv) * wide(o2)
    merged = jnp.zeros_like(x)
    for j, (y, p) in enumerate(((y_a, pa), (yb[...], pb), (yc[...], pc), (yd[...], pd))):
        gate = _sigmoid(_dot(hn, wg[:, j * D:(j + 1) * D]))
        merged = merged + gate * _dot(y.astype(BF16), p[...])
    out_ref[...] = x + _dot(merged.astype(BF16), wo[...])


def _merge(dil, y_b, y_c, y_d, x2, norm_g, w_gate, p_a, p_b, p_c, p_d, w_out):
    T, D = x2.shape
    tm = min(T, 512)
    rows = lambda w: pl.BlockSpec((tm, w), lambda i: (i, 0))
    full = lambda a: pl.BlockSpec(a.shape, lambda i: (0, 0))
    flat = lambda a: a.reshape(T, a.shape[-1])
    acts = []
    for o, lse in dil:
        acts += [o, lse]
    acts += [flat(y_b), flat(y_c), flat(y_d)]
    weights = [norm_g.reshape(1, D), w_gate] + [w.astype(BF16) for w in (p_a, p_b, p_c, p_d, w_out)]
    slabs = pl.BlockSpec((MIX_W // LANES, tm, LANES), lambda i: (0, i, 0))
    in_specs = ([slabs] * 6 + [rows(DIFF_W), rows(MIX_W), rows(MIX_W), rows(D)] + [full(w) for w in weights])
    return pl.pallas_call(
        _merge_body,
        grid=(T // tm,),
        in_specs=in_specs,
        out_specs=rows(D),
        out_shape=jax.ShapeDtypeStruct((T, D), F32),
        compiler_params=_cparams(1),
        name="gated_merge",
    )(*acts, x2, *weights)


def _mem_body(x_ref, g_ref, wq_ref, kv_ref, wo_ref, out_ref, *, n_heads):
    x = x_ref[0]
    D = x.shape[-1]
    dm = D // n_heads
    q = _dot(_rms(x, g_ref[...]).astype(BF16), wq_ref[...]) * (dm ** -0.5)
    kv = kv_ref[0].astype(BF16)
    outs = []
    for h in range(n_heads):
        s = _dot_nt(q[:, h * dm:(h + 1) * dm].astype(BF16), kv[:, h * dm:(h + 1) * dm])
        p = jnp.exp(s - jnp.max(s, axis=-1, keepdims=True))
        p = p / jnp.sum(p, axis=-1, keepdims=True)
        outs.append(_dot(p.astype(BF16), kv[:, D + h * dm:D + (h + 1) * dm]))
    o = jnp.concatenate(outs, axis=-1)
    out_ref[0] = x + _dot(o.astype(BF16), wo_ref[...])


def _mem_attention(x, g, w_q, kv, w_o, n_heads=4):
    B, S, D = x.shape
    M = kv.shape[1]
    tm = min(S, 512)
    full = lambda a: pl.BlockSpec(a.shape, lambda b, i: (0, 0))
    wq, wo = w_q.astype(BF16), w_o.astype(BF16)
    g = g.reshape(1, D)
    return pl.pallas_call(
        functools.partial(_mem_body, n_heads=n_heads),
        grid=(B, S // tm),
        in_specs=[pl.BlockSpec((1, tm, D), lambda b, i: (b, i, 0)), full(g), full(wq),
                  pl.BlockSpec((1, M, 2 * D), lambda b, i: (b, 0, 0)), full(wo)],
        out_specs=pl.BlockSpec((1, tm, D), lambda b, i: (b, i, 0)),
        out_shape=jax.ShapeDtypeStruct((B, S, D), F32),
        compiler_params=_cparams(2),
        name="mem_attention",
    )(x, g, wq, kv, wo)


def _ffn_body(x_ref, halo_ref, g_ref, wg_ref, wv_ref, cwg_ref, cwv_ref, cbg_ref, cbv_ref, wo_ref, fg_ref,
              out_ref, hn_ref, hh_ref, acc_ref, *, final_norm):
    i = pl.program_id(1)
    c = pl.program_id(2)
    tm = x_ref.shape[1]

    @pl.when(c == 0)
    def _():
        hn_ref[...] = _rms(x_ref[0], g_ref[...]).astype(BF16)
        hh_ref[...] = _rms(halo_ref[0], g_ref[...]).astype(BF16)
        acc_ref[...] = jnp.zeros_like(acc_ref)

    live = jnp.where(i > 0, 1.0, 0.0)
    row = lax.broadcasted_iota(jnp.int32, (tm, 1), 0)

    def conv(w_ref, cw_ref, cb_ref):
        u = _dot(hn_ref[...], w_ref[...])
        uh = _dot(hh_ref[...], w_ref[...]) * live
        u1 = jnp.where(row == 0, uh[7:8], pltpu.roll(u, 1, 0))
        u2 = jnp.where(row == 0, uh[6:7], jnp.where(row == 1, uh[7:8], pltpu.roll(u, 2, 0)))
        cw = cw_ref[...]
        return cb_ref[...] + u2 * cw[0:1] + u1 * cw[1:2] + u * cw[2:3]

    gate = conv(wg_ref, cwg_ref, cbg_ref)
    val = conv(wv_ref, cwv_ref, cbv_ref)
    act = (gate * _sigmoid(gate) * val).astype(BF16)
    acc_ref[...] += _dot(act, wo_ref[...])

    @pl.when(c == pl.num_programs(2) - 1)
    def _():
        y = x_ref[0] + acc_ref[...]
        out_ref[0] = _rms(y, fg_ref[...]) if final_norm else y


def _ffn(x, g, w_in, conv_w, conv_b, w_out, final_g):
    B, S, D = x.shape
    d_ff = w_out.shape[0]
    tm = min(S, 1024)
    fc = d_ff
    nf = d_ff // fc
    w_in, w_out = w_in.astype(BF16), w_out.astype(BF16)
    conv_b = conv_b.reshape(1, 2 * d_ff)
    vec = pl.BlockSpec((1, D), lambda b, i, c: (0, 0))
    fg = (final_g if final_g is not None else g).reshape(1, D)
    return pl.pallas_call(
        functools.partial(_ffn_body, final_norm=final_g is not None),
        grid=(B, S // tm, nf),
        in_specs=[pl.BlockSpec((1, tm, D), lambda b, i, c: (b, i, 0)),
                  pl.BlockSpec((1, 8, D), lambda b, i, c: (b, jnp.maximum(i * (tm // 8) - 1, 0), 0)),
                  vec,
                  pl.BlockSpec((D, fc), lambda b, i, c: (0, c)),
                  pl.BlockSpec((D, fc), lambda b, i, c: (0, nf + c)),
                  pl.BlockSpec((3, fc), lambda b, i, c: (0, c)),
                  pl.BlockSpec((3, fc), lambda b, i, c: (0, nf + c)),
                  pl.BlockSpec((1, fc), lambda b, i, c: (0, c)),
                  pl.BlockSpec((1, fc), lambda b, i, c: (0, nf + c)),
                  pl.BlockSpec((fc, D), lambda b, i, c: (c, 0)),
                  vec],
        out_specs=pl.BlockSpec((1, tm, D), lambda b, i, c: (b, i, 0)),
        out_shape=jax.ShapeDtypeStruct((B, S, D), F32),
        scratch_shapes=[pltpu.VMEM((tm, D), BF16), pltpu.VMEM((8, D), BF16), pltpu.VMEM((tm, D), F32)],
        compiler_params=_cparams(3),
        name="conv_ffn",
    )(x, x, g.reshape(1, D), w_in, w_in, conv_w, conv_w, conv_b, conv_b, w_out, fg)


def kernel(x, mem, positions, mix_norm_g, w_in, diff_lam, diff_norm_g, hgrn_lb_logits, hgrn_norm_g, rwkv_mu, rwkv_w0, rwkv_w2, rwkv_a0, rwkv_a2, rwkv_g2, rwkv_k_k, rwkv_k_a, rwkv_r_k, rwkv_lnx_g, rwkv_lnx_b, rwkv_v0, rwkv_v1, rwkv_v2, p_a, p_b, p_c, p_d, w_mix_out, mem_q_norm_g, mem_kv_norm_g, w_mem_q, w_mem_kv, w_mem_o, ffn_norm_g, w_ffn_in, ffn_conv_w, ffn_conv_b, w_ffn_out, final_norm_g):
    B, S, D = x.shape
    M = mem.shape[1]
    T = B * S
    depth = w_in.shape[0]
    assert S % (DIL_PATTERNS[-1][1] * DIL_BLOCK) == 0 and S % CHUNK == 0

    half = ROPE_DIMS // 2
    inv_freq = ROPE_THETA ** (-jnp.arange(half, dtype=F32) / half)
    d = jnp.arange(LANES) % HEAD_DIM
    invf_lanes = jnp.where(d < ROPE_DIMS, inv_freq[d % half], 0.0).reshape(1, LANES)
    rope = _rope_tables(positions.reshape(T, 1), invf_lanes)
    lb_all = jnp.cumsum(jax.nn.softmax(hgrn_lb_logits.astype(F32), axis=0), axis=0)
    lb_all = lb_all - lb_all[0:1]
    offs = (0, SEG_A, SEG_A + SEG_B, SEG_A + SEG_B + SEG_C, SEG_A + SEG_B + SEG_C + SEG_D, w_in.shape[2])
    mem2 = mem.reshape(B * M, D)
    qkv_slabs = lambda w: (True,) * (2 * w // LANES) + (False,) * (w // LANES)

    v_first = None
    for l in range(depth):
        lam_init = 0.8 - 0.6 * math.exp(-0.3 * l)
        w_l = w_in[l].astype(BF16)
        x2 = x.reshape(T, D)
        seg = lambda s: w_l[:, offs[s]:offs[s + 1]]
        seg_a = _norm_matmul(x2, mix_norm_g[l], seg(0), 3 * MIX_W, rope, qkv_slabs(MIX_W), slab_major=True)
        seg_b = _norm_matmul(x2, mix_norm_g[l], seg(1), SEG_B, rope, qkv_slabs(DIFF_W), BF16, slab_major=True)
        seg_d = _norm_matmul(x2, mix_norm_g[l], seg(3), SEG_D).reshape(B, S, SEG_D)
        dil = [_dilated_group(seg_a, B, g, dilation) for g, (_, dilation) in enumerate(DIL_PATTERNS)]
        y_b = _diff_attention(seg_b, B, diff_lam[l], diff_norm_g[l], lam_init)
        y_c = _hgrn(x, mix_norm_g[l], seg(2), lb_all[l], hgrn_norm_g[l])
        rp = dict(mu=rwkv_mu[l], w0=rwkv_w0[l], w2=rwkv_w2[l], a0=rwkv_a0[l], a2=rwkv_a2[l], g2=rwkv_g2[l],
                  k_k=rwkv_k_k[l], k_a=rwkv_k_a[l], r_k=rwkv_r_k[l], lnx_g=rwkv_lnx_g[l], lnx_b=rwkv_lnx_b[l])
        if l > 0:
            rp.update(v0=rwkv_v0[l - 1], v1=rwkv_v1[l - 1], v2=rwkv_v2[l - 1])
        y_d, v_first = _rwkv(seg_d, rp, v_first)
        x2 = _merge(dil, y_b, y_c, y_d, x2, mix_norm_g[l], seg(4), p_a[l], p_b[l], p_c[l], p_d[l], w_mix_out[l])
        kv = _norm_matmul(mem2, mem_kv_norm_g[l], w_mem_kv[l].astype(BF16), D).reshape(B, M, 2 * D)
        x = _mem_attention(x2.reshape(B, S, D), mem_q_norm_g[l], w_mem_q[l], kv, w_mem_o[l])
        x = _ffn(x, ffn_norm_g[l], w_ffn_in[l], ffn_conv_w[l], ffn_conv_b[l], w_ffn_out[l],
                 final_norm_g if l == depth - 1 else None)
    return x
```

```python
import functools
import math

import jax
import jax.numpy as jnp
from jax import lax
from jax.experimental import pallas as pl
from jax.experimental.pallas import tpu as pltpu

F32 = jnp.float32
BF16 = jnp.bfloat16

NORM_EPS = 1e-5
HEAD_DIM = 64
ROPE_THETA = 500000.0
ROPE_DIMS = HEAD_DIM // 4
DIL_PATTERNS = ((128, 1), (512, 4), (2048, 16))
DIL_BLOCK = 128
N_HEADS = 4
MIX_W = N_HEADS * HEAD_DIM
DIFF_W = 2 * MIX_W
SEG_A = 3 * 3 * MIX_W
SEG_B = 3 * DIFF_W
SEG_C = 4 * MIX_W
SEG_D = 3 * MIX_W + 64 + 64 + 128
RWKV_GN_EPS = 1e-5 * HEAD_DIM
CHUNK = 64
SUB = 16
LANES = 128
VMEM_LIMIT = 56 * 1024 * 1024
LOG2E = math.log2(math.e)


def _cparams(n_axes):
    return pltpu.CompilerParams(dimension_semantics=("arbitrary",) * n_axes,
                                vmem_limit_bytes=VMEM_LIMIT)


def _dot(a, b):
    return jnp.dot(a, b, preferred_element_type=F32)


def _dot_nt(a, b):
    return lax.dot_general(a, b, (((1,), (1,)), ((), ())), preferred_element_type=F32)


def _dot_tn(a, b):
    return lax.dot_general(a, b, (((0,), (0,)), ((), ())), preferred_element_type=F32)


def _sigmoid(x):
    return 1.0 / (1.0 + jnp.exp(-x))


def _softplus(x):
    return jnp.maximum(x, 0.0) + jnp.log1p(jnp.exp(-jnp.abs(x)))


def _rms(x, g):
    ms = jnp.mean(x * x, axis=-1, keepdims=True)
    return x * lax.rsqrt(ms + NORM_EPS) * g


def _head_sum_matrix(width):
    r = lax.broadcasted_iota(jnp.int32, (width, width), 0) // HEAD_DIM
    c = lax.broadcasted_iota(jnp.int32, (width, width), 1) // HEAD_DIM
    return r == c


def _cumsum_rows(x):
    n = x.shape[0]
    row = lax.broadcasted_iota(jnp.int32, x.shape, 0)
    d = 1
    while d < n:
        x = x + jnp.where(row >= d, pltpu.roll(x, d, 0), 0.0)
        d *= 2
    return x


def _sum_heads(x, e_bf):
    hi = x.astype(BF16)
    lo = (x - hi.astype(F32)).astype(BF16)
    return _dot(hi, e_bf) + _dot(lo, e_bf)


def _rope_tables_body(pos_ref, invf_ref, cos_ref, sin_ref):
    half = ROPE_DIMS // 2
    d = lax.broadcasted_iota(jnp.int32, (1, LANES), 1) % HEAD_DIM
    ang = pos_ref[...].astype(F32) * invf_ref[...]
    s = jnp.sin(ang)
    cos_ref[...] = jnp.cos(ang)
    sin_ref[...] = jnp.where(d < half, -s, s)


def _rope_tables(positions, invf_lanes):
    T = positions.shape[0]
    tm = min(T, 2048)
    tab = pl.BlockSpec((tm, LANES), lambda i: (i, 0))
    return pl.pallas_call(
        _rope_tables_body,
        grid=(T // tm,),
        in_specs=[pl.BlockSpec((tm, 1), lambda i: (i, 0)), pl.BlockSpec((1, LANES), lambda i: (0, 0))],
        out_specs=[tab, tab],
        out_shape=[jax.ShapeDtypeStruct((T, LANES), F32)] * 2,
        compiler_params=_cparams(1),
        name="rope_tables",
    )(positions, invf_lanes)


def _norm_matmul_body(*refs, rope_slabs, slab_major):
    if rope_slabs:
        x_ref, g_ref, w_ref, cos_ref, sin_ref, o_ref, hn_ref = refs
    else:
        x_ref, g_ref, w_ref, o_ref, hn_ref = refs
    j = pl.program_id(1)
    half = ROPE_DIMS // 2
    d = lax.broadcasted_iota(jnp.int32, (1, LANES), 1) % HEAD_DIM

    @pl.when(j == 0)
    def _():
        hn_ref[...] = _rms(x_ref[...], g_ref[...]).astype(BF16)

    acc = _dot(hn_ref[...], w_ref[...])
    if not rope_slabs:
        o_ref[...] = acc.astype(o_ref.dtype)
        return
    for c, roped in enumerate(rope_slabs):
        t = acc[:, c * LANES:(c + 1) * LANES]
        if roped:
            partner = jnp.where(d < half, pltpu.roll(t, LANES - half, 1), pltpu.roll(t, half, 1))
            t = t * cos_ref[...] + partner * sin_ref[...]
        if slab_major:
            o_ref[c] = t.astype(o_ref.dtype)
        else:
            o_ref[:, c * LANES:(c + 1) * LANES] = t.astype(o_ref.dtype)


def _norm_matmul(x2, g, w_bf, tn, rope=None, rope_slabs=(), out_dtype=F32, slab_major=False):
    T, D = x2.shape
    N = w_bf.shape[1]
    tm = min(T, 1024)
    assert len(rope_slabs) in (0, tn // LANES)
    if slab_major:
        out_spec = pl.BlockSpec((tn // LANES, tm, LANES), lambda i, j: (j, i, 0))
        out_shape = jax.ShapeDtypeStruct((N // LANES, T, LANES), out_dtype)
    else:
        out_spec = pl.BlockSpec((tm, tn), lambda i, j: (i, j))
        out_shape = jax.ShapeDtypeStruct((T, N), out_dtype)
    in_specs = [pl.BlockSpec((tm, D), lambda i, j: (i, 0)),
                pl.BlockSpec((1, D), lambda i, j: (0, 0)),
                pl.BlockSpec((D, tn), lambda i, j: (0, j))]
    args = [x2, g.reshape(1, D), w_bf]
    scratch = [pltpu.VMEM((tm, D), BF16)]
    if rope_slabs:
        in_specs += [pl.BlockSpec((tm, LANES), lambda i, j: (i, 0))] * 2
        args += list(rope)
    return pl.pallas_call(
        functools.partial(_norm_matmul_body, rope_slabs=tuple(rope_slabs), slab_major=slab_major),
        grid=(T // tm, N // tn),
        in_specs=in_specs,
        out_specs=out_spec,
        out_shape=out_shape,
        scratch_shapes=scratch,
        compiler_params=_cparams(2),
        name="norm_matmul_rope" if rope_slabs else "norm_matmul",
    )(*args)


def _dilated_body(q_ref, kc_ref, kp_ref, vc_ref, vp_ref, o_ref, lse_ref, *, dilation, nbs, width):
    n = pl.program_id(2)
    blk = DIL_BLOCK
    unit = blk * dilation
    n_lane_heads = LANES // HEAD_DIM
    lane_head = lax.broadcasted_iota(jnp.int32, (1, LANES), 1) // HEAD_DIM
    qi = lax.broadcasted_iota(jnp.int32, (blk, blk), 0)
    kj = lax.broadcasted_iota(jnp.int32, (blk, blk), 1)
    cur_ok = kj <= qi
    first_prev_ok = (kj - qi) >= jnp.where(n > 0, 0, 2 * blk)
    later_prev_ok = kj >= qi
    ones = jnp.ones((blk, LANES), BF16)

    def rows_of(j, r):
        start = j * unit + r
        return pl.ds(start, blk) if dilation == 1 else pl.ds(start, blk, stride=dilation)

    def process(units):
        loaded = []
        for j, r in units:
            rows = rows_of(j, r)
            prows = rows_of(max(j - 1, 0), r)
            kpr, vpr = (kp_ref, vp_ref) if j == 0 else (kc_ref, vc_ref)
            loaded.append((rows, first_prev_ok if j == 0 else later_prev_ok,
                           q_ref[0, rows, :] * (HEAD_DIM ** -0.5),
                           kc_ref[0, rows, :].astype(BF16), kpr[0, prows, :].astype(BF16),
                           jnp.concatenate([vc_ref[0, rows, :].astype(BF16), ones], axis=1),
                           jnp.concatenate([vpr[0, prows, :].astype(BF16), ones], axis=1)))
        heads = [(u, h) for u in range(len(units)) for h in range(n_lane_heads)]
        scores = []
        for u, h in heads:
            _, prev_ok, q, kc, kp, _, _ = loaded[u]
            qh = jnp.where(lane_head == h, q, 0.0).astype(BF16)
            scores.append((jnp.where(cur_ok, _dot_nt(qh, kc), -jnp.inf), jnp.where(prev_ok, _dot_nt(qh, kp), -jnp.inf)))
        probs = []
        for sc, sp in scores:
            m = jnp.max(jnp.maximum(sc, sp), axis=-1, keepdims=True)
            probs.append((m, jnp.exp(sc - m).astype(BF16), jnp.exp(sp - m).astype(BF16)))
        outs = []
        for (u, h), (m, pc, pp) in zip(heads, probs):
            ext = _dot(pc, loaded[u][5]) + _dot(pp, loaded[u][6])
            den = ext[:, LANES:]
            outs.append((ext[:, :LANES] / den, m + jnp.log(den)))
        for u in range(len(units)):
            o_acc = jnp.zeros((blk, LANES), F32)
            lse_acc = jnp.zeros((blk, LANES), F32)
            for h in range(n_lane_heads):
                oh, lse = outs[u * n_lane_heads + h]
                o_acc = jnp.where(lane_head == h, oh, o_acc)
                lse_acc = jnp.where(lane_head == h, lse, lse_acc)
            o_ref[0, loaded[u][0], :] = o_acc
            lse_ref[0, loaded[u][0], :] = lse_acc

    if dilation == 1:
        for j in range(0, nbs, width):
            process([(j + w, 0) for w in range(width)])
    else:
        step = dilation // width
        for j in range(nbs):
            def body(r, _, j=j):
                process([(j, r + w * step) for w in range(width)])
                return 0
            lax.fori_loop(0, step, body, 0)


def _dilated_group(seg_a, B, g, dilation):
    _, T, _ = seg_a.shape
    S = T // B
    unit = DIL_BLOCK * dilation
    width = 8 if dilation == 1 else 4
    nbs = max(1, width * DIL_BLOCK // unit)
    tr = unit * nbs
    slab = 2 * (g * 3)

    def cur(which):
        return pl.BlockSpec((1, tr, LANES), lambda b, p, n: (slab + 2 * which + p, b * (S // tr) + n, 0))

    def prev(which):
        return pl.BlockSpec((1, unit, LANES),
                            lambda b, p, n: (slab + 2 * which + p, b * (S // unit) + jnp.maximum(n * nbs - 1, 0), 0))

    out_spec = pl.BlockSpec((1, tr, LANES), lambda b, p, n: (p, b * (S // tr) + n, 0))
    return pl.pallas_call(
        functools.partial(_dilated_body, dilation=dilation, nbs=nbs, width=width),
        grid=(B, MIX_W // LANES, S // tr),
        in_specs=[cur(0), cur(1), prev(1), cur(2), prev(2)],
        out_specs=[out_spec, out_spec],
        out_shape=[jax.ShapeDtypeStruct((MIX_W // LANES, T, LANES), F32)] * 2,
        compiler_params=_cparams(3),
        name="dilated_attention",
    )(seg_a, seg_a, seg_a, seg_a, seg_a)


def _diff_body(q_ref, k_ref, v_ref, lam_ref, g_ref, o_ref, vt_ref, a_ref, sa_ref, sb_ref, *, tq, kc, lam_init, nh):
    qi = pl.program_id(2)
    n_kv = vt_ref.shape[1]
    hw = 2 * HEAD_DIM
    hs = range(nh)

    @pl.when(qi == 0)
    def _():
        for hh in hs:
            for c in range(n_kv):
                vt_ref[hh, c] = v_ref[hh, c * kc:(c + 1) * kc, :].astype(F32).T.astype(BF16)

    lane = lax.broadcasted_iota(jnp.int32, (1, hw), 1)
    qs = []
    for hh in hs:
        q = q_ref[hh].astype(F32) * (HEAD_DIM ** -0.5 * LOG2E)
        qs.append((jnp.where(lane < HEAD_DIM, q, 0.0).astype(BF16), jnp.where(lane >= HEAD_DIM, q, 0.0).astype(BF16)))
    key = lax.broadcasted_iota(jnp.int32, (kc, tq), 0)
    qry = lax.broadcasted_iota(jnp.int32, (kc, tq), 1)
    a_ref[...] = jnp.zeros_like(a_ref)
    chains = [(hh, j) for hh in hs for j in range(2)]
    n_c = len(chains)

    def scores(kb, s_ref):
        rows = pl.ds(pl.multiple_of(kb * kc, kc), kc)
        for hh, j in chains:
            s_ref[hh, j] = _dot_nt(k_ref[hh, rows, :], qs[hh][j])

    def softmax_pv(kb, s_ref, carry, masked):
        s = [s_ref[hh, j] for hh, j in chains]
        if masked:
            s = [jnp.where(key <= qry, x, -jnp.inf) for x in s]
        m_old = [carry[2 * i] for i in range(n_c)]
        l_old = [carry[2 * i + 1] for i in range(n_c)]
        m_new = [jnp.maximum(m_old[i], jnp.max(s[i], axis=0, keepdims=True)) for i in range(n_c)]
        p = [jnp.exp2(s[i] - m_new[i]) for i in range(n_c)]
        al = [jnp.exp2(m_old[i] - m_new[i]) for i in range(n_c)]
        pv = [_dot(vt_ref[chains[i][0], kb], p[i].astype(BF16)) for i in range(n_c)]
        out = []
        for i, (hh, j) in enumerate(chains):
            l_new = l_old[i] * al[i] + jnp.sum(p[i].reshape(kc // 8, 8, tq), axis=0)
            a_ref[hh, j] = a_ref[hh, j] * al[i] + pv[i]
            out += [m_new[i], l_new]
        return tuple(out)

    def pair(i, carry):
        scores(2 * i + 1, sb_ref)
        carry = softmax_pv(2 * i, sa_ref, carry, False)
        scores(2 * i + 2, sa_ref)
        return softmax_pv(2 * i + 1, sb_ref, carry, False)

    def finish(carry):
        lv = lam_ref[...]
        lam = (jnp.exp(jnp.sum(lv[0:1] * lv[1:2], axis=-1, keepdims=True))
               - jnp.exp(jnp.sum(lv[2:3] * lv[3:4], axis=-1, keepdims=True)) + lam_init)
        for hh in hs:
            l1 = carry[4 * hh + 1]
            l2 = carry[4 * hh + 3]
            o_t = (a_ref[hh, 0] / jnp.sum(l1, axis=0, keepdims=True)
                   - lam * (a_ref[hh, 1] / jnp.sum(l2, axis=0, keepdims=True)))
            o_ref[0, :, hh * hw:(hh + 1) * hw] = _rms(o_t.T, g_ref[...]) * (1.0 - lam_init)

    neg = jnp.full((1, tq), -jnp.inf, F32)
    zero = jnp.zeros((8, tq), F32)
    init = (neg, zero) * n_c
    scores(0, sa_ref)

    @pl.when(qi % 2 == 0)
    def _():
        carry = lax.fori_loop(0, qi // 2, pair, init)
        finish(softmax_pv(qi, sa_ref, carry, True))

    @pl.when(qi % 2 == 1)
    def _():
        carry = lax.fori_loop(0, qi // 2, pair, init)
        scores(qi, sb_ref)
        carry = softmax_pv(qi - 1, sa_ref, carry, False)
        finish(softmax_pv(qi, sb_ref, carry, True))


def _diff_attention(seg_b, B, lam_vecs, norm_g, lam_init, tq=512, nh=2):
    _, T, hw = seg_b.shape
    S = T // B
    tq = min(S, tq)
    kc = tq
    ng = N_HEADS // nh
    return pl.pallas_call(
        functools.partial(_diff_body, tq=tq, kc=kc, lam_init=lam_init, nh=nh),
        grid=(B, ng, S // tq),
        in_specs=[pl.BlockSpec((nh, tq, hw), lambda b, h, i: (h, b * (S // tq) + i, 0)),
                  pl.BlockSpec((nh, S, hw), lambda b, h, i: (ng + h, b, 0)),
                  pl.BlockSpec((nh, S, hw), lambda b, h, i: (2 * ng + h, b, 0)),
                  pl.BlockSpec((4, HEAD_DIM), lambda b, h, i: (0, 0)),
                  pl.BlockSpec((1, hw), lambda b, h, i: (0, 0))],
        out_specs=pl.BlockSpec((1, tq, nh * hw), lambda b, h, i: (b, i, h)),
        out_shape=jax.ShapeDtypeStruct((B, S, DIFF_W), F32),
        scratch_shapes=[pltpu.VMEM((nh, S // kc, hw, kc), BF16), pltpu.VMEM((nh, 2, hw, tq), F32),
                        pltpu.VMEM((nh, 2, kc, tq), F32), pltpu.VMEM((nh, 2, kc, tq), F32)],
        compiler_params=_cparams(3),
        name="diff_attention",
    )(seg_b, seg_b, seg_b, lam_vecs, norm_g.reshape(1, hw))


def _hgrn_body(x_ref, xg_ref, w_ref, la_ref, lc_ref, oml_ref, ng_ref, o_ref, st_ref, seg_ref, *, tr, unroll):
    t = pl.program_id(1)
    c = SUB
    W = MIX_W

    @pl.when(t == 0)
    def _():
        st_ref[...] = jnp.zeros_like(st_ref)

    seg_ref[...] = _dot(_rms(x_ref[0], xg_ref[...]).astype(BF16), w_ref[...])

    e_bf = jnp.where(_head_sum_matrix(W), 1.0, 0.0).astype(BF16)
    row = lax.broadcasted_iota(jnp.int32, (c, W), 0)
    head = lax.broadcasted_iota(jnp.int32, (1, W), 1) // HEAD_DIM

    def stack(x):
        return jnp.concatenate([jnp.where(head == h, x, 0.0) for h in range(N_HEADS)], axis=0).astype(BF16)

    def group(j, _):
        base = j * (c * unroll)
        rows = [pl.ds(pl.multiple_of(base + u * c, c), c) for u in range(unroll)]
        us = range(unroll)
        qr = [seg_ref[r, 0:W] for r in rows]
        f = [seg_ref[r, W:2 * W] for r in rows]
        iv = [seg_ref[r, 2 * W:3 * W] for r in rows]
        gr = [seg_ref[r, 3 * W:4 * W] for r in rows]
        q = [x * _sigmoid(x) for x in qr]
        log_sig = [jnp.minimum(x, 0.0) - jnp.log1p(jnp.exp(-jnp.abs(x))) for x in f]
        x1 = la_ref[...]
        x2 = [lc_ref[...] + x for x in log_sig]
        log_f = [jnp.maximum(x1, x) + jnp.log1p(jnp.exp(-jnp.abs(x1 - x))) for x in x2]
        k = [oml_ref[...] * _sigmoid(-x) for x in f]
        b = [_cumsum_rows(x) for x in log_f]
        qe_st = [stack(q[u] * jnp.exp(b[u])) for u in us]
        b2 = [x * LOG2E for x in b]
        pmat = []
        for u in us:
            ps = []
            for s in range(c):
                e = jnp.exp2(jnp.where(row >= s, b2[u] - b2[u][s:s + 1], -jnp.inf))
                ps.append((q[u] * e * k[u][s:s + 1]).astype(BF16))
            pmat.append(jnp.concatenate(ps, axis=0))
        a = [_dot(p, e_bf) for p in pmat]
        o_intra = []
        for u in us:
            o = a[u][0:c] * iv[u][0:1]
            for s in range(1, c):
                o = o + a[u][s * c:(s + 1) * c] * iv[u][s:s + 1]
            o_intra.append(o)
        b_last = [x[c - 1:c] for x in b]
        kt_st = [stack(k[u] * jnp.exp(b_last[u] - b[u])) for u in us]
        i_heads = [jnp.concatenate([x[:, h * HEAD_DIM:(h + 1) * HEAD_DIM] for h in range(N_HEADS)], axis=0).astype(BF16)
                   for x in iv]
        upd = [_dot_tn(i_heads[u], kt_st[u]) for u in us]
        dec = [jnp.exp(x) for x in b_last]
        st = st_ref[...]
        o_heads = []
        for u in us:
            o_heads.append(_dot_nt(qe_st[u], st.astype(BF16)))
            st = st * dec[u] + upd[u]
        st_ref[...] = st
        for u in us:
            o = o_intra[u] + jnp.concatenate([o_heads[u][h * c:(h + 1) * c] for h in range(N_HEADS)], axis=1)
            ms = _sum_heads(o * o, e_bf) * (1.0 / HEAD_DIM)
            o_ref[0, rows[u], :] = o * lax.rsqrt(ms + NORM_EPS) * ng_ref[...] * (gr[u] * _sigmoid(gr[u]))
        return 0

    lax.fori_loop(0, tr // (c * unroll), group, 0)


def _hgrn(x, mix_g, w_c, lb, norm_g, tr=256, unroll=16):
    B, S, D = x.shape
    tr = min(S, tr)
    W = MIX_W
    vec = pl.BlockSpec((1, W), lambda b, t: (0, 0))
    lb = lb.reshape(1, W)
    return pl.pallas_call(
        functools.partial(_hgrn_body, tr=tr, unroll=unroll),
        grid=(B, S // tr),
        in_specs=[pl.BlockSpec((1, tr, D), lambda b, t: (b, t, 0)),
                  pl.BlockSpec((1, D), lambda b, t: (0, 0)),
                  pl.BlockSpec((D, SEG_C), lambda b, t: (0, 0)),
                  vec, vec, vec, vec],
        out_specs=pl.BlockSpec((1, tr, W), lambda b, t: (b, t, 0)),
        out_shape=jax.ShapeDtypeStruct((B, S, W), F32),
        scratch_shapes=[pltpu.VMEM((HEAD_DIM, W), F32), pltpu.VMEM((tr, SEG_C), F32)],
        compiler_params=_cparams(2),
        name="hgrn2",
    )(x, mix_g.reshape(1, D), w_c, jnp.log(lb), jnp.log1p(-lb), 1.0 - lb,
      jnp.tile(norm_g.reshape(1, HEAD_DIM), (1, N_HEADS)))


def _rwkv_body(*refs, tr, first_layer):
    if first_layer:
        (x_ref, mu_ref, w0_ref, w2_ref, a0_ref, a2_ref, g2_ref, kk_ref, ka_ref, rk_ref, lg_ref, lb_ref,
         y_ref, vf_out_ref,
         carry_ref, st_ref) = refs
    else:
        (x_ref, vf_ref, mu_ref, w0_ref, w2_ref, a0_ref, a2_ref, g2_ref, kk_ref, ka_ref, rk_ref, lg_ref, lb_ref,
         v0_ref, v1_ref, v2_ref,
         y_ref,
         carry_ref, st_ref) = refs
    t = pl.program_id(1)
    C = CHUNK
    W = MIX_W

    @pl.when(t == 0)
    def _():
        st_ref[...] = jnp.zeros_like(st_ref)
        carry_ref[...] = jnp.zeros_like(carry_ref)

    e_bf = jnp.where(_head_sum_matrix(W), 1.0, 0.0).astype(BF16)

    xs = x_ref[0]
    rowi = lax.broadcasted_iota(jnp.int32, xs.shape, 0)
    prev = jnp.where(rowi == 0, carry_ref[0:1, :], pltpu.roll(xs, 1, 0))
    carry_ref[0:1, :] = xs[tr - 1:tr, :]
    xm = xs + (prev - xs) * mu_ref[...]
    r = xm[:, 0:W]
    k = xm[:, W:2 * W]
    v = xm[:, 2 * W:3 * W]
    w_low = xm[:, 3 * W:3 * W + 64]
    a_low = xm[:, 3 * W + 64:3 * W + 128]
    g_low = xm[:, 3 * W + 128:3 * W + 256]
    wlog = -_softplus(-(w0_ref[...] + _dot(jnp.tanh(w_low).astype(BF16), w2_ref[...]))) - 0.5
    a = _sigmoid(a0_ref[...] + _dot(a_low.astype(BF16), a2_ref[...]))
    g = _dot(_sigmoid(g_low).astype(BF16), g2_ref[...])
    kkr = k * kk_ref[...]
    kn = kkr / jnp.maximum(jnp.sqrt(_sum_heads(kkr * kkr, e_bf)), 1e-12)
    k = k * (1.0 + (a - 1.0) * ka_ref[...])
    if first_layer:
        vf_out_ref[0] = v
    else:
        mix = _dot(_dot(v.astype(BF16), v1_ref[...]).astype(BF16), v2_ref[...])
        v = v + (vf_ref[0] - v) * _sigmoid(v0_ref[...] + mix)

    HP = LANES // HEAD_DIM
    hc = HP * C
    n_pairs = W // LANES
    tt = lax.broadcasted_iota(jnp.int32, (hc, hc), 0) % C
    ss = lax.broadcasted_iota(jnp.int32, (hc, hc), 1) % C
    strict = ss < tt
    incl = ss <= tt
    eye = jnp.where(lax.broadcasted_iota(jnp.int32, (hc, hc), 0) == lax.broadcasted_iota(jnp.int32, (hc, hc), 1),
                    1.0, 0.0)
    same_head_p = _head_sum_matrix(LANES)
    lane_head = lax.broadcasted_iota(jnp.int32, (1, LANES), 1) // HEAD_DIM

    def stack(x):
        return jnp.concatenate([jnp.where(lane_head == j, x, 0.0) for j in range(HP)], axis=0).astype(BF16)

    def fold(x):
        return x[0:C] + x[C:2 * C]

    n_ch = tr // C
    units = [(c, p) for c in range(n_ch) for p in range(n_pairs)]
    sl = lambda x, u: x[u[0] * C:(u[0] + 1) * C, u[1] * LANES:(u[1] + 1) * LANES]
    idx = range(len(units))
    lw_all = -jnp.exp(wlog)
    bb = kn * a
    cs_c = [_cumsum_rows(lw_all[c * C:(c + 1) * C]) for c in range(n_ch)]
    cs = [cs_c[u[0]][:, u[1] * LANES:(u[1] + 1) * LANES] for u in units]
    r_st = [stack(sl(r, units[i]) * jnp.exp(cs[i])) for i in idx]
    n_st = [stack(sl(kn, units[i]) * jnp.exp(cs[i] - sl(lw_all, units[i]))) for i in idx]
    kb_st = [jnp.concatenate([stack(sl(k, units[i]) * jnp.exp(-cs[i])), stack(sl(bb, units[i]) * jnp.exp(-cs[i]))],
                             axis=0) for i in idx]
    v_st = [stack(sl(v, units[i])) for i in idx]
    gram_n = [_dot_nt(n_st[i], kb_st[i]) for i in idx]
    gram_r = [_dot_nt(r_st[i], kb_st[i]) for i in idx]
    l_k = [jnp.where(strict, g_[:, :hc], 0.0).astype(BF16) for g_ in gram_n]
    l_b = [jnp.where(strict, g_[:, hc:], 0.0) for g_ in gram_n]
    m_k = [jnp.where(incl, g_[:, :hc], 0.0).astype(BF16) for g_ in gram_r]
    m_b = [jnp.where(incl, g_[:, hc:], 0.0).astype(BF16) for g_ in gram_r]
    inv_t = [eye - x for x in l_b]
    pw = [_dot(x.astype(BF16), x.astype(BF16)) for x in l_b]
    n = 2
    while n < C:
        inv_t = [t_ + _dot(t_.astype(BF16), p_.astype(BF16)) for t_, p_ in zip(inv_t, pw)]
        n *= 2
        if n < C:
            pw = [_dot(p_.astype(BF16), p_.astype(BF16)) for p_ in pw]
    lkv = [_dot(l_k[i], v_st[i]) for i in idx]
    mkv = [_dot(m_k[i], v_st[i]) for i in idx]
    tw = [_dot(inv_t[i].astype(BF16), jnp.concatenate([n_st[i], lkv[i].astype(BF16)], axis=1)) for i in idx]
    mw = [_dot(m_b[i], tw[i].astype(BF16)) for i in idx]
    wnr = [jnp.concatenate([tw[i][:, :LANES], r_st[i].astype(F32) - mw[i][:, :LANES]], axis=0).astype(BF16)
           for i in idx]
    u_a = [fold(tw[i][:, LANES:]) for i in idx]
    y_a = [fold(mkv[i] - mw[i][:, LANES:]) for i in idx]
    st = [st_ref[p] for p in range(n_pairs)]
    ys = [[None] * n_pairs for _ in range(n_ch)]
    for i, (c, p) in enumerate(units):
        x = _dot_nt(wnr[i], st[p].astype(BF16))
        u = fold(x[:hc]) + u_a[i]
        ys[c][p] = fold(x[hc:]) + y_a[i]
        c_last = cs[i][C - 1:C, :]
        dec = jnp.exp(c_last - cs[i])
        vu = jnp.concatenate([sl(v, (c, p)), -u], axis=0).astype(BF16)
        kb_end = jnp.concatenate([sl(k, (c, p)) * dec, sl(bb, (c, p)) * dec], axis=0).astype(BF16)
        st[p] = st[p] * jnp.exp(c_last) + jnp.where(same_head_p, _dot_tn(vu, kb_end), 0.0)
    for p in range(n_pairs):
        st_ref[p] = st[p]
    y = jnp.concatenate([jnp.concatenate(row_, axis=1) for row_ in ys], axis=0)

    mean = _sum_heads(y, e_bf) * (1.0 / HEAD_DIM)
    yc = y - mean
    var = _sum_heads(yc * yc, e_bf) * (1.0 / HEAD_DIM)
    yn = yc * lax.rsqrt(var + RWKV_GN_EPS) * lg_ref[...] + lb_ref[...]
    bonus = _sum_heads(r * k * rk_ref[...], e_bf) * v
    y_ref[0] = (yn + bonus) * g


def _rwkv(seg_d, p, v_first, tr=512):
    B, S, _ = seg_d.shape
    tr = min(S, tr)
    W = MIX_W
    first = v_first is None
    row = lambda a: a.reshape(1, -1)
    full = lambda a: pl.BlockSpec(a.shape, lambda b, t: (0,) * a.ndim)
    tile = lambda w: pl.BlockSpec((1, tr, w), lambda b, t: (b, t, 0))
    params = [row(p['mu']), row(p['w0']), p['w2'].astype(BF16), row(p['a0']), p['a2'].astype(BF16),
              p['g2'].astype(BF16), row(p['k_k']), row(p['k_a']), row(p['r_k']), row(p['lnx_g']), row(p['lnx_b'])]
    args = [seg_d]
    in_specs = [tile(SEG_D)]
    if not first:
        args.append(v_first)
        in_specs.append(tile(W))
        params += [row(p['v0']), p['v1'].astype(BF16), p['v2'].astype(BF16)]
    args += params
    in_specs += [full(a) for a in params]
    n_out = 2 if first else 1
    outs = pl.pallas_call(
        functools.partial(_rwkv_body, tr=tr, first_layer=first),
        grid=(B, S // tr),
        in_specs=in_specs,
        out_specs=[tile(W)] * n_out,
        out_shape=[jax.ShapeDtypeStruct((B, S, W), F32)] * n_out,
        scratch_shapes=[pltpu.VMEM((8, SEG_D), F32), pltpu.VMEM((W // LANES, LANES, LANES), F32)],
        compiler_params=_cparams(2),
        name="rwkv7",
    )(*args)
    return (outs[0], outs[1]) if first else (outs[0], v_first)


def _merge_body(o0, l0, o1, l1, o2, l2, yb, yc, yd, x_ref, ng_ref, wg, pa, pb, pc, pd, wo, out_ref):
    x = x_ref[...]
    D = x.shape[-1]
    hn = _rms(x, ng_ref[...]).astype(BF16)
    wide = lambda ref: jnp.concatenate([ref[0], ref[1]], axis=1)
    la, lb, lc = wide(l0), wide(l1), wide(l2)
    m = jnp.maximum(jnp.maximum(la, lb), lc)
    e0, e1, e2 = jnp.exp(la - m), jnp.exp(lb - m), jnp.exp(lc - m)
    inv = 1.0 / (e0 + e1 + e2)
    y_a = (e0 * inv) * wide(o0) + (e1 * inv) * wide(o1) + (e2 * inv) * wide(o2)
    merged = jnp.zeros_like(x)
    for j, (y, p) in enumerate(((y_a, pa), (yb[...], pb), (yc[...], pc), (yd[...], pd))):
        gate = _sigmoid(_dot(hn, wg[:, j * D:(j + 1) * D]))
        merged = merged + gate * _dot(y.astype(BF16), p[...])
    out_ref[...] = x + _dot(merged.astype(BF16), wo[...])


def _merge(dil, y_b, y_c, y_d, x2, norm_g, w_gate, p_a, p_b, p_c, p_d, w_out):
    T, D = x2.shape
    tm = min(T, 512)
    rows = lambda w: pl.BlockSpec((tm, w), lambda i: (i, 0))
    full = lambda a: pl.BlockSpec(a.shape, lambda i: (0, 0))
    flat = lambda a: a.reshape(T, a.shape[-1])
    acts = []
    for o, lse in dil:
        acts += [o, lse]
    acts += [flat(y_b), flat(y_c), flat(y_d)]
    weights = [norm_g.reshape(1, D), w_gate] + [w.astype(BF16) for w in (p_a, p_b, p_c, p_d, w_out)]
    slabs = pl.BlockSpec((MIX_W // LANES, tm, LANES), lambda i: (0, i, 0))
    in_specs = ([slabs] * 6 + [rows(DIFF_W), rows(MIX_W), rows(MIX_W), rows(D)] + [full(w) for w in weights])
    return pl.pallas_call(
        _merge_body,
        grid=(T // tm,),
        in_specs=in_specs,
        out_specs=rows(D),
        out_shape=jax.ShapeDtypeStruct((T, D), F32),
        compiler_params=_cparams(1),
        name="gated_merge",
    )(*acts, x2, *weights)


def _mem_body(x_ref, g_ref, wq_ref, kv_ref, wo_ref, out_ref, *, n_heads):
    x = x_ref[0]
    D = x.shape[-1]
    dm = D // n_heads
    q = _dot(_rms(x, g_ref[...]).astype(BF16), wq_ref[...]) * (dm ** -0.5)
    kv = kv_ref[0].astype(BF16)
    outs = []
    for h in range(n_heads):
        s = _dot_nt(q[:, h * dm:(h + 1) * dm].astype(BF16), kv[:, h * dm:(h + 1) * dm])
        p = jnp.exp(s - jnp.max(s, axis=-1, keepdims=True))
        p = p / jnp.sum(p, axis=-1, keepdims=True)
        outs.append(_dot(p.astype(BF16), kv[:, D + h * dm:D + (h + 1) * dm]))
    o = jnp.concatenate(outs, axis=-1)
    out_ref[0] = x + _dot(o.astype(BF16), wo_ref[...])


def _mem_attention(x, g, w_q, kv, w_o, n_heads=4):
    B, S, D = x.shape
    M = kv.shape[1]
    tm = min(S, 512)
    full = lambda a: pl.BlockSpec(a.shape, lambda b, i: (0, 0))
    wq, wo = w_q.astype(BF16), w_o.astype(BF16)
    g = g.reshape(1, D)
    return pl.pallas_call(
        functools.partial(_mem_body, n_heads=n_heads),
        grid=(B, S // tm),
        in_specs=[pl.BlockSpec((1, tm, D), lambda b, i: (b, i, 0)), full(g), full(wq),
                  pl.BlockSpec((1, M, 2 * D), lambda b, i: (b, 0, 0)), full(wo)],
        out_specs=pl.BlockSpec((1, tm, D), lambda b, i: (b, i, 0)),
        out_shape=jax.ShapeDtypeStruct((B, S, D), F32),
        compiler_params=_cparams(2),
        name="mem_attention",
    )(x, g, wq, kv, wo)


def _ffn_body(x_ref, halo_ref, g_ref, wg_ref, wv_ref, cwg_ref, cwv_ref, cbg_ref, cbv_ref, wo_ref, fg_ref,
              out_ref, hn_ref, hh_ref, acc_ref, *, final_norm):
    i = pl.program_id(1)
    c = pl.program_id(2)
    tm = x_ref.shape[1]

    @pl.when(c == 0)
    def _():
        hn_ref[...] = _rms(x_ref[0], g_ref[...]).astype(BF16)
        hh_ref[...] = _rms(halo_ref[0], g_ref[...]).astype(BF16)
        acc_ref[...] = jnp.zeros_like(acc_ref)

    live = jnp.where(i > 0, 1.0, 0.0)
    row = lax.broadcasted_iota(jnp.int32, (tm, 1), 0)

    def conv(w_ref, cw_ref, cb_ref):
        u = _dot(hn_ref[...], w_ref[...])
        uh = _dot(hh_ref[...], w_ref[...]) * live
        u1 = jnp.where(row == 0, uh[7:8], pltpu.roll(u, 1, 0))
        u2 = jnp.where(row == 0, uh[6:7], jnp.where(row == 1, uh[7:8], pltpu.roll(u, 2, 0)))
        cw = cw_ref[...]
        return cb_ref[...] + u2 * cw[0:1] + u1 * cw[1:2] + u * cw[2:3]

    gate = conv(wg_ref, cwg_ref, cbg_ref)
    val = conv(wv_ref, cwv_ref, cbv_ref)
    act = (gate * _sigmoid(gate) * val).astype(BF16)
    acc_ref[...] += _dot(act, wo_ref[...])

    @pl.when(c == pl.num_programs(2) - 1)
    def _():
        y = x_ref[0] + acc_ref[...]
        out_ref[0] = _rms(y, fg_ref[...]) if final_norm else y


def _ffn(x, g, w_in, conv_w, conv_b, w_out, final_g):
    B, S, D = x.shape
    d_ff = w_out.shape[0]
    tm = min(S, 1024)
    fc = d_ff
    nf = d_ff // fc
    w_in, w_out = w_in.astype(BF16), w_out.astype(BF16)
    conv_b = conv_b.reshape(1, 2 * d_ff)
    vec = pl.BlockSpec((1, D), lambda b, i, c: (0, 0))
    fg = (final_g if final_g is not None else g).reshape(1, D)
    return pl.pallas_call(
        functools.partial(_ffn_body, final_norm=final_g is not None),
        grid=(B, S // tm, nf),
        in_specs=[pl.BlockSpec((1, tm, D), lambda b, i, c: (b, i, 0)),
                  pl.BlockSpec((1, 8, D), lambda b, i, c: (b, jnp.maximum(i * (tm // 8) - 1, 0), 0)),
                  vec,
                  pl.BlockSpec((D, fc), lambda b, i, c: (0, c)),
                  pl.BlockSpec((D, fc), lambda b, i, c: (0, nf + c)),
                  pl.BlockSpec((3, fc), lambda b, i, c: (0, c)),
                  pl.BlockSpec((3, fc), lambda b, i, c: (0, nf + c)),
                  pl.BlockSpec((1, fc), lambda b, i, c: (0, c)),
                  pl.BlockSpec((1, fc), lambda b, i, c: (0, nf + c)),
                  pl.BlockSpec((fc, D), lambda b, i, c: (c, 0)),
                  vec],
        out_specs=pl.BlockSpec((1, tm, D), lambda b, i, c: (b, i, 0)),
        out_shape=jax.ShapeDtypeStruct((B, S, D), F32),
        scratch_shapes=[pltpu.VMEM((tm, D), BF16), pltpu.VMEM((8, D), BF16), pltpu.VMEM((tm, D), F32)],
        compiler_params=_cparams(3),
        name="conv_ffn",
    )(x, x, g.reshape(1, D), w_in, w_in, conv_w, conv_w, conv_b, conv_b, w_out, fg)


def kernel(x, mem, positions, mix_norm_g, w_in, diff_lam, diff_norm_g, hgrn_lb_logits, hgrn_norm_g, rwkv_mu, rwkv_w0, rwkv_w2, rwkv_a0, rwkv_a2, rwkv_g2, rwkv_k_k, rwkv_k_a, rwkv_r_k, rwkv_lnx_g, rwkv_lnx_b, rwkv_v0, rwkv_v1, rwkv_v2, p_a, p_b, p_c, p_d, w_mix_out, mem_q_norm_g, mem_kv_norm_g, w_mem_q, w_mem_kv, w_mem_o, ffn_norm_g, w_ffn_in, ffn_conv_w, ffn_conv_b, w_ffn_out, final_norm_g):
    B, S, D = x.shape
    M = mem.shape[1]
    T = B * S
    depth = w_in.shape[0]
    assert S % (DIL_PATTERNS[-1][1] * DIL_BLOCK) == 0 and S % CHUNK == 0

    half = ROPE_DIMS // 2
    inv_freq = ROPE_THETA ** (-jnp.arange(half, dtype=F32) / half)
    d = jnp.arange(LANES) % HEAD_DIM
    invf_lanes = jnp.where(d < ROPE_DIMS, inv_freq[d % half], 0.0).reshape(1, LANES)
    rope = _rope_tables(positions.reshape(T, 1), invf_lanes)
    lb_all = jnp.cumsum(jax.nn.softmax(hgrn_lb_logits.astype(F32), axis=0), axis=0)
    lb_all = lb_all - lb_all[0:1]
    offs = (0, SEG_A, SEG_A + SEG_B, SEG_A + SEG_B + SEG_C, SEG_A + SEG_B + SEG_C + SEG_D, w_in.shape[2])
    mem2 = mem.reshape(B * M, D)
    qkv_slabs = lambda w: (True,) * (2 * w // LANES) + (False,) * (w // LANES)

    v_first = None
    for l in range(depth):
        lam_init = 0.8 - 0.6 * math.exp(-0.3 * l)
        w_l = w_in[l].astype(BF16)
        x2 = x.reshape(T, D)
        seg = lambda s: w_l[:, offs[s]:offs[s + 1]]
        seg_a = _norm_matmul(x2, mix_norm_g[l], seg(0), 3 * MIX_W, rope, qkv_slabs(MIX_W), slab_major=True)
        seg_b = _norm_matmul(x2, mix_norm_g[l], seg(1), SEG_B, rope, qkv_slabs(DIFF_W), BF16, slab_major=True)
        seg_d = _norm_matmul(x2, mix_norm_g[l], seg(3), SEG_D).reshape(B, S, SEG_D)
        dil = [_dilated_group(seg_a, B, g, dilation) for g, (_, dilation) in enumerate(DIL_PATTERNS)]
        y_b = _diff_attention(seg_b, B, diff_lam[l], diff_norm_g[l], lam_init)
        y_c = _hgrn(x, mix_norm_g[l], seg(2), lb_all[l], hgrn_norm_g[l])
        rp = dict(mu=rwkv_mu[l], w0=rwkv_w0[l], w2=rwkv_w2[l], a0=rwkv_a0[l], a2=rwkv_a2[l], g2=rwkv_g2[l],
                  k_k=rwkv_k_k[l], k_a=rwkv_k_a[l], r_k=rwkv_r_k[l], lnx_g=rwkv_lnx_g[l], lnx_b=rwkv_lnx_b[l])
        if l > 0:
            rp.update(v0=rwkv_v0[l - 1], v1=rwkv_v1[l - 1], v2=rwkv_v2[l - 1])
        y_d, v_first = _rwkv(seg_d, rp, v_first)
        x2 = _merge(dil, y_b, y_c, y_d, x2, mix_norm_g[l], seg(4), p_a[l], p_b[l], p_c[l], p_d[l], w_mix_out[l])
        kv = _norm_matmul(mem2, mem_kv_norm_g[l], w_mem_kv[l].astype(BF16), D).reshape(B, M, 2 * D)
        x = _mem_attention(x2.reshape(B, S, D), mem_q_norm_g[l], w_mem_q[l], kv, w_mem_o[l])
        x = _ffn(x, ffn_norm_g[l], w_ffn_in[l], ffn_conv_w[l], ffn_conv_b[l], w_ffn_out[l],
                 final_norm_g if l == depth - 1 else None)
    return x
```

```python
import functools
import math

import jax
import jax.numpy as jnp
from jax import lax
from jax.experimental import pallas as pl
from jax.experimental.pallas import tpu as pltpu

F32 = jnp.float32
BF16 = jnp.bfloat16

NORM_EPS = 1e-5
HEAD_DIM = 64
ROPE_THETA = 500000.0
ROPE_DIMS = HEAD_DIM // 4
DIL_PATTERNS = ((128, 1), (512, 4), (2048, 16))
DIL_BLOCK = 128
N_HEADS = 4
MIX_W = N_HEADS * HEAD_DIM
DIFF_W = 2 * MIX_W
SEG_A = 3 * 3 * MIX_W
SEG_B = 3 * DIFF_W
SEG_C = 4 * MIX_W
SEG_D = 3 * MIX_W + 64 + 64 + 128
RWKV_GN_EPS = 1e-5 * HEAD_DIM
CHUNK = 64
SUB = 16
LANES = 128
VMEM_LIMIT = 56 * 1024 * 1024
LOG2E = math.log2(math.e)


def _cparams(n_axes):
    return pltpu.CompilerParams(dimension_semantics=("arbitrary",) * n_axes,
                                vmem_limit_bytes=VMEM_LIMIT)


def _dot(a, b):
    return jnp.dot(a, b, preferred_element_type=F32)


def _dot_nt(a, b):
    return lax.dot_general(a, b, (((1,), (1,)), ((), ())), preferred_element_type=F32)


def _dot_tn(a, b):
    return lax.dot_general(a, b, (((0,), (0,)), ((), ())), preferred_element_type=F32)


def _sigmoid(x):
    return 1.0 / (1.0 + jnp.exp(-x))


def _softplus(x):
    return jnp.maximum(x, 0.0) + jnp.log1p(jnp.exp(-jnp.abs(x)))


def _rms(x, g):
    ms = jnp.mean(x * x, axis=-1, keepdims=True)
    return x * lax.rsqrt(ms + NORM_EPS) * g


def _head_sum_matrix(width):
    r = lax.broadcasted_iota(jnp.int32, (width, width), 0) // HEAD_DIM
    c = lax.broadcasted_iota(jnp.int32, (width, width), 1) // HEAD_DIM
    return r == c


def _cumsum_rows(x):
    n = x.shape[0]
    row = lax.broadcasted_iota(jnp.int32, x.shape, 0)
    d = 1
    while d < n:
        x = x + jnp.where(row >= d, pltpu.roll(x, d, 0), 0.0)
        d *= 2
    return x


def _sum_heads(x, e_bf):
    hi = x.astype(BF16)
    lo = (x - hi.astype(F32)).astype(BF16)
    return _dot(hi, e_bf) + _dot(lo, e_bf)


def _rope_tables_body(pos_ref, invf_ref, cos_ref, sin_ref):
    half = ROPE_DIMS // 2
    d = lax.broadcasted_iota(jnp.int32, (1, LANES), 1) % HEAD_DIM
    ang = pos_ref[...].astype(F32) * invf_ref[...]
    s = jnp.sin(ang)
    cos_ref[...] = jnp.cos(ang)
    sin_ref[...] = jnp.where(d < half, -s, s)


def _rope_tables(positions, invf_lanes):
    T = positions.shape[0]
    tm = min(T, 2048)
    tab = pl.BlockSpec((tm, LANES), lambda i: (i, 0))
    return pl.pallas_call(
        _rope_tables_body,
        grid=(T // tm,),
        in_specs=[pl.BlockSpec((tm, 1), lambda i: (i, 0)), pl.BlockSpec((1, LANES), lambda i: (0, 0))],
        out_specs=[tab, tab],
        out_shape=[jax.ShapeDtypeStruct((T, LANES), F32)] * 2,
        compiler_params=_cparams(1),
        name="rope_tables",
    )(positions, invf_lanes)


def _norm_matmul_body(*refs, rope_slabs, slab_major):
    if rope_slabs:
        x_ref, g_ref, w_ref, cos_ref, sin_ref, o_ref, hn_ref = refs
    else:
        x_ref, g_ref, w_ref, o_ref, hn_ref = refs
    j = pl.program_id(1)
    half = ROPE_DIMS // 2
    d = lax.broadcasted_iota(jnp.int32, (1, LANES), 1) % HEAD_DIM

    @pl.when(j == 0)
    def _():
        hn_ref[...] = _rms(x_ref[...], g_ref[...]).astype(BF16)

    acc = _dot(hn_ref[...], w_ref[...])
    if not rope_slabs:
        o_ref[...] = acc.astype(o_ref.dtype)
        return
    for c, roped in enumerate(rope_slabs):
        t = acc[:, c * LANES:(c + 1) * LANES]
        if roped:
            partner = jnp.where(d < half, pltpu.roll(t, LANES - half, 1), pltpu.roll(t, half, 1))
            t = t * cos_ref[...] + partner * sin_ref[...]
        if slab_major:
            o_ref[c] = t.astype(o_ref.dtype)
        else:
            o_ref[:, c * LANES:(c + 1) * LANES] = t.astype(o_ref.dtype)


def _norm_matmul(x2, g, w_bf, tn, rope=None, rope_slabs=(), out_dtype=F32, slab_major=False):
    T, D = x2.shape
    N = w_bf.shape[1]
    tm = min(T, 1024)
    assert len(rope_slabs) in (0, tn // LANES)
    if slab_major:
        out_spec = pl.BlockSpec((tn // LANES, tm, LANES), lambda i, j: (j, i, 0))
        out_shape = jax.ShapeDtypeStruct((N // LANES, T, LANES), out_dtype)
    else:
        out_spec = pl.BlockSpec((tm, tn), lambda i, j: (i, j))
        out_shape = jax.ShapeDtypeStruct((T, N), out_dtype)
    in_specs = [pl.BlockSpec((tm, D), lambda i, j: (i, 0)),
                pl.BlockSpec((1, D), lambda i, j: (0, 0)),
                pl.BlockSpec((D, tn), lambda i, j: (0, j))]
    args = [x2, g.reshape(1, D), w_bf]
    scratch = [pltpu.VMEM((tm, D), BF16)]
    if rope_slabs:
        in_specs += [pl.BlockSpec((tm, LANES), lambda i, j: (i, 0))] * 2
        args += list(rope)
    return pl.pallas_call(
        functools.partial(_norm_matmul_body, rope_slabs=tuple(rope_slabs), slab_major=slab_major),
        grid=(T // tm, N // tn),
        in_specs=in_specs,
        out_specs=out_spec,
        out_shape=out_shape,
        scratch_shapes=scratch,
        compiler_params=_cparams(2),
        name="norm_matmul_rope" if rope_slabs else "norm_matmul",
    )(*args)


def _dilated_body(q_ref, kc_ref, kp_ref, vc_ref, vp_ref, o_ref, lse_ref, *, dilation, nbs, width):
    n = pl.program_id(2)
    blk = DIL_BLOCK
    unit = blk * dilation
    n_lane_heads = LANES // HEAD_DIM
    lane_head = lax.broadcasted_iota(jnp.int32, (1, LANES), 1) // HEAD_DIM
    qi = lax.broadcasted_iota(jnp.int32, (blk, blk), 0)
    kj = lax.broadcasted_iota(jnp.int32, (blk, blk), 1)
    cur_ok = kj <= qi
    first_prev_ok = (kj - qi) >= jnp.where(n > 0, 0, 2 * blk)
    later_prev_ok = kj >= qi
    ones = jnp.ones((blk, LANES), BF16)

    def rows_of(j, r):
        start = j * unit + r
        return pl.ds(start, blk) if dilation == 1 else pl.ds(start, blk, stride=dilation)

    def process(units):
        loaded = []
        for j, r in units:
            rows = rows_of(j, r)
            prows = rows_of(max(j - 1, 0), r)
            kpr, vpr = (kp_ref, vp_ref) if j == 0 else (kc_ref, vc_ref)
            loaded.append((rows, first_prev_ok if j == 0 else later_prev_ok,
                           q_ref[0, rows, :] * (HEAD_DIM ** -0.5),
                           kc_ref[0, rows, :].astype(BF16), kpr[0, prows, :].astype(BF16),
                           jnp.concatenate([vc_ref[0, rows, :].astype(BF16), ones], axis=1),
                           jnp.concatenate([vpr[0, prows, :].astype(BF16), ones], axis=1)))
        heads = [(u, h) for u in range(len(units)) for h in range(n_lane_heads)]
        scores = []
        for u, h in heads:
            _, prev_ok, q, kc, kp, _, _ = loaded[u]
            qh = jnp.where(lane_head == h, q, 0.0).astype(BF16)
            scores.append((jnp.where(cur_ok, _dot_nt(qh, kc), -jnp.inf), jnp.where(prev_ok, _dot_nt(qh, kp), -jnp.inf)))
        probs = []
        for sc, sp in scores:
            m = jnp.max(jnp.maximum(sc, sp), axis=-1, keepdims=True)
            probs.append((m, jnp.exp(sc - m).astype(BF16), jnp.exp(sp - m).astype(BF16)))
        outs = []
        for (u, h), (m, pc, pp) in zip(heads, probs):
            ext = _dot(pc, loaded[u][5]) + _dot(pp, loaded[u][6])
            den = ext[:, LANES:]
            outs.append((ext[:, :LANES] / den, m + jnp.log(den)))
        for u in range(len(units)):
            o_acc = jnp.zeros((blk, LANES), F32)
            lse_acc = jnp.zeros((blk, LANES), F32)
            for h in range(n_lane_heads):
                oh, lse = outs[u * n_lane_heads + h]
                o_acc = jnp.where(lane_head == h, oh, o_acc)
                lse_acc = jnp.where(lane_head == h, lse, lse_acc)
            o_ref[0, loaded[u][0], :] = o_acc
            lse_ref[0, loaded[u][0], :] = lse_acc

    if dilation == 1:
        for j in range(0, nbs, width):
            process([(j + w, 0) for w in range(width)])
    else:
        step = dilation // width
        for j in range(nbs):
            def body(r, _, j=j):
                process([(j, r + w * step) for w in range(width)])
                return 0
            lax.fori_loop(0, step, body, 0)


def _dilated_group(seg_a, B, g, dilation):
    _, T, _ = seg_a.shape
    S = T // B
    unit = DIL_BLOCK * dilation
    width = 8 if dilation == 1 else 4
    nbs = max(1, width * DIL_BLOCK // unit)
    tr = unit * nbs
    slab = 2 * (g * 3)

    def cur(which):
        return pl.BlockSpec((1, tr, LANES), lambda b, p, n: (slab + 2 * which + p, b * (S // tr) + n, 0))

    def prev(which):
        return pl.BlockSpec((1, unit, LANES),
                            lambda b, p, n: (slab + 2 * which + p, b * (S // unit) + jnp.maximum(n * nbs - 1, 0), 0))

    out_spec = pl.BlockSpec((1, tr, LANES), lambda b, p, n: (p, b * (S // tr) + n, 0))
    return pl.pallas_call(
        functools.partial(_dilated_body, dilation=dilation, nbs=nbs, width=width),
        grid=(B, MIX_W // LANES, S // tr),
        in_specs=[cur(0), cur(1), prev(1), cur(2), prev(2)],
        out_specs=[out_spec, out_spec],
        out_shape=[jax.ShapeDtypeStruct((MIX_W // LANES, T, LANES), F32)] * 2,
        compiler_params=_cparams(3),
        name="dilated_attention",
    )(seg_a, seg_a, seg_a, seg_a, seg_a)


def _diff_body(q_ref, k_ref, v_ref, lam_ref, g_ref, o_ref, vt_ref, a_ref, sa_ref, sb_ref, *, tq, kc, lam_init, nh):
    qi = pl.program_id(2)
    n_kv = vt_ref.shape[1]
    hw = 2 * HEAD_DIM
    hs = range(nh)

    @pl.when(qi == 0)
    def _():
        for hh in hs:
            for c in range(n_kv):
                vt_ref[hh, c] = v_ref[hh, c * kc:(c + 1) * kc, :].astype(F32).T.astype(BF16)

    lane = lax.broadcasted_iota(jnp.int32, (1, hw), 1)
    qs = []
    for hh in hs:
        q = q_ref[hh].astype(F32) * (HEAD_DIM ** -0.5 * LOG2E)
        qs.append((jnp.where(lane < HEAD_DIM, q, 0.0).astype(BF16), jnp.where(lane >= HEAD_DIM, q, 0.0).astype(BF16)))
    key = lax.broadcasted_iota(jnp.int32, (kc, tq), 0)
    qry = lax.broadcasted_iota(jnp.int32, (kc, tq), 1)
    a_ref[...] = jnp.zeros_like(a_ref)
    chains = [(hh, j) for hh in hs for j in range(2)]
    n_c = len(chains)

    def scores(kb, s_ref):
        rows = pl.ds(pl.multiple_of(kb * kc, kc), kc)
        for hh, j in chains:
            s_ref[hh, j] = _dot_nt(k_ref[hh, rows, :], qs[hh][j])

    def softmax_pv(kb, s_ref, carry, masked):
        s = [s_ref[hh, j] for hh, j in chains]
        if masked:
            s = [jnp.where(key <= qry, x, -jnp.inf) for x in s]
        m_old = [carry[2 * i] for i in range(n_c)]
        l_old = [carry[2 * i + 1] for i in range(n_c)]
        m_new = [jnp.maximum(m_old[i], jnp.max(s[i], axis=0, keepdims=True)) for i in range(n_c)]
        p = [jnp.exp2(s[i] - m_new[i]) for i in range(n_c)]
        al = [jnp.exp2(m_old[i] - m_new[i]) for i in range(n_c)]
        pv = [_dot(vt_ref[chains[i][0], kb], p[i].astype(BF16)) for i in range(n_c)]
        out = []
        for i, (hh, j) in enumerate(chains):
            l_new = l_old[i] * al[i] + jnp.sum(p[i].reshape(kc // 8, 8, tq), axis=0)
            a_ref[hh, j] = a_ref[hh, j] * al[i] + pv[i]
            out += [m_new[i], l_new]
        return tuple(out)

    def pair(i, carry):
        scores(2 * i + 1, sb_ref)
        carry = softmax_pv(2 * i, sa_ref, carry, False)
        scores(2 * i + 2, sa_ref)
        return softmax_pv(2 * i + 1, sb_ref, carry, False)

    def finish(carry):
        lv = lam_ref[...]
        lam = (jnp.exp(jnp.sum(lv[0:1] * lv[1:2], axis=-1, keepdims=True))
               - jnp.exp(jnp.sum(lv[2:3] * lv[3:4], axis=-1, keepdims=True)) + lam_init)
        for hh in hs:
            l1 = carry[4 * hh + 1]
            l2 = carry[4 * hh + 3]
            o_t = (a_ref[hh, 0] / jnp.sum(l1, axis=0, keepdims=True)
                   - lam * (a_ref[hh, 1] / jnp.sum(l2, axis=0, keepdims=True)))
            o_ref[0, :, hh * hw:(hh + 1) * hw] = _rms(o_t.T, g_ref[...]) * (1.0 - lam_init)

    neg = jnp.full((1, tq), -jnp.inf, F32)
    zero = jnp.zeros((8, tq), F32)
    init = (neg, zero) * n_c
    scores(0, sa_ref)

    @pl.when(qi % 2 == 0)
    def _():
        carry = lax.fori_loop(0, qi // 2, pair, init)
        finish(softmax_pv(qi, sa_ref, carry, True))

    @pl.when(qi % 2 == 1)
    def _():
        carry = lax.fori_loop(0, qi // 2, pair, init)
        scores(qi, sb_ref)
        carry = softmax_pv(qi - 1, sa_ref, carry, False)
        finish(softmax_pv(qi, sb_ref, carry, True))


def _diff_attention(seg_b, B, lam_vecs, norm_g, lam_init, tq=512, nh=2):
    _, T, hw = seg_b.shape
    S = T // B
    tq = min(S, tq)
    kc = tq
    ng = N_HEADS // nh
    return pl.pallas_call(
        functools.partial(_diff_body, tq=tq, kc=kc, lam_init=lam_init, nh=nh),
        grid=(B, ng, S // tq),
        in_specs=[pl.BlockSpec((nh, tq, hw), lambda b, h, i: (h, b * (S // tq) + i, 0)),
                  pl.BlockSpec((nh, S, hw), lambda b, h, i: (ng + h, b, 0)),
                  pl.BlockSpec((nh, S, hw), lambda b, h, i: (2 * ng + h, b, 0)),
                  pl.BlockSpec((4, HEAD_DIM), lambda b, h, i: (0, 0)),
                  pl.BlockSpec((1, hw), lambda b, h, i: (0, 0))],
        out_specs=pl.BlockSpec((1, tq, nh * hw), lambda b, h, i: (b, i, h)),
        out_shape=jax.ShapeDtypeStruct((B, S, DIFF_W), F32),
        scratch_shapes=[pltpu.VMEM((nh, S // kc, hw, kc), BF16), pltpu.VMEM((nh, 2, hw, tq), F32),
                        pltpu.VMEM((nh, 2, kc, tq), F32), pltpu.VMEM((nh, 2, kc, tq), F32)],
        compiler_params=_cparams(3),
        name="diff_attention",
    )(seg_b, seg_b, seg_b, lam_vecs, norm_g.reshape(1, hw))


def _hgrn_body(x_ref, xg_ref, w_ref, la_ref, lc_ref, oml_ref, ng_ref, o_ref, st_ref, seg_ref, *, tr, unroll):
    t = pl.program_id(1)
    c = SUB
    W = MIX_W

    @pl.when(t == 0)
    def _():
        st_ref[...] = jnp.zeros_like(st_ref)

    seg_ref[...] = _dot(_rms(x_ref[0], xg_ref[...]).astype(BF16), w_ref[...])

    e_bf = jnp.where(_head_sum_matrix(W), 1.0, 0.0).astype(BF16)
    row = lax.broadcasted_iota(jnp.int32, (c, W), 0)
    head = lax.broadcasted_iota(jnp.int32, (1, W), 1) // HEAD_DIM

    def stack(x):
        return jnp.concatenate([jnp.where(head == h, x, 0.0) for h in range(N_HEADS)], axis=0).astype(BF16)

    def group(j, _):
        base = j * (c * unroll)
        rows = [pl.ds(pl.multiple_of(base + u * c, c), c) for u in range(unroll)]
        us = range(unroll)
        qr = [seg_ref[r, 0:W] for r in rows]
        f = [seg_ref[r, W:2 * W] for r in rows]
        iv = [seg_ref[r, 2 * W:3 * W] for r in rows]
        gr = [seg_ref[r, 3 * W:4 * W] for r in rows]
        q = [x * _sigmoid(x) for x in qr]
        log_sig = [jnp.minimum(x, 0.0) - jnp.log1p(jnp.exp(-jnp.abs(x))) for x in f]
        x1 = la_ref[...]
        x2 = [lc_ref[...] + x for x in log_sig]
        log_f = [jnp.maximum(x1, x) + jnp.log1p(jnp.exp(-jnp.abs(x1 - x))) for x in x2]
        k = [oml_ref[...] * _sigmoid(-x) for x in f]
        b = [_cumsum_rows(x) for x in log_f]
        qe_st = [stack(q[u] * jnp.exp(b[u])) for u in us]
        b2 = [x * LOG2E for x in b]
        pmat = []
        for u in us:
            ps = []
            for s in range(c):
                e = jnp.exp2(jnp.where(row >= s, b2[u] - b2[u][s:s + 1], -jnp.inf))
                ps.append((q[u] * e * k[u][s:s + 1]).astype(BF16))
            pmat.append(jnp.concatenate(ps, axis=0))
        a = [_dot(p, e_bf) for p in pmat]
        o_intra = []
        for u in us:
            o = a[u][0:c] * iv[u][0:1]
            for s in range(1, c):
                o = o + a[u][s * c:(s + 1) * c] * iv[u][s:s + 1]
            o_intra.append(o)
        b_last = [x[c - 1:c] for x in b]
        kt_st = [stack(k[u] * jnp.exp(b_last[u] - b[u])) for u in us]
        i_heads = [jnp.concatenate([x[:, h * HEAD_DIM:(h + 1) * HEAD_DIM] for h in range(N_HEADS)], axis=0).astype(BF16)
                   for x in iv]
        upd = [_dot_tn(i_heads[u], kt_st[u]) for u in us]
        dec = [jnp.exp(x) for x in b_last]
        st = st_ref[...]
        o_heads = []
        for u in us:
            o_heads.append(_dot_nt(qe_st[u], st.astype(BF16)))
            st = st * dec[u] + upd[u]
        st_ref[...] = st
        for u in us:
            o = o_intra[u] + jnp.concatenate([o_heads[u][h * c:(h + 1) * c] for h in range(N_HEADS)], axis=1)
            ms = _sum_heads(o * o, e_bf) * (1.0 / HEAD_DIM)
            o_ref[0, rows[u], :] = o * lax.rsqrt(ms + NORM_EPS) * ng_ref[...] * (gr[u] * _sigmoid(gr[u]))
        return 0

    lax.fori_loop(0, tr // (c * unroll), group, 0)


def _hgrn(x, mix_g, w_c, lb, norm_g, tr=256, unroll=16):
    B, S, D = x.shape
    tr = min(S, tr)
    W = MIX_W
    vec = pl.BlockSpec((1, W), lambda b, t: (0, 0))
    lb = lb.reshape(1, W)
    return pl.pallas_call(
        functools.partial(_hgrn_body, tr=tr, unroll=unroll),
        grid=(B, S // tr),
        in_specs=[pl.BlockSpec((1, tr, D), lambda b, t: (b, t, 0)),
                  pl.BlockSpec((1, D), lambda b, t: (0, 0)),
                  pl.BlockSpec((D, SEG_C), lambda b, t: (0, 0)),
                  vec, vec, vec, vec],
        out_specs=pl.BlockSpec((1, tr, W), lambda b, t: (b, t, 0)),
        out_shape=jax.ShapeDtypeStruct((B, S, W), F32),
        scratch_shapes=[pltpu.VMEM((HEAD_DIM, W), F32), pltpu.VMEM((tr, SEG_C), F32)],
        compiler_params=_cparams(2),
        name="hgrn2",
    )(x, mix_g.reshape(1, D), w_c, jnp.log(lb), jnp.log1p(-lb), 1.0 - lb,
      jnp.tile(norm_g.reshape(1, HEAD_DIM), (1, N_HEADS)))


def _rwkv_body(*refs, tr, first_layer):
    if first_layer:
        (x_ref, mu_ref, w0_ref, w2_ref, a0_ref, a2_ref, g2_ref, kk_ref, ka_ref, rk_ref, lg_ref, lb_ref,
         xg_ref, wd_ref,
         y_ref, vf_out_ref,
         carry_ref, st_ref) = refs
    else:
        (x_ref, vf_ref, mu_ref, w0_ref, w2_ref, a0_ref, a2_ref, g2_ref, kk_ref, ka_ref, rk_ref, lg_ref, lb_ref,
         v0_ref, v1_ref, v2_ref,
         xg_ref, wd_ref,
         y_ref,
         carry_ref, st_ref) = refs
    t = pl.program_id(1)
    C = CHUNK
    W = MIX_W

    @pl.when(t == 0)
    def _():
        st_ref[...] = jnp.zeros_like(st_ref)
        carry_ref[...] = jnp.zeros_like(carry_ref)

    e_bf = jnp.where(_head_sum_matrix(W), 1.0, 0.0).astype(BF16)

    xs = _dot(_rms(x_ref[0], xg_ref[...]).astype(BF16), wd_ref[...])
    rowi = lax.broadcasted_iota(jnp.int32, xs.shape, 0)
    prev = jnp.where(rowi == 0, carry_ref[0:1, :], pltpu.roll(xs, 1, 0))
    carry_ref[0:1, :] = xs[tr - 1:tr, :]
    xm = xs + (prev - xs) * mu_ref[...]
    r = xm[:, 0:W]
    k = xm[:, W:2 * W]
    v = xm[:, 2 * W:3 * W]
    w_low = xm[:, 3 * W:3 * W + 64]
    a_low = xm[:, 3 * W + 64:3 * W + 128]
    g_low = xm[:, 3 * W + 128:3 * W + 256]
    wlog = -_softplus(-(w0_ref[...] + _dot(jnp.tanh(w_low).astype(BF16), w2_ref[...]))) - 0.5
    a = _sigmoid(a0_ref[...] + _dot(a_low.astype(BF16), a2_ref[...]))
    g = _dot(_sigmoid(g_low).astype(BF16), g2_ref[...])
    kkr = k * kk_ref[...]
    kn = kkr / jnp.maximum(jnp.sqrt(_sum_heads(kkr * kkr, e_bf)), 1e-12)
    k = k * (1.0 + (a - 1.0) * ka_ref[...])
    if first_layer:
        vf_out_ref[0] = v
    else:
        mix = _dot(_dot(v.astype(BF16), v1_ref[...]).astype(BF16), v2_ref[...])
        v = v + (vf_ref[0] - v) * _sigmoid(v0_ref[...] + mix)

    HP = LANES // HEAD_DIM
    hc = HP * C
    n_pairs = W // LANES
    tt = lax.broadcasted_iota(jnp.int32, (hc, hc), 0) % C
    ss = lax.broadcasted_iota(jnp.int32, (hc, hc), 1) % C
    strict = ss < tt
    incl = ss <= tt
    eye = jnp.where(lax.broadcasted_iota(jnp.int32, (hc, hc), 0) == lax.broadcasted_iota(jnp.int32, (hc, hc), 1),
                    1.0, 0.0)
    same_head_p = _head_sum_matrix(LANES)
    lane_head = lax.broadcasted_iota(jnp.int32, (1, LANES), 1) // HEAD_DIM

    def stack(x):
        return jnp.concatenate([jnp.where(lane_head == j, x, 0.0) for j in range(HP)], axis=0).astype(BF16)

    def fold(x):
        return x[0:C] + x[C:2 * C]

    n_ch = tr // C
    units = [(c, p) for c in range(n_ch) for p in range(n_pairs)]
    sl = lambda x, u: x[u[0] * C:(u[0] + 1) * C, u[1] * LANES:(u[1] + 1) * LANES]
    idx = range(len(units))
    lw_all = -jnp.exp(wlog)
    bb = kn * a
    cs_c = [_cumsum_rows(lw_all[c * C:(c + 1) * C]) for c in range(n_ch)]
    cs = [cs_c[u[0]][:, u[1] * LANES:(u[1] + 1) * LANES] for u in units]
    r_st = [stack(sl(r, units[i]) * jnp.exp(cs[i])) for i in idx]
    n_st = [stack(sl(kn, units[i]) * jnp.exp(cs[i] - sl(lw_all, units[i]))) for i in idx]
    kb_st = [jnp.concatenate([stack(sl(k, units[i]) * jnp.exp(-cs[i])), stack(sl(bb, units[i]) * jnp.exp(-cs[i]))],
                             axis=0) for i in idx]
    v_st = [stack(sl(v, units[i])) for i in idx]
    gram_n = [_dot_nt(n_st[i], kb_st[i]) for i in idx]
    gram_r = [_dot_nt(r_st[i], kb_st[i]) for i in idx]
    l_k = [jnp.where(strict, g_[:, :hc], 0.0).astype(BF16) for g_ in gram_n]
    l_b = [jnp.where(strict, g_[:, hc:], 0.0) for g_ in gram_n]
    m_k = [jnp.where(incl, g_[:, :hc], 0.0).astype(BF16) for g_ in gram_r]
    m_b = [jnp.where(incl, g_[:, hc:], 0.0).astype(BF16) for g_ in gram_r]
    inv_t = [eye - x for x in l_b]
    pw = [_dot(x.astype(BF16), x.astype(BF16)) for x in l_b]
    n = 2
    while n < C:
        inv_t = [t_ + _dot(t_.astype(BF16), p_.astype(BF16)) for t_, p_ in zip(inv_t, pw)]
        n *= 2
        if n < C:
            pw = [_dot(p_.astype(BF16), p_.astype(BF16)) for p_ in pw]
    lkv = [_dot(l_k[i], v_st[i]) for i in idx]
    mkv = [_dot(m_k[i], v_st[i]) for i in idx]
    tw = [_dot(inv_t[i].astype(BF16), jnp.concatenate([n_st[i], lkv[i].astype(BF16)], axis=1)) for i in idx]
    mw = [_dot(m_b[i], tw[i].astype(BF16)) for i in idx]
    wnr = [jnp.concatenate([tw[i][:, :LANES], r_st[i].astype(F32) - mw[i][:, :LANES]], axis=0).astype(BF16)
           for i in idx]
    u_a = [fold(tw[i][:, LANES:]) for i in idx]
    y_a = [fold(mkv[i] - mw[i][:, LANES:]) for i in idx]
    st = [st_ref[p] for p in range(n_pairs)]
    ys = [[None] * n_pairs for _ in range(n_ch)]
    for i, (c, p) in enumerate(units):
        x = _dot_nt(wnr[i], st[p].astype(BF16))
        u = fold(x[:hc]) + u_a[i]
        ys[c][p] = fold(x[hc:]) + y_a[i]
        c_last = cs[i][C - 1:C, :]
        dec = jnp.exp(c_last - cs[i])
        vu = jnp.concatenate([sl(v, (c, p)), -u], axis=0).astype(BF16)
        kb_end = jnp.concatenate([sl(k, (c, p)) * dec, sl(bb, (c, p)) * dec], axis=0).astype(BF16)
        st[p] = st[p] * jnp.exp(c_last) + jnp.where(same_head_p, _dot_tn(vu, kb_end), 0.0)
    for p in range(n_pairs):
        st_ref[p] = st[p]
    y = jnp.concatenate([jnp.concatenate(row_, axis=1) for row_ in ys], axis=0)

    mean = _sum_heads(y, e_bf) * (1.0 / HEAD_DIM)
    yc = y - mean
    var = _sum_heads(yc * yc, e_bf) * (1.0 / HEAD_DIM)
    yn = yc * lax.rsqrt(var + RWKV_GN_EPS) * lg_ref[...] + lb_ref[...]
    bonus = _sum_heads(r * k * rk_ref[...], e_bf) * v
    y_ref[0] = (yn + bonus) * g


def _rwkv(x, mix_g, w_d, p, v_first, tr=512):
    B, S, D = x.shape
    tr = min(S, tr)
    W = MIX_W
    first = v_first is None
    row = lambda a: a.reshape(1, -1)
    full = lambda a: pl.BlockSpec(a.shape, lambda b, t: (0,) * a.ndim)
    tile = lambda w: pl.BlockSpec((1, tr, w), lambda b, t: (b, t, 0))
    params = [row(p['mu']), row(p['w0']), p['w2'].astype(BF16), row(p['a0']), p['a2'].astype(BF16),
              p['g2'].astype(BF16), row(p['k_k']), row(p['k_a']), row(p['r_k']), row(p['lnx_g']), row(p['lnx_b'])]
    args = [x]
    in_specs = [tile(D)]
    if not first:
        args.append(v_first)
        in_specs.append(tile(W))
        params += [row(p['v0']), p['v1'].astype(BF16), p['v2'].astype(BF16)]
    params += [row(mix_g), w_d]
    args += params
    in_specs += [full(a) for a in params]
    n_out = 2 if first else 1
    outs = pl.pallas_call(
        functools.partial(_rwkv_body, tr=tr, first_layer=first),
        grid=(B, S // tr),
        in_specs=in_specs,
        out_specs=[tile(W)] * n_out,
        out_shape=[jax.ShapeDtypeStruct((B, S, W), F32)] * n_out,
        scratch_shapes=[pltpu.VMEM((8, SEG_D), F32), pltpu.VMEM((W // LANES, LANES, LANES), F32)],
        compiler_params=_cparams(2),
        name="rwkv7",
    )(*args)
    return (outs[0], outs[1]) if first else (outs[0], v_first)


def _merge_body(o0, l0, o1, l1, o2, l2, yb, yc, yd, x_ref, ng_ref, wg, pa, pb, pc, pd, wo, out_ref):
    x = x_ref[...]
    D = x.shape[-1]
    hn = _rms(x, ng_ref[...]).astype(BF16)
    wide = lambda ref: jnp.concatenate([ref[0], ref[1]], axis=1)
    la, lb, lc = wide(l0), wide(l1), wide(l2)
    m = jnp.maximum(jnp.maximum(la, lb), lc)
    e0, e1, e2 = jnp.exp(la - m), jnp.exp(lb - m), jnp.exp(lc - m)
    inv = 1.0 / (e0 + e1 + e2)
    y_a = (e0 * inv) * wide(o0) + (e1 * inv) * wide(o1) + (e2 * inv) * wide(o2)
    merged = jnp.zeros_like(x)
    for j, (y, p) in enumerate(((y_a, pa), (yb[...], pb), (yc[...], pc), (yd[...], pd))):
        gate = _sigmoid(_dot(hn, wg[:, j * D:(j + 1) * D]))
        merged = merged + gate * _dot(y.astype(BF16), p[...])
    out_ref[...] = x + _dot(merged.astype(BF16), wo[...])


def _merge(dil, y_b, y_c, y_d, x2, norm_g, w_gate, p_a, p_b, p_c, p_d, w_out):
    T, D = x2.shape
    tm = min(T, 512)
    rows = lambda w: pl.BlockSpec((tm, w), lambda i: (i, 0))
    full = lambda a: pl.BlockSpec(a.shape, lambda i: (0, 0))
    flat = lambda a: a.reshape(T, a.shape[-1])
    acts = []
    for o, lse in dil:
        acts += [o, lse]
    acts += [flat(y_b), flat(y_c), flat(y_d)]
    weights = [norm_g.reshape(1, D), w_gate] + [w.astype(BF16) for w in (p_a, p_b, p_c, p_d, w_out)]
    slabs = pl.BlockSpec((MIX_W // LANES, tm, LANES), lambda i: (0, i, 0))
    in_specs = ([slabs] * 6 + [rows(DIFF_W), rows(MIX_W), rows(MIX_W), rows(D)] + [full(w) for w in weights])
    return pl.pallas_call(
        _merge_body,
        grid=(T // tm,),
        in_specs=in_specs,
        out_specs=rows(D),
        out_shape=jax.ShapeDtypeStruct((T, D), F32),
        compiler_params=_cparams(1),
        name="gated_merge",
    )(*acts, x2, *weights)


def _mem_body(x_ref, g_ref, wq_ref, kv_ref, wo_ref, out_ref, *, n_heads):
    x = x_ref[0]
    D = x.shape[-1]
    dm = D // n_heads
    q = _dot(_rms(x, g_ref[...]).astype(BF16), wq_ref[...]) * (dm ** -0.5)
    kv = kv_ref[0].astype(BF16)
    outs = []
    for h in range(n_heads):
        s = _dot_nt(q[:, h * dm:(h + 1) * dm].astype(BF16), kv[:, h * dm:(h + 1) * dm])
        p = jnp.exp(s - jnp.max(s, axis=-1, keepdims=True))
        p = p / jnp.sum(p, axis=-1, keepdims=True)
        outs.append(_dot(p.astype(BF16), kv[:, D + h * dm:D + (h + 1) * dm]))
    o = jnp.concatenate(outs, axis=-1)
    out_ref[0] = x + _dot(o.astype(BF16), wo_ref[...])


def _mem_attention(x, g, w_q, kv, w_o, n_heads=4):
    B, S, D = x.shape
    M = kv.shape[1]
    tm = min(S, 512)
    full = lambda a: pl.BlockSpec(a.shape, lambda b, i: (0, 0))
    wq, wo = w_q.astype(BF16), w_o.astype(BF16)
    g = g.reshape(1, D)
    return pl.pallas_call(
        functools.partial(_mem_body, n_heads=n_heads),
        grid=(B, S // tm),
        in_specs=[pl.BlockSpec((1, tm, D), lambda b, i: (b, i, 0)), full(g), full(wq),
                  pl.BlockSpec((1, M, 2 * D), lambda b, i: (b, 0, 0)), full(wo)],
        out_specs=pl.BlockSpec((1, tm, D), lambda b, i: (b, i, 0)),
        out_shape=jax.ShapeDtypeStruct((B, S, D), F32),
        compiler_params=_cparams(2),
        name="mem_attention",
    )(x, g, wq, kv, wo)


def _ffn_body(x_ref, halo_ref, g_ref, wg_ref, wv_ref, cwg_ref, cwv_ref, cbg_ref, cbv_ref, wo_ref, fg_ref,
              out_ref, hn_ref, hh_ref, acc_ref, *, final_norm):
    i = pl.program_id(1)
    c = pl.program_id(2)
    tm = x_ref.shape[1]

    @pl.when(c == 0)
    def _():
        hn_ref[...] = _rms(x_ref[0], g_ref[...]).astype(BF16)
        hh_ref[...] = _rms(halo_ref[0], g_ref[...]).astype(BF16)
        acc_ref[...] = jnp.zeros_like(acc_ref)

    live = jnp.where(i > 0, 1.0, 0.0)
    row = lax.broadcasted_iota(jnp.int32, (tm, 1), 0)

    def conv(w_ref, cw_ref, cb_ref):
        u = _dot(hn_ref[...], w_ref[...])
        uh = _dot(hh_ref[...], w_ref[...]) * live
        u1 = jnp.where(row == 0, uh[7:8], pltpu.roll(u, 1, 0))
        u2 = jnp.where(row == 0, uh[6:7], jnp.where(row == 1, uh[7:8], pltpu.roll(u, 2, 0)))
        cw = cw_ref[...]
        return cb_ref[...] + u2 * cw[0:1] + u1 * cw[1:2] + u * cw[2:3]

    gate = conv(wg_ref, cwg_ref, cbg_ref)
    val = conv(wv_ref, cwv_ref, cbv_ref)
    act = (gate * _sigmoid(gate) * val).astype(BF16)
    acc_ref[...] += _dot(act, wo_ref[...])

    @pl.when(c == pl.num_programs(2) - 1)
    def _():
        y = x_ref[0] + acc_ref[...]
        out_ref[0] = _rms(y, fg_ref[...]) if final_norm else y


def _ffn(x, g, w_in, conv_w, conv_b, w_out, final_g):
    B, S, D = x.shape
    d_ff = w_out.shape[0]
    tm = min(S, 1024)
    fc = d_ff
    nf = d_ff // fc
    w_in, w_out = w_in.astype(BF16), w_out.astype(BF16)
    conv_b = conv_b.reshape(1, 2 * d_ff)
    vec = pl.BlockSpec((1, D), lambda b, i, c: (0, 0))
    fg = (final_g if final_g is not None else g).reshape(1, D)
    return pl.pallas_call(
        functools.partial(_ffn_body, final_norm=final_g is not None),
        grid=(B, S // tm, nf),
        in_specs=[pl.BlockSpec((1, tm, D), lambda b, i, c: (b, i, 0)),
                  pl.BlockSpec((1, 8, D), lambda b, i, c: (b, jnp.maximum(i * (tm // 8) - 1, 0), 0)),
                  vec,
                  pl.BlockSpec((D, fc), lambda b, i, c: (0, c)),
                  pl.BlockSpec((D, fc), lambda b, i, c: (0, nf + c)),
                  pl.BlockSpec((3, fc), lambda b, i, c: (0, c)),
                  pl.BlockSpec((3, fc), lambda b, i, c: (0, nf + c)),
                  pl.BlockSpec((1, fc), lambda b, i, c: (0, c)),
                  pl.BlockSpec((1, fc), lambda b, i, c: (0, nf + c)),
                  pl.BlockSpec((fc, D), lambda b, i, c: (c, 0)),
                  vec],
        out_specs=pl.BlockSpec((1, tm, D), lambda b, i, c: (b, i, 0)),
        out_shape=jax.ShapeDtypeStruct((B, S, D), F32),
        scratch_shapes=[pltpu.VMEM((tm, D), BF16), pltpu.VMEM((8, D), BF16), pltpu.VMEM((tm, D), F32)],
        compiler_params=_cparams(3),
        name="conv_ffn",
    )(x, x, g.reshape(1, D), w_in, w_in, conv_w, conv_w, conv_b, conv_b, w_out, fg)


def kernel(x, mem, positions, mix_norm_g, w_in, diff_lam, diff_norm_g, hgrn_lb_logits, hgrn_norm_g, rwkv_mu, rwkv_w0, rwkv_w2, rwkv_a0, rwkv_a2, rwkv_g2, rwkv_k_k, rwkv_k_a, rwkv_r_k, rwkv_lnx_g, rwkv_lnx_b, rwkv_v0, rwkv_v1, rwkv_v2, p_a, p_b, p_c, p_d, w_mix_out, mem_q_norm_g, mem_kv_norm_g, w_mem_q, w_mem_kv, w_mem_o, ffn_norm_g, w_ffn_in, ffn_conv_w, ffn_conv_b, w_ffn_out, final_norm_g):
    B, S, D = x.shape
    M = mem.shape[1]
    T = B * S
    depth = w_in.shape[0]
    assert S % (DIL_PATTERNS[-1][1] * DIL_BLOCK) == 0 and S % CHUNK == 0

    half = ROPE_DIMS // 2
    inv_freq = ROPE_THETA ** (-jnp.arange(half, dtype=F32) / half)
    d = jnp.arange(LANES) % HEAD_DIM
    invf_lanes = jnp.where(d < ROPE_DIMS, inv_freq[d % half], 0.0).reshape(1, LANES)
    rope = _rope_tables(positions.reshape(T, 1), invf_lanes)
    lb_all = jnp.cumsum(jax.nn.softmax(hgrn_lb_logits.astype(F32), axis=0), axis=0)
    lb_all = lb_all - lb_all[0:1]
    offs = (0, SEG_A, SEG_A + SEG_B, SEG_A + SEG_B + SEG_C, SEG_A + SEG_B + SEG_C + SEG_D, w_in.shape[2])
    mem2 = mem.reshape(B * M, D)
    qkv_slabs = lambda w: (True,) * (2 * w // LANES) + (False,) * (w // LANES)

    v_first = None
    for l in range(depth):
        lam_init = 0.8 - 0.6 * math.exp(-0.3 * l)
        w_l = w_in[l].astype(BF16)
        x2 = x.reshape(T, D)
        seg = lambda s: w_l[:, offs[s]:offs[s + 1]]
        seg_a = _norm_matmul(x2, mix_norm_g[l], seg(0), 3 * MIX_W, rope, qkv_slabs(MIX_W), slab_major=True)
        seg_b = _norm_matmul(x2, mix_norm_g[l], seg(1), SEG_B, rope, qkv_slabs(DIFF_W), BF16, slab_major=True)
        dil = [_dilated_group(seg_a, B, g, dilation) for g, (_, dilation) in enumerate(DIL_PATTERNS)]
        y_b = _diff_attention(seg_b, B, diff_lam[l], diff_norm_g[l], lam_init)
        y_c = _hgrn(x, mix_norm_g[l], seg(2), lb_all[l], hgrn_norm_g[l])
        rp = dict(mu=rwkv_mu[l], w0=rwkv_w0[l], w2=rwkv_w2[l], a0=rwkv_a0[l], a2=rwkv_a2[l], g2=rwkv_g2[l],
                  k_k=rwkv_k_k[l], k_a=rwkv_k_a[l], r_k=rwkv_r_k[l], lnx_g=rwkv_lnx_g[l], lnx_b=rwkv_lnx_b[l])
        if l > 0:
            rp.update(v0=rwkv_v0[l - 1], v1=rwkv_v1[l - 1], v2=rwkv_v2[l - 1])
        y_d, v_first = _rwkv(x, mix_norm_g[l], seg(3), rp, v_first)
        x2 = _merge(dil, y_b, y_c, y_d, x2, mix_norm_g[l], seg(4), p_a[l], p_b[l], p_c[l], p_d[l], w_mix_out[l])
        kv = _norm_matmul(mem2, mem_kv_norm_g[l], w_mem_kv[l].astype(BF16), D).reshape(B, M, 2 * D)
        x = _mem_attention(x2.reshape(B, S, D), mem_q_norm_g[l], w_mem_q[l], kv, w_mem_o[l])
        x = _ffn(x, ffn_norm_g[l], w_ffn_in[l], ffn_conv_w[l], ffn_conv_b[l], w_ffn_out[l],
                 final_norm_g if l == depth - 1 else None)
    return x
```
